```python
import math
import jax
import jax.numpy as jnp
from jax import lax
import numpy as np

D_MODEL = 1024
BATCH = 32
SEQ = 2048
DEPTH = 2

F32 = jnp.float32
PLE_DIM = 256
EPS = 1e-6
NEG = -1e30
TINY = 1e-30

SSD_W = D_MODEL
SSD_HEAD_DIM = 64
SSD_H = SSD_W // SSD_HEAD_DIM
SSD_G = 2
SSD_HG = SSD_H // SSD_G
SSD_N = 128
SSD_CONV = 5
SSD_XBC = SSD_W + 2 * SSD_G * SSD_N
SSD_CHUNK = 64

HG_W = D_MODEL
HG_HEAD_DIM = 128
HG_H = HG_W // HG_HEAD_DIM
HG_CHUNK = 16

ML_W = D_MODEL
ML_HEAD_DIM = 128
ML_H = ML_W // ML_HEAD_DIM
ML_CHUNK = 64

N_BRANCH = 3
D_FF = 2816
FFN_CONV = 3

IN_SIZES = (SSD_W, SSD_XBC, 2 * SSD_H,
            HG_W, 2 * HG_W, HG_W, HG_W,
            ML_W, ML_W, ML_W, 2 * ML_H, 2 * ML_H, ML_W,
            N_BRANCH * D_MODEL)
N_IN = sum(IN_SIZES)

kernel_name = 'bidir_hybrid_ssd_hgrn2_mlstm_block'


def _rmsnorm(x, g):
    xf = x.astype(F32)
    y = xf * lax.rsqrt(jnp.mean(xf * xf, axis=-1, keepdims=True) + EPS)
    return (y * g.astype(F32)).astype(x.dtype)


def _group_rmsnorm(x, g, n_groups):
    shp = x.shape
    xf = x.astype(F32).reshape(shp[:-1] + (n_groups, shp[-1] // n_groups))
    xf = xf * lax.rsqrt(jnp.mean(xf * xf, axis=-1, keepdims=True) + EPS)
    return (xf.reshape(shp) * g.astype(F32)).astype(x.dtype)


def _head_layernorm(x, g, n_heads):
    shp = x.shape
    xf = x.astype(F32).reshape(shp[:-1] + (n_heads, shp[-1] // n_heads))
    xf = xf - jnp.mean(xf, axis=-1, keepdims=True)
    xf = xf * lax.rsqrt(jnp.mean(xf * xf, axis=-1, keepdims=True) + EPS)
    return (xf.reshape(shp) * g.astype(F32)).astype(x.dtype)


def _dwconv(x, w, b):
    width = w.shape[0]
    pad = width // 2
    n_s = x.shape[1]
    xp = jnp.pad(x, ((0, 0), (pad, pad), (0, 0)))
    y = b
    for j in range(width):
        y = y + xp[:, j:j + n_s, :] * w[j]
    return y


def _split(a, sizes):
    idx, acc = [], 0
    for n in sizes[:-1]:
        acc += n
        idx.append(acc)
    return jnp.split(a, idx, axis=-1)


def _flip(a):
    return jnp.flip(a, axis=1)


def _chunk(a, size):
    n_b, n_s = a.shape[:2]
    return a.reshape((n_b, n_s // size, size) + a.shape[2:]).swapaxes(0, 1)


def _unchunk(a):
    n_c, n_b, size = a.shape[:3]
    return a.swapaxes(0, 1).reshape((n_b, n_c * size) + a.shape[3:])


def _ssd_scan(x, dt, A, bm, cm):
    x, dt, bm, cm = x.astype(F32), dt.astype(F32), bm.astype(F32), cm.astype(F32)
    n_b, _, n_g, n_hg, n_p = x.shape
    log_a = dt * A.astype(F32)
    xdt = x * dt[..., None]
    mask = jnp.tril(jnp.ones((SSD_CHUNK, SSD_CHUNK), dtype=bool))

    def body(state, inp):
        xc, ac, bc, cc = inp
        acum = jnp.cumsum(ac, axis=1)
        act = jnp.moveaxis(acum, 1, -1)
        seg = jnp.exp(jnp.where(mask, act[..., :, None] - act[..., None, :], NEG))
        cb = jnp.einsum('blgn,bsgn->bgls', cc, bc)
        y = jnp.einsum('bgls,bghls,bsghp->blghp', cb, seg, xc)
        y = y + jnp.einsum('blgn,bghpn->blghp', cc, state) * jnp.exp(acum)[..., None]
        to_end = jnp.exp(acum[:, -1:] - acum)
        state = (state * jnp.exp(acum[:, -1])[..., None, None]
                 + jnp.einsum('bsgn,bsgh,bsghp->bghpn', bc, to_end, xc))
        return state, y

    state0 = jnp.zeros((n_b, n_g, n_hg, n_p, bm.shape[-1]), F32)
    _, ys = lax.scan(body, state0, (_chunk(xdt, SSD_CHUNK), _chunk(log_a, SSD_CHUNK),
                                    _chunk(bm, SSD_CHUNK), _chunk(cm, SSD_CHUNK)))
    return _unchunk(ys)


def _hgrn2_scan(q, k, v, log_f):
    n_b, _, n_h, n_k = q.shape
    n_v = v.shape[-1]
    mask = jnp.tril(jnp.ones((HG_CHUNK, HG_CHUNK), dtype=bool))[None, :, :, None, None]

    def body(state, inp):
        qc, kc, vc, fc = inp
        b = jnp.cumsum(fc, axis=1)
        o = jnp.einsum('blhk,bhkv->blhv', qc * jnp.exp(b), state)
        w = jnp.exp(jnp.where(mask, b[:, :, None] - b[:, None, :], NEG))
        att = jnp.einsum('bthk,btshk,bshk->bhts', qc, w, kc)
        o = o + jnp.einsum('bhts,bshv->bthv', att, vc)
        b_end = b[:, -1]
        state = (state * jnp.exp(b_end)[..., None]
                 + jnp.einsum('bshk,bshv->bhkv', kc * jnp.exp(b_end[:, None] - b), vc))
        return state, o

    state0 = jnp.zeros((n_b, n_h, n_k, n_v), F32)
    _, ys = lax.scan(body, state0, (_chunk(q, HG_CHUNK), _chunk(k, HG_CHUNK),
                                    _chunk(v, HG_CHUNK), _chunk(log_f, HG_CHUNK)))
    return _unchunk(ys)


def _mlstm_scan(q, k, v, ig, log_f):
    n_b, _, n_h, n_d = q.shape
    mask = jnp.tril(jnp.ones((ML_CHUNK, ML_CHUNK), dtype=bool))

    def body(carry, inp):
        c, n, m = carry
        qc, kc, vc, ic, fc = inp
        bt = jnp.moveaxis(jnp.cumsum(fc, axis=1), 1, -1)
        it = jnp.moveaxis(ic, 1, -1)
        dmat = jnp.where(mask, bt[..., :, None] - bt[..., None, :] + it[..., None, :], NEG)
        inter = bt + m[..., None]
        m_t = jnp.maximum(inter, jnp.max(dmat, axis=-1))
        sc = jnp.einsum('bthd,bshd->bhts', qc, kc) * jnp.exp(dmat - m_t[..., None])
        w_inter = jnp.exp(inter - m_t)
        num = (jnp.einsum('bhts,bshd->bthd', sc, vc)
               + jnp.einsum('bht,bthk,bhvk->bthv', w_inter, qc, c))
        den = jnp.sum(sc, axis=-1) + w_inter * jnp.einsum('bthk,bhk->bht', qc, n)
        denom = jnp.maximum(jnp.abs(den), jnp.exp(-m_t))
        h = num / jnp.moveaxis(denom, -1, 1)[..., None]
        m_new = m_t[..., -1]
        decay = jnp.exp(bt[..., -1] + m - m_new)
        wk = jnp.exp(bt[..., -1:] - bt + it - m_new[..., None])
        c = c * decay[..., None, None] + jnp.einsum('bhs,bshv,bshk->bhvk', wk, vc, kc)
        n = n * decay[..., None] + jnp.einsum('bhs,bshk->bhk', wk, kc)
        return (c, n, m_new), h

    carry0 = (jnp.zeros((n_b, n_h, n_d, n_d), F32), jnp.zeros((n_b, n_h, n_d), F32),
              jnp.full((n_b, n_h), NEG, F32))
    _, ys = lax.scan(body, carry0, (_chunk(q, ML_CHUNK), _chunk(k, ML_CHUNK), _chunk(v, ML_CHUNK),
                                    _chunk(ig, ML_CHUNK), _chunk(log_f, ML_CHUNK)))
    return _unchunk(ys)


def _ssd_branch(z, xbc, dt_raw, conv_w, conv_b, dt_bias, a_log, d_skip, norm_g):
    n_b, n_s, _ = z.shape
    xbc = jax.nn.silu(_dwconv(xbc, conv_w, conv_b))
    xs, bm, cm = _split(xbc, (SSD_W, SSD_G * SSD_N, SSD_G * SSD_N))
    xs = xs.reshape(n_b, n_s, SSD_G, SSD_HG, SSD_HEAD_DIM)
    bm = bm.reshape(n_b, n_s, SSD_G, SSD_N)
    cm = cm.reshape(n_b, n_s, SSD_G, SSD_N)
    dt = jax.nn.softplus(dt_raw.astype(F32).reshape(n_b, n_s, 2, SSD_G, SSD_HG)
                         + dt_bias.astype(F32).reshape(2, SSD_G, SSD_HG))
    A = -jnp.exp(a_log.astype(F32)).reshape(2, SSD_G, SSD_HG)
    y_f = _ssd_scan(xs, dt[:, :, 0], A[0], bm, cm)
    y_b = _flip(_ssd_scan(_flip(xs), _flip(dt[:, :, 1]), A[1], _flip(bm), _flip(cm)))
    y = y_f + y_b + xs.astype(F32) * d_skip.astype(F32).reshape(SSD_G, SSD_HG, 1)
    y = y.reshape(n_b, n_s, SSD_W).astype(z.dtype)
    return _group_rmsnorm(y * jax.nn.silu(z), norm_g, SSD_G)


def _hgrn2_branch(q_raw, f_raw, i_in, g_out, lb, norm_g):
    n_b, n_s, _ = q_raw.shape
    hshape = (n_b, n_s, HG_H, HG_HEAD_DIM)
    q = jax.nn.silu(q_raw.astype(F32)).reshape(hshape)
    v = i_in.astype(F32).reshape(hshape)
    f_raw = f_raw.astype(F32).reshape(n_b, n_s, 2, HG_W)
    lb = lb.astype(F32)
    f = lb + (1.0 - lb) * jax.nn.sigmoid(f_raw)
    log_f = jnp.log(jnp.maximum(f, TINY))
    k = (1.0 - lb) * jax.nn.sigmoid(-f_raw)
    log_f = log_f.reshape(n_b, n_s, 2, HG_H, HG_HEAD_DIM)
    k = k.reshape(n_b, n_s, 2, HG_H, HG_HEAD_DIM)
    o_f = _hgrn2_scan(q, k[:, :, 0], v, log_f[:, :, 0])
    o_b = _flip(_hgrn2_scan(_flip(q), _flip(k[:, :, 1]), _flip(v), _flip(log_f[:, :, 1])))
    o = _group_rmsnorm((o_f + o_b).reshape(n_b, n_s, HG_W), norm_g, HG_H)
    return (o * jax.nn.silu(g_out.astype(F32))).astype(q_raw.dtype)


def _mlstm_branch(q, k, v, ig_raw, fg_raw, o_raw, i_bias, f_bias, norm_g):
    n_b, n_s, _ = q.shape
    hshape = (n_b, n_s, ML_H, ML_HEAD_DIM)
    qf = q.astype(F32).reshape(hshape)
    kf = k.astype(F32).reshape(hshape) * (ML_HEAD_DIM ** -0.5)
    vf = v.astype(F32).reshape(hshape)
    ig = ig_raw.astype(F32).reshape(n_b, n_s, 2, ML_H) + i_bias.astype(F32)
    log_f = jax.nn.log_sigmoid(fg_raw.astype(F32).reshape(n_b, n_s, 2, ML_H) + f_bias.astype(F32))
    h_f = _mlstm_scan(qf, kf, vf, ig[:, :, 0], log_f[:, :, 0])
    h_b = _flip(_mlstm_scan(_flip(qf), _flip(kf), _flip(vf), _flip(ig[:, :, 1]), _flip(log_f[:, :, 1])))
    hn = _head_layernorm((h_f + h_b).reshape(n_b, n_s, ML_W), norm_g, ML_H)
    return (hn * jax.nn.sigmoid(o_raw.astype(F32))).astype(q.dtype)


def setup_inputs(seed: int = 0) -> dict:
    key = jax.random.key(seed)
    ks = iter(jax.random.split(key, 40))

    def nrm(shape, scale):
        return jax.random.normal(next(ks), shape, F32) * scale

    def gain(shape):
        return 1.0 + nrm(shape, 0.02)

    x = nrm((BATCH, SEQ, D_MODEL), 1.0)
    p = nrm((DEPTH, BATCH, SEQ, PLE_DIM), 1.0)
    norm_mix_g = gain((DEPTH, D_MODEL))
    w_in = nrm((DEPTH, D_MODEL, N_IN), D_MODEL ** -0.5)
    ssd_conv_w = nrm((DEPTH, SSD_CONV, SSD_XBC), SSD_CONV ** -0.5)
    ssd_conv_b = nrm((DEPTH, SSD_XBC), 0.02)
    dt0 = jnp.exp(jax.random.uniform(next(ks), (DEPTH, 2, SSD_H), F32,
                                     minval=math.log(1e-3), maxval=math.log(1e-1)))
    ssd_dt_bias = dt0 + jnp.log(-jnp.expm1(-dt0))
    ssd_a_log = jnp.log(jax.random.uniform(next(ks), (DEPTH, 2, SSD_H), F32, minval=1.0, maxval=16.0))
    ssd_d = gain((DEPTH, SSD_H))
    ssd_norm_g = gain((DEPTH, SSD_W))
    hg_lb_raw = nrm((DEPTH, 2, HG_W), 0.5)
    hg_norm_g = gain((DEPTH, HG_W))
    ml_i_bias = nrm((DEPTH, 2, ML_H), 0.1)
    ml_f_bias = jnp.linspace(3.0, 6.0, ML_H, dtype=F32) + nrm((DEPTH, 2, ML_H), 0.1)
    ml_norm_g = gain((DEPTH, ML_W))
    w_br_ssd = nrm((DEPTH, SSD_W, D_MODEL), SSD_W ** -0.5)
    w_br_hg = nrm((DEPTH, HG_W, D_MODEL), HG_W ** -0.5)
    w_br_ml = nrm((DEPTH, ML_W, D_MODEL), ML_W ** -0.5)
    w_out = nrm((DEPTH, D_MODEL, D_MODEL), D_MODEL ** -0.5)
    norm_ffn_g = gain((DEPTH, D_MODEL))
    w_up = nrm((DEPTH, D_MODEL, 2 * D_FF), D_MODEL ** -0.5)
    ffn_conv_w = nrm((DEPTH, FFN_CONV, 2 * D_FF), FFN_CONV ** -0.5)
    ffn_conv_b = nrm((DEPTH, 2 * D_FF), 0.02)
    w_down = nrm((DEPTH, D_FF, D_MODEL), D_FF ** -0.5)
    w_ple = nrm((DEPTH, PLE_DIM, D_MODEL), PLE_DIM ** -0.5)
    w_ple_gate = nrm((DEPTH, D_MODEL, D_MODEL), D_MODEL ** -0.5)
    final_norm_g = gain((D_MODEL,))
    return {'x': x, 'p': p, 'norm_mix_g': norm_mix_g, 'w_in': w_in,
            'ssd_conv_w': ssd_conv_w, 'ssd_conv_b': ssd_conv_b, 'ssd_dt_bias': ssd_dt_bias,
            'ssd_a_log': ssd_a_log, 'ssd_d': ssd_d, 'ssd_norm_g': ssd_norm_g,
            'hg_lb_raw': hg_lb_raw, 'hg_norm_g': hg_norm_g,
            'ml_i_bias': ml_i_bias, 'ml_f_bias': ml_f_bias, 'ml_norm_g': ml_norm_g,
            'w_br_ssd': w_br_ssd, 'w_br_hg': w_br_hg, 'w_br_ml': w_br_ml, 'w_out': w_out,
            'norm_ffn_g': norm_ffn_g, 'w_up': w_up, 'ffn_conv_w': ffn_conv_w, 'ffn_conv_b': ffn_conv_b,
            'w_down': w_down, 'w_ple': w_ple, 'w_ple_gate': w_ple_gate, 'final_norm_g': final_norm_g}


def reference(x, p, norm_mix_g, w_in, ssd_conv_w, ssd_conv_b, ssd_dt_bias, ssd_a_log, ssd_d,
              ssd_norm_g, hg_lb_raw, hg_norm_g, ml_i_bias, ml_f_bias, ml_norm_g,
              w_br_ssd, w_br_hg, w_br_ml, w_out, norm_ffn_g, w_up, ffn_conv_w, ffn_conv_b,
              w_down, w_ple, w_ple_gate, final_norm_g):
    n_b, n_s, _ = x.shape
    lb_soft = jax.nn.softmax(hg_lb_raw.astype(F32), axis=0)
    hg_lb = jnp.cumsum(lb_soft, axis=0) - lb_soft[0:1]
    h = x
    for l in range(DEPTH):
        u = _rmsnorm(h, norm_mix_g[l])
        (s_z, s_xbc, s_dt, g_q, g_f, g_i, g_g, m_q, m_k, m_v, m_i, m_f, m_o,
         gates) = _split(u @ w_in[l], IN_SIZES)
        y_ssd = _ssd_branch(s_z, s_xbc, s_dt, ssd_conv_w[l], ssd_conv_b[l], ssd_dt_bias[l],
                            ssd_a_log[l], ssd_d[l], ssd_norm_g[l])
        y_hg = _hgrn2_branch(g_q, g_f, g_i, g_g, hg_lb[l], hg_norm_g[l])
        y_ml = _mlstm_branch(m_q, m_k, m_v, m_i, m_f, m_o, ml_i_bias[l], ml_f_bias[l], ml_norm_g[l])
        gate = jax.nn.sigmoid(gates.reshape(n_b, n_s, N_BRANCH, D_MODEL))
        merged = (gate[:, :, 0] * (y_ssd @ w_br_ssd[l])
                  + gate[:, :, 1] * (y_hg @ w_br_hg[l])
                  + gate[:, :, 2] * (y_ml @ w_br_ml[l]))
        h = h + merged @ w_out[l]
        u = _rmsnorm(h, norm_ffn_g[l])
        up = _dwconv(u @ w_up[l], ffn_conv_w[l], ffn_conv_b[l])
        a_half, v_half = jnp.split(up, 2, axis=-1)
        h = h + (jax.nn.silu(a_half) * v_half) @ w_down[l]
        h = h + (p[l] @ w_ple[l]) * jax.nn.sigmoid(h @ w_ple_gate[l])
    return _rmsnorm(h, final_norm_g)
```

```python
import functools

import jax
import jax.numpy as jnp
from jax import lax
from jax.experimental import pallas as pl
from jax.experimental.pallas import tpu as pltpu

F32 = jnp.float32
BF16 = jnp.bfloat16

D_MODEL = 1024
PLE_DIM = 256
EPS = 1e-6
NEG = -1e30
TINY = 1e-30

SSD_HEAD_DIM = 64
SSD_H = 16
SSD_G = 2
SSD_HG = 8
SSD_N = 128
SSD_CONV = 5
SSD_GW = SSD_HG * SSD_HEAD_DIM
SSD_XBC = D_MODEL + 2 * SSD_G * SSD_N
HG_H = 8
ML_H = 8
HEAD = 128
D_FF = 2816
FFN_CONV = 3

LANES = 128
SUBLANES = 8
VMEM_LIMIT = 56 * 1024 * 1024

SSD_CHUNK = 128
HG_CHUNK = 64
ML_CHUNK = 128
HG_EXP_CLAMP = 80.0

_OFF_Z = 0
_OFF_XBC = 1024
_OFF_DT = _OFF_XBC + SSD_XBC
_OFF_GQ = _OFF_DT + 2 * SSD_H
_OFF_GF = _OFF_GQ + 1024
_OFF_GI = _OFF_GF + 2048
_OFF_GG = _OFF_GI + 1024
_OFF_MQ = _OFF_GG + 1024
_OFF_MK = _OFF_MQ + 1024
_OFF_MV = _OFF_MK + 1024
_OFF_MI = _OFF_MV + 1024
_OFF_MF = _OFF_MI + 2 * ML_H
_OFF_MO = _OFF_MF + 2 * ML_H
_OFF_GATES = _OFF_MO + 1024


def _dot(a, b):
    return jnp.dot(a, b, preferred_element_type=F32)


def _dot_nt(a, b):
    return lax.dot_general(a, b, (((1,), (1,)), ((), ())), preferred_element_type=F32)


def _dot_tn(a, b):
    return lax.dot_general(a, b, (((0,), (0,)), ((), ())), preferred_element_type=F32)


def _sigmoid(x):
    return 1.0 / (1.0 + jnp.exp(-x))


def _silu(x):
    return x * _sigmoid(x)


def _softplus(x):
    return jnp.maximum(x, 0.0) + jnp.log1p(jnp.exp(-jnp.abs(x)))


def _chunk_scan(x, chunk, reverse, op):
    n_rows = x.shape[0]
    row = lax.broadcasted_iota(jnp.int32, x.shape, 0) & (chunk - 1)
    sh = 1
    while sh < chunk:
        if reverse:
            shifted = pltpu.roll(x, n_rows - sh, axis=0)
            ok = row < chunk - sh
        else:
            shifted = pltpu.roll(x, sh, axis=0)
            ok = row >= sh
        if op == "add":
            x = x + jnp.where(ok, shifted, 0.0)
        else:
            x = jnp.maximum(x, jnp.where(ok, shifted, NEG))
        sh *= 2
    return x


def _tri_mask(n, reverse):
    t = lax.broadcasted_iota(jnp.int32, (n, n), 0)
    s = lax.broadcasted_iota(jnp.int32, (n, n), 1)
    return (s >= t) if reverse else (s <= t)


def _rmsnorm_kernel(h_ref, g_ref, o_ref):
    x = h_ref[...]
    ms = jnp.mean(x * x, axis=-1, keepdims=True)
    o_ref[...] = (x * lax.rsqrt(ms + EPS) * g_ref[...]).astype(o_ref.dtype)


def _rmsnorm_call(h2d, g, out_dtype, tm=1024):
    n_tok, d = h2d.shape
    tm = min(tm, n_tok)
    return pl.pallas_call(
        _rmsnorm_kernel,
        out_shape=jax.ShapeDtypeStruct((n_tok, d), out_dtype),
        grid=(n_tok // tm,),
        in_specs=[pl.BlockSpec((tm, d), lambda i: (i, 0)),
                  pl.BlockSpec((1, d), lambda i: (0, 0))],
        out_specs=pl.BlockSpec((tm, d), lambda i: (i, 0)),
        compiler_params=pltpu.CompilerParams(dimension_semantics=("parallel",),
                                             vmem_limit_bytes=VMEM_LIMIT),
        name="rmsnorm",
    )(h2d, g.reshape(1, d))


def _hgrn2_kernel(u_ref, w_ref, lb_ref, ng_ref, o_ref,
                  q_s, v_s, k_s, b_s, acc_s, *, seq, chunk):
    n_chunks = seq // chunk
    proj = _dot(u_ref[...], w_ref[...])
    q_s[...] = _silu(proj[:, 0:HEAD])
    v_s[...] = proj[:, 3 * HEAD:4 * HEAD].astype(BF16)
    o_ref[...] = proj[:, 4 * HEAD:5 * HEAD]

    for d in range(2):
        reverse = d == 1
        f_raw = proj[:, (1 + d) * HEAD:(2 + d) * HEAD]
        lb = lb_ref[d:d + 1, :]
        f = lb + (1.0 - lb) * _sigmoid(f_raw)
        k_s[...] = (1.0 - lb) * _sigmoid(-f_raw)
        b_s[...] = _chunk_scan(jnp.log(jnp.maximum(f, TINY)), chunk, reverse, "add")
        mask = _tri_mask(chunk, reverse)
        end_row = 0 if reverse else chunk - 1

        def body(ci, st_t, reverse=reverse, mask=mask, end_row=end_row, d=d):
            c = (n_chunks - 1 - ci) if reverse else ci
            r0 = pl.multiple_of(c * chunk, chunk)
            rows = pl.ds(r0, chunk)
            qc = q_s[rows, :]
            kc = k_s[rows, :]
            vc = v_s[rows, :]
            bc = b_s[rows, :]
            ref = b_s[pl.ds(r0 + chunk // 2, 1), :]
            bend = b_s[pl.ds(r0 + end_row, 1), :]
            qe = (qc * jnp.exp(jnp.clip(bc - ref, -HG_EXP_CLAMP, HG_EXP_CLAMP))).astype(BF16)
            ke = (kc * jnp.exp(jnp.clip(ref - bc, -HG_EXP_CLAMP, HG_EXP_CLAMP))).astype(BF16)
            att = jnp.where(mask, _dot_nt(qe, ke), 0.0)
            qi = (qc * jnp.exp(bc)).astype(BF16)
            o = _dot(att.astype(BF16), vc) + _dot_nt(qi, st_t.astype(BF16))
            kd = (kc * jnp.exp(bend - bc)).astype(BF16)
            st_t = st_t * jnp.exp(bend) + _dot_tn(vc, kd)
            if d == 0:
                acc_s[rows, :] = o
            else:
                acc_s[rows, :] = acc_s[rows, :] + o
            return st_t

        lax.fori_loop(0, n_chunks, body, jnp.zeros((HEAD, HEAD), F32))

    o = acc_s[...]
    o = o * lax.rsqrt(jnp.mean(o * o, axis=-1, keepdims=True) + EPS) * ng_ref[...]
    o_ref[...] = o * _silu(o_ref[...])


def _hgrn2_call(u, w, lb, ng, *, chunk=HG_CHUNK):
    n_b, seq, d = u.shape
    kern = functools.partial(_hgrn2_kernel, seq=seq, chunk=chunk)
    return pl.pallas_call(
        kern,
        out_shape=jax.ShapeDtypeStruct((n_b, seq, HG_H * HEAD), F32),
        grid=(n_b, HG_H),
        in_specs=[pl.BlockSpec((None, seq, d), lambda b, h: (b, 0, 0)),
                  pl.BlockSpec((None, d, 5 * HEAD), lambda b, h: (h, 0, 0)),
                  pl.BlockSpec((None, 2, HEAD), lambda b, h: (h, 0, 0)),
                  pl.BlockSpec((None, 1, HEAD), lambda b, h: (h, 0, 0))],
        out_specs=pl.BlockSpec((None, seq, HEAD), lambda b, h: (b, 0, h)),
        scratch_shapes=[pltpu.VMEM((seq, HEAD), F32),
                        pltpu.VMEM((seq, HEAD), BF16),
                        pltpu.VMEM((seq, HEAD), F32),
                        pltpu.VMEM((seq, HEAD), F32),
                        pltpu.VMEM((seq, HEAD), F32)],
        compiler_params=pltpu.CompilerParams(dimension_semantics=("parallel", "arbitrary"),
                                             vmem_limit_bytes=VMEM_LIMIT),
        name="hgrn2_branch",
    )(u, w, lb, ng)


def _mlstm_kernel(u_ref, w_ref, bias_ref, ng_ref, o_ref,
                  q_s, k_s, va_s, a_s, at_s, ct_s, acc_s, *, seq, chunk):
    n_chunks = seq // chunk
    proj = _dot(u_ref[...], w_ref[...])
    q_s[...] = proj[:, 0:HEAD].astype(BF16)
    k_s[...] = proj[:, HEAD:2 * HEAD] * (HEAD ** -0.5)
    va_s[:, 0:HEAD] = proj[:, 2 * HEAD:3 * HEAD].astype(BF16)
    va_s[:, HEAD:2 * HEAD] = jnp.ones((seq, HEAD), BF16)
    o_ref[...] = proj[:, 3 * HEAD:4 * HEAD]

    gates = proj[:, 4 * HEAD:5 * HEAD] + bias_ref[...]
    lane = lax.broadcasted_iota(jnp.int32, (seq, LANES), 1)
    log_f = jnp.minimum(gates, 0.0) - jnp.log1p(jnp.exp(-jnp.abs(gates)))
    pre = _chunk_scan(log_f, chunk, False, "add")
    suf = _chunk_scan(log_f, chunk, True, "add")
    a = jnp.where(lane < 2, gates, jnp.where(lane == 2, pre, suf))
    g = a - pltpu.roll(a, LANES - 2, axis=1)
    cm_f = pltpu.roll(_chunk_scan(g, chunk, False, "max"), 4, axis=1)
    cm_b = pltpu.roll(_chunk_scan(g, chunk, True, "max"), 4, axis=1)
    a = jnp.where(lane < 4, a, jnp.where(lane == 4, cm_f, cm_b))
    a_s[...] = a
    a_t = a.T
    for c in range(n_chunks):
        at_s[c] = a_t[0:SUBLANES, c * chunk:(c + 1) * chunk]

    for d in range(2):
        reverse = d == 1
        mask = _tri_mask(chunk, reverse)
        end_row = 0 if reverse else chunk - 1
        ct_s[...] = jnp.zeros_like(ct_s)

        def body(ci, m_prev, reverse=reverse, mask=mask, end_row=end_row, d=d):
            c = (n_chunks - 1 - ci) if reverse else ci
            r0 = pl.multiple_of(c * chunk, chunk)
            rows = pl.ds(r0, chunk)
            qc = q_s[rows, :]
            kc = k_s[rows, :]
            vac = va_s[rows, :]
            blk = a_s[rows, :]
            blk_t = at_s[c]
            it_c = blk[:, d:d + 1]
            bt_c = blk[:, 2 + d:3 + d]
            cm_c = blk[:, 4 + d:5 + d]
            it_r = blk_t[d:d + 1, :]
            bt_r = blk_t[2 + d:3 + d, :]
            m_t = bt_c + jnp.maximum(m_prev, cm_c)
            dmat = bt_c - bt_r + it_r
            p = jnp.exp(jnp.where(mask, dmat - m_t, NEG))
            sc = (_dot_nt(qc, kc.astype(BF16)) * p).astype(BF16)
            w_inter = jnp.exp(bt_c + m_prev - m_t)
            tot = _dot(sc, vac) + w_inter * _dot(qc, ct_s[...].astype(BF16))
            num = tot[:, 0:HEAD]
            den = tot[:, HEAD:2 * HEAD]
            hval = num / jnp.maximum(jnp.abs(den), jnp.exp(-m_t))
            m_new = m_t[end_row:end_row + 1, :]
            bt_end = bt_c[end_row:end_row + 1, :]
            decay = jnp.exp(bt_end + m_prev - m_new)
            wk = jnp.exp(bt_end - bt_c + it_c - m_new)
            ct_s[...] = ct_s[...] * decay + _dot_tn((kc * wk).astype(BF16), vac)
            if d == 0:
                acc_s[rows, :] = hval
            else:
                acc_s[rows, :] = acc_s[rows, :] + hval
            return m_new

        lax.fori_loop(0, n_chunks, body, jnp.full((1, 1), NEG, F32))

    hsum = acc_s[...]
    hc = hsum - jnp.mean(hsum, axis=-1, keepdims=True)
    hn = hc * lax.rsqrt(jnp.mean(hc * hc, axis=-1, keepdims=True) + EPS) * ng_ref[...]
    o_ref[...] = hn * _sigmoid(o_ref[...])


def _mlstm_call(u, w, bias, ng, *, chunk=ML_CHUNK):
    n_b, seq, d = u.shape
    kern = functools.partial(_mlstm_kernel, seq=seq, chunk=chunk)
    return pl.pallas_call(
        kern,
        out_shape=jax.ShapeDtypeStruct((n_b, seq, ML_H * HEAD), F32),
        grid=(n_b, ML_H),
        in_specs=[pl.BlockSpec((None, seq, d), lambda b, h: (b, 0, 0)),
                  pl.BlockSpec((None, d, 5 * HEAD), lambda b, h: (h, 0, 0)),
                  pl.BlockSpec((None, 1, HEAD), lambda b, h: (h, 0, 0)),
                  pl.BlockSpec((None, 1, HEAD), lambda b, h: (h, 0, 0))],
        out_specs=pl.BlockSpec((None, seq, HEAD), lambda b, h: (b, 0, h)),
        scratch_shapes=[pltpu.VMEM((seq, HEAD), BF16),
                        pltpu.VMEM((seq, HEAD), F32),
                        pltpu.VMEM((seq, 2 * HEAD), BF16),
                        pltpu.VMEM((seq, LANES), F32),
                        pltpu.VMEM((seq // chunk, SUBLANES, chunk), F32),
                        pltpu.VMEM((HEAD, 2 * HEAD), F32),
                        pltpu.VMEM((seq, HEAD), F32)],
        compiler_params=pltpu.CompilerParams(dimension_semantics=("parallel", "arbitrary"),
                                             vmem_limit_bytes=VMEM_LIMIT),
        name="mlstm_branch",
    )(u, w, bias, ng)


_SSD_W_Z = 0
_SSD_W_XBC = SSD_GW
_SSD_W_DT = SSD_GW + SSD_GW + 2 * SSD_N
_SSD_W_END = _SSD_W_DT + LANES
_SSD_XBC_G = SSD_GW + 2 * SSD_N
_SSD_PAD = SUBLANES
_SSD_PAIRS = SSD_HG // 2


def _pair_expand(blk, col0, lane_lo):
    return jnp.where(lane_lo, blk[:, col0:col0 + 1], blk[:, col0 + 1:col0 + 2])


def _ssd_kernel(u_ref, w_ref, cw_ref, cb_ref, dtb_ref, alog_ref, dsk_ref, ng_ref, o_ref,
                pad_s, xs_s, b_s, c_s, dt_s, a_s, at_s, cbm_s, st_s, *, seq, chunk, rblk):
    n_chunks = seq // chunk
    n_rblk = seq // rblk

    zero_rows = jnp.zeros((_SSD_PAD, _SSD_XBC_G), F32)
    pad_s[0:_SSD_PAD, :] = zero_rows
    pad_s[_SSD_PAD + seq:2 * _SSD_PAD + seq, :] = zero_rows
    for i in range(n_rblk):
        r0 = i * rblk
        pad_s[_SSD_PAD + r0:_SSD_PAD + r0 + rblk, :] = _dot(
            u_ref[r0:r0 + rblk, :], w_ref[:, _SSD_W_XBC:_SSD_W_DT])
    for i in range(n_rblk):
        r0 = i * rblk
        acc = cb_ref[...]
        for j in range(SSD_CONV):
            off = _SSD_PAD + r0 + j - SSD_CONV // 2
            acc = acc + pad_s[off:off + rblk, :] * cw_ref[j:j + 1, :]
        xbc = _silu(acc)
        xs_s[r0:r0 + rblk, :] = xbc[:, 0:SSD_GW]
        b_s[r0:r0 + rblk, :] = xbc[:, SSD_GW:SSD_GW + SSD_N].astype(BF16)
        c_s[r0:r0 + rblk, :] = xbc[:, SSD_GW + SSD_N:SSD_GW + 2 * SSD_N].astype(BF16)

    dt = _softplus(_dot(u_ref[...], w_ref[:, _SSD_W_DT:_SSD_W_END]) + dtb_ref[...])
    dt_s[...] = dt
    log_a = dt * (-jnp.exp(alog_ref[...]))
    lane = lax.broadcasted_iota(jnp.int32, (seq, LANES), 1)
    acum = jnp.where(lane < SSD_HG,
                     _chunk_scan(log_a, chunk, False, "add"),
                     _chunk_scan(log_a, chunk, True, "add"))
    a_s[...] = acum
    acum_t = acum.T
    for c in range(n_chunks):
        at_s[c] = acum_t[0:2 * SSD_HG, c * chunk:(c + 1) * chunk]
        r0 = c * chunk
        cbm_s[r0:r0 + chunk, :] = _dot_nt(c_s[r0:r0 + chunk, :], b_s[r0:r0 + chunk, :])

    lane_lo = lax.broadcasted_iota(jnp.int32, (chunk, LANES), 1) < SSD_HEAD_DIM
    lane_lo_row = lax.broadcasted_iota(jnp.int32, (1, LANES), 1) < SSD_HEAD_DIM

    for d in range(2):
        reverse = d == 1
        mask = _tri_mask(chunk, reverse)
        end_row = 0 if reverse else chunk - 1
        st_s[...] = jnp.zeros_like(st_s)

        def body(ci, carry, reverse=reverse, mask=mask, end_row=end_row, d=d):
            c = (n_chunks - 1 - ci) if reverse else ci
            r0 = pl.multiple_of(c * chunk, chunk)
            rows = pl.ds(r0, chunk)
            blk = a_s[rows, :]
            blk_t = at_s[c]
            dt_blk = dt_s[rows, :]
            cbm = cbm_s[rows, :]
            cc = c_s[rows, :]
            bc = b_s[rows, :]
            y_inter = _dot(cc, st_s[...].astype(BF16))
            end_blk = blk[end_row:end_row + 1, :]
            xw_tiles = []
            for p in range(_SSD_PAIRS):
                col0 = d * SSD_HG + 2 * p
                cols = slice(p * LANES, (p + 1) * LANES)
                a_pair = _pair_expand(blk, col0, lane_lo)
                xdt = xs_s[rows, cols] * _pair_expand(dt_blk, col0, lane_lo)
                y = y_inter[:, cols] * jnp.exp(a_pair)
                for hh in range(2):
                    col = col0 + hh
                    seg = jnp.exp(jnp.where(mask, blk[:, col:col + 1] - blk_t[col:col + 1, :], NEG))
                    keep = lane_lo if hh == 0 else jnp.logical_not(lane_lo)
                    x_h = jnp.where(keep, xdt, 0.0).astype(BF16)
                    y = y + _dot((cbm * seg).astype(BF16), x_h)
                if d == 0:
                    o_ref[rows, cols] = y
                else:
                    o_ref[rows, cols] = o_ref[rows, cols] + y
                end_pair = jnp.where(lane_lo_row, end_blk[:, col0:col0 + 1], end_blk[:, col0 + 1:col0 + 2])
                xw_tiles.append((xdt * jnp.exp(end_pair - a_pair)).astype(BF16))
                st_s[:, cols] = st_s[:, cols] * jnp.exp(end_pair)
            st_s[...] = st_s[...] + _dot_tn(bc, jnp.concatenate(xw_tiles, axis=1))
            return carry

        lax.fori_loop(0, n_chunks, body, 0)

    for i in range(n_rblk):
        r0 = i * rblk
        rows = slice(r0, r0 + rblk)
        z = _dot(u_ref[rows, :], w_ref[:, _SSD_W_Z:_SSD_W_XBC])
        y = (o_ref[rows, :] + xs_s[rows, :] * dsk_ref[...]) * _silu(z)
        o_ref[rows, :] = y * lax.rsqrt(jnp.mean(y * y, axis=-1, keepdims=True) + EPS) * ng_ref[...]


def _ssd_call(u, w, cw, cb, dtb, alog, dsk, ng, *, chunk=SSD_CHUNK):
    n_b, seq, d = u.shape
    rblk = min(seq, 512)
    kern = functools.partial(_ssd_kernel, seq=seq, chunk=chunk, rblk=rblk)
    vec = lambda n: pl.BlockSpec((None, 1, n), lambda b, g: (g, 0, 0))
    return pl.pallas_call(
        kern,
        out_shape=jax.ShapeDtypeStruct((n_b, seq, SSD_G * SSD_GW), F32),
        grid=(n_b, SSD_G),
        in_specs=[pl.BlockSpec((None, seq, d), lambda b, g: (b, 0, 0)),
                  pl.BlockSpec((None, d, _SSD_W_END), lambda b, g: (g, 0, 0)),
                  pl.BlockSpec((None, SSD_CONV, _SSD_XBC_G), lambda b, g: (g, 0, 0)),
                  vec(_SSD_XBC_G), vec(LANES), vec(LANES), vec(SSD_GW), vec(SSD_GW)],
        out_specs=pl.BlockSpec((None, seq, SSD_GW), lambda b, g: (b, 0, g)),
        scratch_shapes=[pltpu.VMEM((seq + 2 * _SSD_PAD, _SSD_XBC_G), F32),
                        pltpu.VMEM((seq, SSD_GW), F32),
                        pltpu.VMEM((seq, SSD_N), BF16),
                        pltpu.VMEM((seq, SSD_N), BF16),
                        pltpu.VMEM((seq, LANES), F32),
                        pltpu.VMEM((seq, LANES), F32),
                        pltpu.VMEM((seq // chunk, 2 * SSD_HG, chunk), F32),
                        pltpu.VMEM((seq, chunk), F32),
                        pltpu.VMEM((SSD_N, SSD_GW), F32)],
        compiler_params=pltpu.CompilerParams(dimension_semantics=("parallel", "arbitrary"),
                                             vmem_limit_bytes=VMEM_LIMIT),
        name="ssd_branch",
    )(u, w, cw, cb, dtb, alog, dsk, ng)


def _merge_kernel(u_ref, h_ref, y0_ref, y1_ref, y2_ref, wg_ref, wb_ref, wo_ref, o_ref, acc_s):
    br = pl.program_id(1)
    gate = _sigmoid(_dot(u_ref[...], wg_ref[...]))

    def branch(y_ref):
        return gate * _dot(y_ref[...].astype(BF16), wb_ref[...])

    @pl.when(br == 0)
    def _():
        acc_s[...] = branch(y0_ref)

    @pl.when(br == 1)
    def _():
        acc_s[...] = acc_s[...] + branch(y1_ref)

    @pl.when(br == 2)
    def _():
        merged = acc_s[...] + branch(y2_ref)
        o_ref[...] = h_ref[...] + _dot(merged.astype(BF16), wo_ref[...])


def _merge_call(u2d, h2d, y0, y1, y2, wg, wb, wo, tm=512):
    n_tok, d = h2d.shape
    tm = min(tm, n_tok)
    tok = lambda: pl.BlockSpec((tm, d), lambda i, r: (i, 0))
    return pl.pallas_call(
        _merge_kernel,
        out_shape=jax.ShapeDtypeStruct((n_tok, d), F32),
        grid=(n_tok // tm, 3),
        in_specs=[tok(), tok(), tok(), tok(), tok(),
                  pl.BlockSpec((None, d, d), lambda i, r: (r, 0, 0)),
                  pl.BlockSpec((None, d, d), lambda i, r: (r, 0, 0)),
                  pl.BlockSpec((d, d), lambda i, r: (0, 0))],
        out_specs=tok(),
        scratch_shapes=[pltpu.VMEM((tm, d), F32)],
        compiler_params=pltpu.CompilerParams(dimension_semantics=("parallel", "arbitrary"),
                                             vmem_limit_bytes=VMEM_LIMIT),
        name="merge",
    )(u2d, h2d, y0, y1, y2, wg, wb, wo)


_FFN_CW = 256
_FFN_NJ = D_FF // _FFN_CW
_FFN_HALO = SUBLANES


def _ffn_kernel(h_ref, hp_ref, hn_ref, p_ref, g_ref, wa_ref, wv_ref, cwa_ref, cwv_ref,
                cba_ref, cbv_ref, wd_ref, wple_ref, wpg_ref, fg_ref, o_ref, u_s,
                *, tm, n_tiles, final_norm):
    i = pl.program_id(1)
    j = pl.program_id(2)

    def norm(x):
        return (x * lax.rsqrt(jnp.mean(x * x, axis=-1, keepdims=True) + EPS) * g_ref[...]).astype(BF16)

    @pl.when(j == 0)
    def _():
        up = jnp.where(i > 0, 1.0, 0.0)
        dn = jnp.where(i < n_tiles - 1, 1.0, 0.0)
        u_s[0:_FFN_HALO, :] = norm(hp_ref[...] * up)
        u_s[_FFN_HALO:_FFN_HALO + tm, :] = norm(h_ref[...])
        u_s[_FFN_HALO + tm:2 * _FFN_HALO + tm, :] = norm(hn_ref[...] * dn)

    def conv(w_ref, cw_ref, cb_ref):
        up = _dot(u_s[...], w_ref[...])
        acc = cb_ref[...]
        for t in range(FFN_CONV):
            off = _FFN_HALO + t - FFN_CONV // 2
            acc = acc + up[off:off + tm, :] * cw_ref[t:t + 1, :]
        return acc

    act = (_silu(conv(wa_ref, cwa_ref, cba_ref)) * conv(wv_ref, cwv_ref, cbv_ref)).astype(BF16)
    part = _dot(act, wd_ref[...])

    @pl.when(j == 0)
    def _():
        o_ref[...] = h_ref[...] + part

    @pl.when(j > 0)
    def _():
        o_ref[...] = o_ref[...] + part

    @pl.when(j == _FFN_NJ - 1)
    def _():
        h2 = o_ref[...]
        ple = _dot(p_ref[...].astype(BF16), wple_ref[...])
        h3 = h2 + ple * _sigmoid(_dot(h2.astype(BF16), wpg_ref[...]))
        if final_norm:
            h3 = h3 * lax.rsqrt(jnp.mean(h3 * h3, axis=-1, keepdims=True) + EPS) * fg_ref[...]
        o_ref[...] = h3


def _ffn_call(h, p, g, w_up, cw, cb, w_down, w_ple, w_pg, fg, *, final_norm, tm=1024):
    n_b, seq, d = h.shape
    tm = min(tm, seq)
    n_tiles = seq // tm
    hb = tm // _FFN_HALO
    n_hb = seq // _FFN_HALO
    kern = functools.partial(_ffn_kernel, tm=tm, n_tiles=n_tiles, final_norm=final_norm)
    cst = lambda shape: pl.BlockSpec(shape, lambda b, i, j: tuple(0 for _ in shape))
    return pl.pallas_call(
        kern,
        out_shape=jax.ShapeDtypeStruct((n_b, seq, d), F32),
        grid=(n_b, n_tiles, _FFN_NJ),
        in_specs=[pl.BlockSpec((None, tm, d), lambda b, i, j: (b, i, 0)),
                  pl.BlockSpec((None, _FFN_HALO, d), lambda b, i, j: (b, jnp.maximum(i * hb - 1, 0), 0)),
                  pl.BlockSpec((None, _FFN_HALO, d), lambda b, i, j: (b, jnp.minimum((i + 1) * hb, n_hb - 1), 0)),
                  pl.BlockSpec((None, tm, PLE_DIM), lambda b, i, j: (b, i, 0)),
                  cst((1, d)),
                  pl.BlockSpec((d, _FFN_CW), lambda b, i, j: (0, j)),
                  pl.BlockSpec((d, _FFN_CW), lambda b, i, j: (0, _FFN_NJ + j)),
                  pl.BlockSpec((FFN_CONV, _FFN_CW), lambda b, i, j: (0, j)),
                  pl.BlockSpec((FFN_CONV, _FFN_CW), lambda b, i, j: (0, _FFN_NJ + j)),
                  pl.BlockSpec((1, _FFN_CW), lambda b, i, j: (0, j)),
                  pl.BlockSpec((1, _FFN_CW), lambda b, i, j: (0, _FFN_NJ + j)),
                  pl.BlockSpec((_FFN_CW, d), lambda b, i, j: (j, 0)),
                  cst((PLE_DIM, d)), cst((d, d)), cst((1, d))],
        out_specs=pl.BlockSpec((None, tm, d), lambda b, i, j: (b, i, 0)),
        scratch_shapes=[pltpu.VMEM((tm + 2 * _FFN_HALO, d), BF16)],
        compiler_params=pltpu.CompilerParams(
            dimension_semantics=("parallel", "parallel", "arbitrary"),
            vmem_limit_bytes=VMEM_LIMIT),
        name="convffn_ple",
    )(h, h, h, p, g.reshape(1, d), w_up, w_up, cw, cw, cb.reshape(1, -1), cb.reshape(1, -1),
      w_down, w_ple, w_pg, fg.reshape(1, d))


def _cols(w, start, width):
    return lax.slice_in_dim(w, start, start + width, axis=1)


def _pad_cols(w, width):
    return jnp.pad(w, ((0, 0), (0, width - w.shape[1])))


def _ssd_params(w_in, conv_w, conv_b, dt_bias, a_log, d_skip, norm_g):
    ws, cws, cbs, dtbs, alogs = [], [], [], [], []
    for g in range(SSD_G):
        dt_cols = jnp.concatenate(
            [_cols(w_in, _OFF_DT + dd * SSD_H + g * SSD_HG, SSD_HG) for dd in range(2)], axis=1)
        ws.append(jnp.concatenate([
            _cols(w_in, _OFF_Z + g * SSD_GW, SSD_GW),
            _cols(w_in, _OFF_XBC + g * SSD_GW, SSD_GW),
            _cols(w_in, _OFF_XBC + D_MODEL + g * SSD_N, SSD_N),
            _cols(w_in, _OFF_XBC + D_MODEL + SSD_G * SSD_N + g * SSD_N, SSD_N),
            _pad_cols(dt_cols, LANES)], axis=1))
        pick = lambda a: jnp.concatenate([
            _cols(a, g * SSD_GW, SSD_GW),
            _cols(a, D_MODEL + g * SSD_N, SSD_N),
            _cols(a, D_MODEL + SSD_G * SSD_N + g * SSD_N, SSD_N)], axis=1)
        cws.append(pick(conv_w))
        cbs.append(pick(conv_b.reshape(1, -1)))
        head_row = lambda a: _pad_cols(
            jnp.concatenate([a[dd, g * SSD_HG:(g + 1) * SSD_HG] for dd in range(2)]).reshape(1, -1), LANES)
        dtbs.append(head_row(dt_bias))
        alogs.append(head_row(a_log))
    dsk = jnp.repeat(d_skip, SSD_HEAD_DIM).reshape(SSD_G, 1, SSD_GW)
    return (jnp.stack(ws).astype(BF16), jnp.stack(cws), jnp.stack(cbs), jnp.stack(dtbs),
            jnp.stack(alogs), dsk, norm_g.reshape(SSD_G, 1, SSD_GW))


def _hgrn2_params(w_in, lb, norm_g):
    ws = [jnp.concatenate([
        _cols(w_in, _OFF_GQ + h * HEAD, HEAD),
        _cols(w_in, _OFF_GF + h * HEAD, HEAD),
        _cols(w_in, _OFF_GF + D_MODEL + h * HEAD, HEAD),
        _cols(w_in, _OFF_GI + h * HEAD, HEAD),
        _cols(w_in, _OFF_GG + h * HEAD, HEAD)], axis=1) for h in range(HG_H)]
    lbs = lb.reshape(2, HG_H, HEAD).swapaxes(0, 1)
    return jnp.stack(ws).astype(BF16), lbs, norm_g.reshape(HG_H, 1, HEAD)


def _mlstm_params(w_in, i_bias, f_bias, norm_g):
    ws, biases = [], []
    for h in range(ML_H):
        gate_cols = jnp.concatenate([
            _cols(w_in, _OFF_MI + h, 1), _cols(w_in, _OFF_MI + ML_H + h, 1),
            _cols(w_in, _OFF_MF + h, 1), _cols(w_in, _OFF_MF + ML_H + h, 1)], axis=1)
        ws.append(jnp.concatenate([
            _cols(w_in, _OFF_MQ + h * HEAD, HEAD),
            _cols(w_in, _OFF_MK + h * HEAD, HEAD),
            _cols(w_in, _OFF_MV + h * HEAD, HEAD),
            _cols(w_in, _OFF_MO + h * HEAD, HEAD),
            _pad_cols(gate_cols, LANES)], axis=1))
        biases.append(_pad_cols(
            jnp.stack([i_bias[0, h], i_bias[1, h], f_bias[0, h], f_bias[1, h]]).reshape(1, 4), LANES))
    return jnp.stack(ws).astype(BF16), jnp.stack(biases), norm_g.reshape(ML_H, 1, HEAD)


def kernel(x, p, norm_mix_g, w_in, ssd_conv_w, ssd_conv_b, ssd_dt_bias, ssd_a_log, ssd_d, ssd_norm_g, hg_lb_raw, hg_norm_g, ml_i_bias, ml_f_bias, ml_norm_g, w_br_ssd, w_br_hg, w_br_ml, w_out, norm_ffn_g, w_up, ffn_conv_w, ffn_conv_b, w_down, w_ple, w_ple_gate, final_norm_g):
    n_b, seq, d = x.shape
    depth = w_in.shape[0]
    lb_soft = jax.nn.softmax(hg_lb_raw.astype(F32), axis=0)
    hg_lb = jnp.cumsum(lb_soft, axis=0) - lb_soft[0:1]
    h = x
    for l in range(depth):
        u2d = _rmsnorm_call(h.reshape(n_b * seq, d), norm_mix_g[l], BF16)
        u = u2d.reshape(n_b, seq, d)
        y_ssd = _ssd_call(u, *_ssd_params(w_in[l], ssd_conv_w[l], ssd_conv_b[l], ssd_dt_bias[l],
                                          ssd_a_log[l], ssd_d[l], ssd_norm_g[l]))
        y_hg = _hgrn2_call(u, *_hgrn2_params(w_in[l], hg_lb[l], hg_norm_g[l]))
        y_ml = _mlstm_call(u, *_mlstm_params(w_in[l], ml_i_bias[l], ml_f_bias[l], ml_norm_g[l]))
        wg = jnp.stack([_cols(w_in[l], _OFF_GATES + r * d, d) for r in range(3)]).astype(BF16)
        wb = jnp.stack([w_br_ssd[l], w_br_hg[l], w_br_ml[l]]).astype(BF16)
        tok = lambda a: a.reshape(n_b * seq, d)
        h = _merge_call(u2d, tok(h), tok(y_ssd), tok(y_hg), tok(y_ml), wg, wb,
                        w_out[l].astype(BF16)).reshape(n_b, seq, d)
        h = _ffn_call(h, p[l], norm_ffn_g[l], w_up[l].astype(BF16), ffn_conv_w[l], ffn_conv_b[l],
                      w_down[l].astype(BF16), w_ple[l].astype(BF16), w_ple_gate[l].astype(BF16),
                      final_norm_g, final_norm=(l == depth - 1))
    return h
```

```python
import functools

import jax
import jax.numpy as jnp
from jax import lax
from jax.experimental import pallas as pl
from jax.experimental.pallas import tpu as pltpu

F32 = jnp.float32
BF16 = jnp.bfloat16

D_MODEL = 1024
PLE_DIM = 256
EPS = 1e-6
NEG = -1e30
TINY = 1e-30

SSD_HEAD_DIM = 64
SSD_H = 16
SSD_G = 2
SSD_HG = 8
SSD_N = 128
SSD_CONV = 5
SSD_GW = SSD_HG * SSD_HEAD_DIM
SSD_XBC = D_MODEL + 2 * SSD_G * SSD_N
HG_H = 8
ML_H = 8
HEAD = 128
D_FF = 2816
FFN_CONV = 3

LANES = 128
SUBLANES = 8
VMEM_LIMIT = 56 * 1024 * 1024

SSD_CHUNK = 128
HG_CHUNK = 64
ML_CHUNK = 128
ML_UNROLL = 8
HG_UNROLL = 16
HG_SAFE_DECAY = 60.0

_OFF_Z = 0
_OFF_XBC = 1024
_OFF_DT = _OFF_XBC + SSD_XBC
_OFF_GQ = _OFF_DT + 2 * SSD_H
_OFF_GF = _OFF_GQ + 1024
_OFF_GI = _OFF_GF + 2048
_OFF_GG = _OFF_GI + 1024
_OFF_MQ = _OFF_GG + 1024
_OFF_MK = _OFF_MQ + 1024
_OFF_MV = _OFF_MK + 1024
_OFF_MI = _OFF_MV + 1024
_OFF_MF = _OFF_MI + 2 * ML_H
_OFF_MO = _OFF_MF + 2 * ML_H
_OFF_GATES = _OFF_MO + 1024


def _dot(a, b):
    return jnp.dot(a, b, preferred_element_type=F32)


def _dot_nt(a, b):
    return lax.dot_general(a, b, (((1,), (1,)), ((), ())), preferred_element_type=F32)


def _dot_tn(a, b):
    return lax.dot_general(a, b, (((0,), (0,)), ((), ())), preferred_element_type=F32)


def _sigmoid(x):
    return 1.0 / (1.0 + jnp.exp(-x))


def _silu(x):
    return x * _sigmoid(x)


def _softplus(x):
    return jnp.maximum(x, 0.0) + jnp.log1p(jnp.exp(-jnp.abs(x)))


def _chunk_scan(x, chunk, reverse, op):
    n_rows = x.shape[0]
    row = lax.broadcasted_iota(jnp.int32, x.shape, 0) & (chunk - 1)
    sh = 1
    while sh < chunk:
        if reverse:
            shifted = pltpu.roll(x, n_rows - sh, axis=0)
            ok = row < chunk - sh
        else:
            shifted = pltpu.roll(x, sh, axis=0)
            ok = row >= sh
        if op == "add":
            x = x + jnp.where(ok, shifted, 0.0)
        else:
            x = jnp.maximum(x, jnp.where(ok, shifted, NEG))
        sh *= 2
    return x


def _tri_mask(n, reverse):
    t = lax.broadcasted_iota(jnp.int32, (n, n), 0)
    s = lax.broadcasted_iota(jnp.int32, (n, n), 1)
    return (s >= t) if reverse else (s <= t)


def _rmsnorm_kernel(h_ref, g_ref, o_ref):
    x = h_ref[...]
    ms = jnp.mean(x * x, axis=-1, keepdims=True)
    o_ref[...] = (x * lax.rsqrt(ms + EPS) * g_ref[...]).astype(o_ref.dtype)


def _rmsnorm_call(h2d, g, out_dtype, tm=1024):
    n_tok, d = h2d.shape
    tm = min(tm, n_tok)
    return pl.pallas_call(
        _rmsnorm_kernel,
        out_shape=jax.ShapeDtypeStruct((n_tok, d), out_dtype),
        grid=(n_tok // tm,),
        in_specs=[pl.BlockSpec((tm, d), lambda i: (i, 0)),
                  pl.BlockSpec((1, d), lambda i: (0, 0))],
        out_specs=pl.BlockSpec((tm, d), lambda i: (i, 0)),
        compiler_params=pltpu.CompilerParams(dimension_semantics=("parallel",),
                                             vmem_limit_bytes=VMEM_LIMIT),
        name="rmsnorm",
    )(h2d, g.reshape(1, d))


def _hg_att_safe(qc, kc, lfc, reverse):
    n = qc.shape[0]
    t = lax.broadcasted_iota(jnp.int32, (n, n), 0)
    s = lax.broadcasted_iota(jnp.int32, (n, n), 1)
    att = jnp.where(t == s, jnp.sum(qc * kc, axis=-1, keepdims=True), 0.0)
    half = 1
    while half < n:
        pre = _chunk_scan(lfc, half, False, "add")
        suf = _chunk_scan(lfc, half, True, "add")
        t_hi = (t & half) != 0
        s_hi = (s & half) != 0
        same_block = (t ^ s) < 2 * half
        if reverse:
            qe = qc * jnp.exp(suf)
            ke = kc * jnp.exp(pre - lfc)
            pair = jnp.logical_and(same_block, jnp.logical_and(jnp.logical_not(t_hi), s_hi))
        else:
            qe = qc * jnp.exp(pre)
            ke = kc * jnp.exp(suf - lfc)
            pair = jnp.logical_and(same_block, jnp.logical_and(t_hi, jnp.logical_not(s_hi)))
        att = att + jnp.where(pair, _dot_nt(qe.astype(BF16), ke.astype(BF16)), 0.0)
        half *= 2
    return att


def _hgrn2_kernel(u_ref, w_ref, lb_ref, ng_ref, o_ref,
                  q_s, v_s, k_s, lf_s, b_s, qe_s, ke_s, qi_s, kd_s, dec_s, upd_s, st_s, acc_s,
                  *, seq, chunk):
    n_chunks = seq // chunk
    proj = _dot(u_ref[...], w_ref[...])
    q_s[...] = _silu(proj[:, 0:HEAD])
    v_s[...] = proj[:, 3 * HEAD:4 * HEAD].astype(BF16)
    o_ref[...] = proj[:, 4 * HEAD:5 * HEAD]

    b_min = None
    for d in range(2):
        f_raw = proj[:, (1 + d) * HEAD:(2 + d) * HEAD]
        lb = lb_ref[d:d + 1, :]
        f = lb + (1.0 - lb) * _sigmoid(f_raw)
        k_s[d] = (1.0 - lb) * _sigmoid(-f_raw)
        log_f = jnp.log(jnp.maximum(f, TINY))
        lf_s[d] = log_f
        b = _chunk_scan(log_f, chunk, d == 1, "add")
        b_s[d] = b
        b_col_min = jnp.min(b, axis=0, keepdims=True)
        b_min = b_col_min if b_min is None else jnp.minimum(b_min, b_col_min)
    mild_decay = jnp.min(b_min) >= -HG_SAFE_DECAY

    for d in range(2):
        reverse = d == 1
        mask = _tri_mask(chunk, reverse)
        end_row = 0 if reverse else chunk - 1

        def chunk_rows(c):
            r0 = pl.multiple_of(c * chunk, chunk)
            return r0, pl.ds(r0, chunk)

        def scale_body(c, carry, d=d, end_row=end_row):
            r0, rows = chunk_rows(c)
            qc = q_s[rows, :]
            kc = k_s[d, rows, :]
            bc = b_s[d, rows, :]
            ref = b_s[d, pl.ds(r0 + chunk // 2, 1), :]
            bend = b_s[d, pl.ds(r0 + end_row, 1), :]
            qe_s[rows, :] = (qc * jnp.exp(bc - ref)).astype(BF16)
            ke_s[rows, :] = (kc * jnp.exp(ref - bc)).astype(BF16)
            qi_s[rows, :] = (qc * jnp.exp(bc)).astype(BF16)
            kd_s[rows, :] = (kc * jnp.exp(bend - bc)).astype(BF16)
            dec_s[c] = jnp.broadcast_to(jnp.exp(bend), (HEAD, HEAD)).T
            return carry

        lax.fori_loop(0, n_chunks, scale_body, 0, unroll=HG_UNROLL)

        def upd_body(c, carry):
            _, rows = chunk_rows(c)
            upd_s[c] = _dot_tn(kd_s[rows, :], v_s[rows, :])
            return carry

        lax.fori_loop(0, n_chunks, upd_body, 0, unroll=HG_UNROLL)

        def rec_body(ci, st, reverse=reverse):
            c = (n_chunks - 1 - ci) if reverse else ci
            st_s[c] = st.astype(BF16)
            return st * dec_s[c] + upd_s[c]

        lax.fori_loop(0, n_chunks, rec_body, jnp.zeros((HEAD, HEAD), F32))

        def out_body(c, carry, fast, d=d, mask=mask, reverse=reverse):
            _, rows = chunk_rows(c)
            if fast:
                att = jnp.where(mask, _dot_nt(qe_s[rows, :], ke_s[rows, :]), 0.0)
            else:
                att = _hg_att_safe(q_s[rows, :], k_s[d, rows, :], lf_s[d, rows, :], reverse)
            o = _dot(jnp.concatenate([qi_s[rows, :], att.astype(BF16)], axis=1),
                     jnp.concatenate([st_s[c], v_s[rows, :]], axis=0))
            if d == 0:
                acc_s[rows, :] = o
            else:
                acc_s[rows, :] = acc_s[rows, :] + o
            return carry

        @pl.when(mild_decay)
        def _():
            lax.fori_loop(0, n_chunks, functools.partial(out_body, fast=True), 0, unroll=HG_UNROLL)

        @pl.when(jnp.logical_not(mild_decay))
        def _():
            lax.fori_loop(0, n_chunks, functools.partial(out_body, fast=False), 0)

    o = acc_s[...]
    o = o * lax.rsqrt(jnp.mean(o * o, axis=-1, keepdims=True) + EPS) * ng_ref[...]
    o_ref[...] = o * _silu(o_ref[...])


def _hgrn2_call(u, w, lb, ng, *, chunk=HG_CHUNK):
    n_b, seq, d = u.shape
    kern = functools.partial(_hgrn2_kernel, seq=seq, chunk=chunk)
    return pl.pallas_call(
        kern,
        out_shape=jax.ShapeDtypeStruct((n_b, seq, HG_H * HEAD), F32),
        grid=(n_b, HG_H),
        in_specs=[pl.BlockSpec((None, seq, d), lambda b, h: (b, 0, 0)),
                  pl.BlockSpec((None, d, 5 * HEAD), lambda b, h: (h, 0, 0)),
                  pl.BlockSpec((None, 2, HEAD), lambda b, h: (h, 0, 0)),
                  pl.BlockSpec((None, 1, HEAD), lambda b, h: (h, 0, 0))],
        out_specs=pl.BlockSpec((None, seq, HEAD), lambda b, h: (b, 0, h)),
        scratch_shapes=[pltpu.VMEM((seq, HEAD), F32),
                        pltpu.VMEM((seq, HEAD), BF16),
                        pltpu.VMEM((2, seq, HEAD), F32),
                        pltpu.VMEM((2, seq, HEAD), F32),
                        pltpu.VMEM((2, seq, HEAD), F32),
                        pltpu.VMEM((seq, HEAD), BF16),
                        pltpu.VMEM((seq, HEAD), BF16),
                        pltpu.VMEM((seq, HEAD), BF16),
                        pltpu.VMEM((seq, HEAD), BF16),
                        pltpu.VMEM((seq // chunk, HEAD, HEAD), F32),
                        pltpu.VMEM((seq // chunk, HEAD, HEAD), F32),
                        pltpu.VMEM((seq // chunk, HEAD, HEAD), BF16),
                        pltpu.VMEM((seq, HEAD), F32)],
        compiler_params=pltpu.CompilerParams(dimension_semantics=("parallel", "arbitrary"),
                                             vmem_limit_bytes=VMEM_LIMIT),
        name="hgrn2_branch",
    )(u, w, lb, ng)


def _log_sigmoid(x):
    return jnp.minimum(x, 0.0) - jnp.log1p(jnp.exp(-jnp.abs(x)))


def _mlstm_kernel(u_ref, w_ref, bias_ref, ng_ref, o_ref,
                  qb_s, kb_s, va_s, bt_s, cm_s, qw_s, kw_s, mx_s, en_s, gt_s, dec_s, upd_s, ct_s,
                  acc_s, *, seq, chunk):
    n_chunks = seq // chunk
    proj = _dot(u_ref[...], w_ref[...])
    q = proj[:, 0:HEAD]
    k = proj[:, HEAD:2 * HEAD] * (HEAD ** -0.5)
    qb_s[...] = q.astype(BF16)
    kb_s[...] = k.astype(BF16)
    va_s[:, 0:HEAD] = proj[:, 2 * HEAD:3 * HEAD].astype(BF16)
    va_s[:, HEAD:2 * HEAD] = jnp.ones((seq, HEAD), BF16)
    o_ref[...] = proj[:, 3 * HEAD:4 * HEAD]

    fwd_lane = lax.broadcasted_iota(jnp.int32, (seq, LANES), 1) == 0
    it = proj[:, 4 * HEAD:5 * HEAD] + bias_ref[0:1, :]
    log_f = _log_sigmoid(proj[:, 5 * HEAD:6 * HEAD] + bias_ref[1:2, :])
    bt = jnp.where(fwd_lane, _chunk_scan(log_f, chunk, False, "add"), _chunk_scan(log_f, chunk, True, "add"))
    g = it - bt
    cm = jnp.where(fwd_lane, _chunk_scan(g, chunk, False, "max"), _chunk_scan(g, chunk, True, "max"))
    bt_s[...] = bt
    cm_s[...] = cm

    fwd_lane_c = lax.broadcasted_iota(jnp.int32, (n_chunks, LANES), 1) == 0
    chunk_id = lax.broadcasted_iota(jnp.int32, (n_chunks, LANES), 0)

    def chunk_ends(ref):
        return jnp.where(fwd_lane_c, ref[pl.ds(chunk - 1, n_chunks, stride=chunk), :],
                         ref[pl.ds(0, n_chunks, stride=chunk), :])

    def scan_chunks(x, op):
        return jnp.where(fwd_lane_c, _chunk_scan(x, n_chunks, False, op), _chunk_scan(x, n_chunks, True, op))

    bt_end = chunk_ends(bt_s)
    p_sum = scan_chunks(bt_end, "add")
    m_end = p_sum + scan_chunks(chunk_ends(cm_s) - (p_sum - bt_end), "max")
    m_prev = jnp.where(fwd_lane_c,
                       jnp.where(chunk_id == 0, NEG, pltpu.roll(m_end, 1, axis=0)),
                       jnp.where(chunk_id == n_chunks - 1, NEG, pltpu.roll(m_end, n_chunks - 1, axis=0)))
    decay = jnp.exp(bt_end + m_prev - m_end)

    def per_token(x):
        return jnp.concatenate([jnp.broadcast_to(x[c:c + 1, :], (chunk, LANES)) for c in range(n_chunks)],
                               axis=0)

    m_prev_t = per_token(m_prev)
    mx = jnp.maximum(m_prev_t, cm)
    w_inter = jnp.exp(m_prev_t - mx)
    en = jnp.exp(-(bt + mx))
    wk = jnp.exp(per_token(bt_end) - bt + it - per_token(m_end))

    g_t = g.T
    for c in range(n_chunks):
        gt_s[c] = g_t[0:SUBLANES, c * chunk:(c + 1) * chunk]
    for d in range(2):
        def lanes(x, d=d):
            return jnp.broadcast_to(x[:, d:d + 1], (x.shape[0], LANES))
        qw_s[d] = (q * lanes(w_inter)).astype(BF16)
        kw_s[d] = (k * lanes(wk)).astype(BF16)
        mx_s[d] = lanes(mx)
        en_s[d] = lanes(en)
        dec_s[d] = lanes(decay)

    for d in range(2):
        reverse = d == 1
        mask = _tri_mask(chunk, reverse)

        def chunk_rows(c):
            return pl.ds(pl.multiple_of(c * chunk, chunk), chunk)

        def upd_body(c, carry, d=d):
            rows = chunk_rows(c)
            upd_s[c] = _dot_tn(kw_s[d, rows, :], va_s[rows, :])
            return carry

        lax.fori_loop(0, n_chunks, upd_body, 0, unroll=ML_UNROLL)

        def rec_body(ci, ct, reverse=reverse, d=d):
            c = (n_chunks - 1 - ci) if reverse else ci
            ct_s[c] = ct.astype(BF16)
            dec = dec_s[d, pl.ds(c, 1), :]
            return ct * jnp.concatenate([dec, dec], axis=1) + upd_s[c]

        lax.fori_loop(0, n_chunks, rec_body, jnp.zeros((HEAD, 2 * HEAD), F32))

        def out_body(c, carry, mask=mask, d=d):
            rows = chunk_rows(c)
            p = jnp.exp(jnp.where(mask, gt_s[c][d:d + 1, :] - mx_s[d, rows, :], NEG))
            sc = (_dot_nt(qb_s[rows, :], kb_s[rows, :]) * p).astype(BF16)
            tot = _dot(jnp.concatenate([qw_s[d, rows, :], sc], axis=1),
                       jnp.concatenate([ct_s[c], va_s[rows, :]], axis=0))
            hval = tot[:, 0:HEAD] / jnp.maximum(jnp.abs(tot[:, HEAD:2 * HEAD]), en_s[d, rows, :])
            if d == 0:
                acc_s[rows, :] = hval
            else:
                acc_s[rows, :] = acc_s[rows, :] + hval
            return carry

        lax.fori_loop(0, n_chunks, out_body, 0, unroll=ML_UNROLL)

    hsum = acc_s[...]
    hc = hsum - jnp.mean(hsum, axis=-1, keepdims=True)
    hn = hc * lax.rsqrt(jnp.mean(hc * hc, axis=-1, keepdims=True) + EPS) * ng_ref[...]
    o_ref[...] = hn * _sigmoid(o_ref[...])


def _mlstm_call(u, w, bias, ng, *, chunk=ML_CHUNK):
    n_b, seq, d = u.shape
    n_chunks = seq // chunk
    kern = functools.partial(_mlstm_kernel, seq=seq, chunk=chunk)
    return pl.pallas_call(
        kern,
        out_shape=jax.ShapeDtypeStruct((n_b, seq, ML_H * HEAD), F32),
        grid=(n_b, ML_H),
        in_specs=[pl.BlockSpec((None, seq, d), lambda b, h: (b, 0, 0)),
                  pl.BlockSpec((None, d, 6 * HEAD), lambda b, h: (h, 0, 0)),
                  pl.BlockSpec((None, 2, HEAD), lambda b, h: (h, 0, 0)),
                  pl.BlockSpec((None, 1, HEAD), lambda b, h: (h, 0, 0))],
        out_specs=pl.BlockSpec((None, seq, HEAD), lambda b, h: (b, 0, h)),
        scratch_shapes=[pltpu.VMEM((seq, HEAD), BF16),
                        pltpu.VMEM((seq, HEAD), BF16),
                        pltpu.VMEM((seq, 2 * HEAD), BF16),
                        pltpu.VMEM((seq, LANES), F32),
                        pltpu.VMEM((seq, LANES), F32),
                        pltpu.VMEM((2, seq, HEAD), BF16),
                        pltpu.VMEM((2, seq, HEAD), BF16),
                        pltpu.VMEM((2, seq, LANES), F32),
                        pltpu.VMEM((2, seq, LANES), F32),
                        pltpu.VMEM((n_chunks, SUBLANES, chunk), F32),
                        pltpu.VMEM((2, n_chunks, LANES), F32),
                        pltpu.VMEM((n_chunks, HEAD, 2 * HEAD), F32),
                        pltpu.VMEM((n_chunks, HEAD, 2 * HEAD), BF16),
                        pltpu.VMEM((seq, HEAD), F32)],
        compiler_params=pltpu.CompilerParams(dimension_semantics=("parallel", "arbitrary"),
                                             vmem_limit_bytes=VMEM_LIMIT),
        name="mlstm_branch",
    )(u, w, bias, ng)


_SSD_W_Z = 0
_SSD_W_XBC = SSD_GW
_SSD_W_DT = SSD_GW + SSD_GW + 2 * SSD_N
_SSD_W_END = _SSD_W_DT + LANES
_SSD_XBC_G = SSD_GW + 2 * SSD_N
_SSD_PAD = SUBLANES
_SSD_PAIRS = SSD_HG // 2


def _pair_expand(blk, col0, lane_lo):
    return jnp.where(lane_lo, blk[:, col0:col0 + 1], blk[:, col0 + 1:col0 + 2])


def _ssd_kernel(u_ref, w_ref, cw_ref, cb_ref, dtb_ref, alog_ref, dsk_ref, ng_ref, o_ref,
                pad_s, xs_s, b_s, c_s, dt_s, a_s, at_s, cbm_s, st_s, *, seq, chunk, rblk):
    n_chunks = seq // chunk
    n_rblk = seq // rblk

    zero_rows = jnp.zeros((_SSD_PAD, _SSD_XBC_G), F32)
    pad_s[0:_SSD_PAD, :] = zero_rows
    pad_s[_SSD_PAD + seq:2 * _SSD_PAD + seq, :] = zero_rows
    for i in range(n_rblk):
        r0 = i * rblk
        pad_s[_SSD_PAD + r0:_SSD_PAD + r0 + rblk, :] = _dot(
            u_ref[r0:r0 + rblk, :], w_ref[:, _SSD_W_XBC:_SSD_W_DT])
    for i in range(n_rblk):
        r0 = i * rblk
        acc = cb_ref[...]
        for j in range(SSD_CONV):
            off = _SSD_PAD + r0 + j - SSD_CONV // 2
            acc = acc + pad_s[off:off + rblk, :] * cw_ref[j:j + 1, :]
        xbc = _silu(acc)
        xs_s[r0:r0 + rblk, :] = xbc[:, 0:SSD_GW]
        b_s[r0:r0 + rblk, :] = xbc[:, SSD_GW:SSD_GW + SSD_N].astype(BF16)
        c_s[r0:r0 + rblk, :] = xbc[:, SSD_GW + SSD_N:SSD_GW + 2 * SSD_N].astype(BF16)

    dt = _softplus(_dot(u_ref[...], w_ref[:, _SSD_W_DT:_SSD_W_END]) + dtb_ref[...])
    dt_s[...] = dt
    log_a = dt * (-jnp.exp(alog_ref[...]))
    lane = lax.broadcasted_iota(jnp.int32, (seq, LANES), 1)
    acum = jnp.where(lane < SSD_HG,
                     _chunk_scan(log_a, chunk, False, "add"),
                     _chunk_scan(log_a, chunk, True, "add"))
    a_s[...] = acum
    acum_t = acum.T
    for c in range(n_chunks):
        at_s[c] = acum_t[0:2 * SSD_HG, c * chunk:(c + 1) * chunk]
        r0 = c * chunk
        cbm_s[r0:r0 + chunk, :] = _dot_nt(c_s[r0:r0 + chunk, :], b_s[r0:r0 + chunk, :])

    lane_lo = lax.broadcasted_iota(jnp.int32, (chunk, LANES), 1) < SSD_HEAD_DIM
    lane_lo_row = lax.broadcasted_iota(jnp.int32, (1, LANES), 1) < SSD_HEAD_DIM

    for d in range(2):
        reverse = d == 1
        mask = _tri_mask(chunk, reverse)
        end_row = 0 if reverse else chunk - 1
        st_s[...] = jnp.zeros_like(st_s)

        def body(ci, carry, reverse=reverse, mask=mask, end_row=end_row, d=d):
            c = (n_chunks - 1 - ci) if reverse else ci
            r0 = pl.multiple_of(c * chunk, chunk)
            rows = pl.ds(r0, chunk)
            blk = a_s[rows, :]
            blk_t = at_s[c]
            dt_blk = dt_s[rows, :]
            cbm = cbm_s[rows, :]
            cc = c_s[rows, :]
            bc = b_s[rows, :]
            y_inter = _dot(cc, st_s[...].astype(BF16))
            end_blk = blk[end_row:end_row + 1, :]
            xw_tiles = []
            for p in range(_SSD_PAIRS):
                col0 = d * SSD_HG + 2 * p
                cols = slice(p * LANES, (p + 1) * LANES)
                a_pair = _pair_expand(blk, col0, lane_lo)
                xdt = xs_s[rows, cols] * _pair_expand(dt_blk, col0, lane_lo)
                y = y_inter[:, cols] * jnp.exp(a_pair)
                for hh in range(2):
                    col = col0 + hh
                    seg = jnp.exp(jnp.where(mask, blk[:, col:col + 1] - blk_t[col:col + 1, :], NEG))
                    keep = lane_lo if hh == 0 else jnp.logical_not(lane_lo)
                    x_h = jnp.where(keep, xdt, 0.0).astype(BF16)
                    y = y + _dot((cbm * seg).astype(BF16), x_h)
                if d == 0:
                    o_ref[rows, cols] = y
                else:
                    o_ref[rows, cols] = o_ref[rows, cols] + y
                end_pair = jnp.where(lane_lo_row, end_blk[:, col0:col0 + 1], end_blk[:, col0 + 1:col0 + 2])
                xw_tiles.append((xdt * jnp.exp(end_pair - a_pair)).astype(BF16))
                st_s[:, cols] = st_s[:, cols] * jnp.exp(end_pair)
            st_s[...] = st_s[...] + _dot_tn(bc, jnp.concatenate(xw_tiles, axis=1))
            return carry

        lax.fori_loop(0, n_chunks, body, 0)

    for i in range(n_rblk):
        r0 = i * rblk
        rows = slice(r0, r0 + rblk)
        z = _dot(u_ref[rows, :], w_ref[:, _SSD_W_Z:_SSD_W_XBC])
        y = (o_ref[rows, :] + xs_s[rows, :] * dsk_ref[...]) * _silu(z)
        o_ref[rows, :] = y * lax.rsqrt(jnp.mean(y * y, axis=-1, keepdims=True) + EPS) * ng_ref[...]


def _ssd_call(u, w, cw, cb, dtb, alog, dsk, ng, *, chunk=SSD_CHUNK):
    n_b, seq, d = u.shape
    rblk = min(seq, 512)
    kern = functools.partial(_ssd_kernel, seq=seq, chunk=chunk, rblk=rblk)
    vec = lambda n: pl.BlockSpec((None, 1, n), lambda b, g: (g, 0, 0))
    return pl.pallas_call(
        kern,
        out_shape=jax.ShapeDtypeStruct((n_b, seq, SSD_G * SSD_GW), F32),
        grid=(n_b, SSD_G),
        in_specs=[pl.BlockSpec((None, seq, d), lambda b, g: (b, 0, 0)),
                  pl.BlockSpec((None, d, _SSD_W_END), lambda b, g: (g, 0, 0)),
                  pl.BlockSpec((None, SSD_CONV, _SSD_XBC_G), lambda b, g: (g, 0, 0)),
                  vec(_SSD_XBC_G), vec(LANES), vec(LANES), vec(SSD_GW), vec(SSD_GW)],
        out_specs=pl.BlockSpec((None, seq, SSD_GW), lambda b, g: (b, 0, g)),
        scratch_shapes=[pltpu.VMEM((seq + 2 * _SSD_PAD, _SSD_XBC_G), F32),
                        pltpu.VMEM((seq, SSD_GW), F32),
                        pltpu.VMEM((seq, SSD_N), BF16),
                        pltpu.VMEM((seq, SSD_N), BF16),
                        pltpu.VMEM((seq, LANES), F32),
                        pltpu.VMEM((seq, LANES), F32),
                        pltpu.VMEM((seq // chunk, 2 * SSD_HG, chunk), F32),
                        pltpu.VMEM((seq, chunk), F32),
                        pltpu.VMEM((SSD_N, SSD_GW), F32)],
        compiler_params=pltpu.CompilerParams(dimension_semantics=("parallel", "arbitrary"),
                                             vmem_limit_bytes=VMEM_LIMIT),
        name="ssd_branch",
    )(u, w, cw, cb, dtb, alog, dsk, ng)


def _merge_kernel(u_ref, h_ref, y0_ref, y1_ref, y2_ref, wg_ref, wb_ref, wo_ref, o_ref, acc_s):
    br = pl.program_id(1)
    gate = _sigmoid(_dot(u_ref[...], wg_ref[...]))

    def branch(y_ref):
        return gate * _dot(y_ref[...].astype(BF16), wb_ref[...])

    @pl.when(br == 0)
    def _():
        acc_s[...] = branch(y0_ref)

    @pl.when(br == 1)
    def _():
        acc_s[...] = acc_s[...] + branch(y1_ref)

    @pl.when(br == 2)
    def _():
        merged = acc_s[...] + branch(y2_ref)
        o_ref[...] = h_ref[...] + _dot(merged.astype(BF16), wo_ref[...])


def _merge_call(u2d, h2d, y0, y1, y2, wg, wb, wo, tm=512):
    n_tok, d = h2d.shape
    tm = min(tm, n_tok)
    tok = lambda: pl.BlockSpec((tm, d), lambda i, r: (i, 0))
    return pl.pallas_call(
        _merge_kernel,
        out_shape=jax.ShapeDtypeStruct((n_tok, d), F32),
        grid=(n_tok // tm, 3),
        in_specs=[tok(), tok(), tok(), tok(), tok(),
                  pl.BlockSpec((None, d, d), lambda i, r: (r, 0, 0)),
                  pl.BlockSpec((None, d, d), lambda i, r: (r, 0, 0)),
                  pl.BlockSpec((d, d), lambda i, r: (0, 0))],
        out_specs=tok(),
        scratch_shapes=[pltpu.VMEM((tm, d), F32)],
        compiler_params=pltpu.CompilerParams(dimension_semantics=("parallel", "arbitrary"),
                                             vmem_limit_bytes=VMEM_LIMIT),
        name="merge",
    )(u2d, h2d, y0, y1, y2, wg, wb, wo)


_FFN_CW = 256
_FFN_NJ = D_FF // _FFN_CW
_FFN_HALO = SUBLANES


def _ffn_kernel(h_ref, hp_ref, hn_ref, p_ref, g_ref, wa_ref, wv_ref, cwa_ref, cwv_ref,
                cba_ref, cbv_ref, wd_ref, wple_ref, wpg_ref, fg_ref, o_ref, u_s,
                *, tm, n_tiles, final_norm):
    i = pl.program_id(1)
    j = pl.program_id(2)

    def norm(x):
        return (x * lax.rsqrt(jnp.mean(x * x, axis=-1, keepdims=True) + EPS) * g_ref[...]).astype(BF16)

    @pl.when(j == 0)
    def _():
        up = jnp.where(i > 0, 1.0, 0.0)
        dn = jnp.where(i < n_tiles - 1, 1.0, 0.0)
        u_s[0:_FFN_HALO, :] = norm(hp_ref[...] * up)
        u_s[_FFN_HALO:_FFN_HALO + tm, :] = norm(h_ref[...])
        u_s[_FFN_HALO + tm:2 * _FFN_HALO + tm, :] = norm(hn_ref[...] * dn)

    def conv(w_ref, cw_ref, cb_ref):
        up = _dot(u_s[...], w_ref[...])
        acc = cb_ref[...]
        for t in range(FFN_CONV):
            off = _FFN_HALO + t - FFN_CONV // 2
            acc = acc + up[off:off + tm, :] * cw_ref[t:t + 1, :]
        return acc

    act = (_silu(conv(wa_ref, cwa_ref, cba_ref)) * conv(wv_ref, cwv_ref, cbv_ref)).astype(BF16)
    part = _dot(act, wd_ref[...])

    @pl.when(j == 0)
    def _():
        o_ref[...] = h_ref[...] + part

    @pl.when(j > 0)
    def _():
        o_ref[...] = o_ref[...] + part

    @pl.when(j == _FFN_NJ - 1)
    def _():
        h2 = o_ref[...]
        ple = _dot(p_ref[...].astype(BF16), wple_ref[...])
        h3 = h2 + ple * _sigmoid(_dot(h2.astype(BF16), wpg_ref[...]))
        if final_norm:
            h3 = h3 * lax.rsqrt(jnp.mean(h3 * h3, axis=-1, keepdims=True) + EPS) * fg_ref[...]
        o_ref[...] = h3


def _ffn_call(h, p, g, w_up, cw, cb, w_down, w_ple, w_pg, fg, *, final_norm, tm=1024):
    n_b, seq, d = h.shape
    tm = min(tm, seq)
    n_tiles = seq // tm
    hb = tm // _FFN_HALO
    n_hb = seq // _FFN_HALO
    kern = functools.partial(_ffn_kernel, tm=tm, n_tiles=n_tiles, final_norm=final_norm)
    cst = lambda shape: pl.BlockSpec(shape, lambda b, i, j: tuple(0 for _ in shape))
    return pl.pallas_call(
        kern,
        out_shape=jax.ShapeDtypeStruct((n_b, seq, d), F32),
        grid=(n_b, n_tiles, _FFN_NJ),
        in_specs=[pl.BlockSpec((None, tm, d), lambda b, i, j: (b, i, 0)),
                  pl.BlockSpec((None, _FFN_HALO, d), lambda b, i, j: (b, jnp.maximum(i * hb - 1, 0), 0)),
                  pl.BlockSpec((None, _FFN_HALO, d), lambda b, i, j: (b, jnp.minimum((i + 1) * hb, n_hb - 1), 0)),
                  pl.BlockSpec((None, tm, PLE_DIM), lambda b, i, j: (b, i, 0)),
                  cst((1, d)),
                  pl.BlockSpec((d, _FFN_CW), lambda b, i, j: (0, j)),
                  pl.BlockSpec((d, _FFN_CW), lambda b, i, j: (0, _FFN_NJ + j)),
                  pl.BlockSpec((FFN_CONV, _FFN_CW), lambda b, i, j: (0, j)),
                  pl.BlockSpec((FFN_CONV, _FFN_CW), lambda b, i, j: (0, _FFN_NJ + j)),
                  pl.BlockSpec((1, _FFN_CW), lambda b, i, j: (0, j)),
                  pl.BlockSpec((1, _FFN_CW), lambda b, i, j: (0, _FFN_NJ + j)),
                  pl.BlockSpec((_FFN_CW, d), lambda b, i, j: (j, 0)),
                  cst((PLE_DIM, d)), cst((d, d)), cst((1, d))],
        out_specs=pl.BlockSpec((None, tm, d), lambda b, i, j: (b, i, 0)),
        scratch_shapes=[pltpu.VMEM((tm + 2 * _FFN_HALO, d), BF16)],
        compiler_params=pltpu.CompilerParams(
            dimension_semantics=("parallel", "parallel", "arbitrary"),
            vmem_limit_bytes=VMEM_LIMIT),
        name="convffn_ple",
    )(h, h, h, p, g.reshape(1, d), w_up, w_up, cw, cw, cb.reshape(1, -1), cb.reshape(1, -1),
      w_down, w_ple, w_pg, fg.reshape(1, d))


def _cols(w, start, width):
    return lax.slice_in_dim(w, start, start + width, axis=1)


def _pad_cols(w, width):
    return jnp.pad(w, ((0, 0), (0, width - w.shape[1])))


def _ssd_params(w_in, conv_w, conv_b, dt_bias, a_log, d_skip, norm_g):
    ws, cws, cbs, dtbs, alogs = [], [], [], [], []
    for g in range(SSD_G):
        dt_cols = jnp.concatenate(
            [_cols(w_in, _OFF_DT + dd * SSD_H + g * SSD_HG, SSD_HG) for dd in range(2)], axis=1)
        ws.append(jnp.concatenate([
            _cols(w_in, _OFF_Z + g * SSD_GW, SSD_GW),
            _cols(w_in, _OFF_XBC + g * SSD_GW, SSD_GW),
            _cols(w_in, _OFF_XBC + D_MODEL + g * SSD_N, SSD_N),
            _cols(w_in, _OFF_XBC + D_MODEL + SSD_G * SSD_N + g * SSD_N, SSD_N),
            _pad_cols(dt_cols, LANES)], axis=1))
        pick = lambda a: jnp.concatenate([
            _cols(a, g * SSD_GW, SSD_GW),
            _cols(a, D_MODEL + g * SSD_N, SSD_N),
            _cols(a, D_MODEL + SSD_G * SSD_N + g * SSD_N, SSD_N)], axis=1)
        cws.append(pick(conv_w))
        cbs.append(pick(conv_b.reshape(1, -1)))
        head_row = lambda a: _pad_cols(
            jnp.concatenate([a[dd, g * SSD_HG:(g + 1) * SSD_HG] for dd in range(2)]).reshape(1, -1), LANES)
        dtbs.append(head_row(dt_bias))
        alogs.append(head_row(a_log))
    dsk = jnp.repeat(d_skip, SSD_HEAD_DIM).reshape(SSD_G, 1, SSD_GW)
    return (jnp.stack(ws).astype(BF16), jnp.stack(cws), jnp.stack(cbs), jnp.stack(dtbs),
            jnp.stack(alogs), dsk, norm_g.reshape(SSD_G, 1, SSD_GW))


def _hgrn2_params(w_in, lb, norm_g):
    ws = [jnp.concatenate([
        _cols(w_in, _OFF_GQ + h * HEAD, HEAD),
        _cols(w_in, _OFF_GF + h * HEAD, HEAD),
        _cols(w_in, _OFF_GF + D_MODEL + h * HEAD, HEAD),
        _cols(w_in, _OFF_GI + h * HEAD, HEAD),
        _cols(w_in, _OFF_GG + h * HEAD, HEAD)], axis=1) for h in range(HG_H)]
    lbs = lb.reshape(2, HG_H, HEAD).swapaxes(0, 1)
    return jnp.stack(ws).astype(BF16), lbs, norm_g.reshape(HG_H, 1, HEAD)


def _mlstm_params(w_in, i_bias, f_bias, norm_g):
    ws, biases = [], []
    for h in range(ML_H):
        both_dirs = lambda off: jnp.concatenate([_cols(w_in, off + h, 1), _cols(w_in, off + ML_H + h, 1)], axis=1)
        ws.append(jnp.concatenate([
            _cols(w_in, _OFF_MQ + h * HEAD, HEAD),
            _cols(w_in, _OFF_MK + h * HEAD, HEAD),
            _cols(w_in, _OFF_MV + h * HEAD, HEAD),
            _cols(w_in, _OFF_MO + h * HEAD, HEAD),
            _pad_cols(both_dirs(_OFF_MI), LANES),
            _pad_cols(both_dirs(_OFF_MF), LANES)], axis=1))
        biases.append(_pad_cols(jnp.stack([i_bias[:, h], f_bias[:, h]]), LANES))
    return jnp.stack(ws).astype(BF16), jnp.stack(biases), norm_g.reshape(ML_H, 1, HEAD)


def kernel(x, p, norm_mix_g, w_in, ssd_conv_w, ssd_conv_b, ssd_dt_bias, ssd_a_log, ssd_d, ssd_norm_g, hg_lb_raw, hg_norm_g, ml_i_bias, ml_f_bias, ml_norm_g, w_br_ssd, w_br_hg, w_br_ml, w_out, norm_ffn_g, w_up, ffn_conv_w, ffn_conv_b, w_down, w_ple, w_ple_gate, final_norm_g):
    n_b, seq, d = x.shape
    depth = w_in.shape[0]
    lb_soft = jax.nn.softmax(hg_lb_raw.astype(F32), axis=0)
    hg_lb = jnp.cumsum(lb_soft, axis=0) - lb_soft[0:1]
    h = x
    for l in range(depth):
        u2d = _rmsnorm_call(h.reshape(n_b * seq, d), norm_mix_g[l], BF16)
        u = u2d.reshape(n_b, seq, d)
        y_ssd = _ssd_call(u, *_ssd_params(w_in[l], ssd_conv_w[l], ssd_conv_b[l], ssd_dt_bias[l],
                                          ssd_a_log[l], ssd_d[l], ssd_norm_g[l]))
        y_hg = _hgrn2_call(u, *_hgrn2_params(w_in[l], hg_lb[l], hg_norm_g[l]))
        y_ml = _mlstm_call(u, *_mlstm_params(w_in[l], ml_i_bias[l], ml_f_bias[l], ml_norm_g[l]))
        wg = jnp.stack([_cols(w_in[l], _OFF_GATES + r * d, d) for r in range(3)]).astype(BF16)
        wb = jnp.stack([w_br_ssd[l], w_br_hg[l], w_br_ml[l]]).astype(BF16)
        tok = lambda a: a.reshape(n_b * seq, d)
        h = _merge_call(u2d, tok(h), tok(y_ssd), tok(y_hg), tok(y_ml), wg, wb,
                        w_out[l].astype(BF16)).reshape(n_b, seq, d)
        h = _ffn_call(h, p[l], norm_ffn_g[l], w_up[l].astype(BF16), ffn_conv_w[l], ffn_conv_b[l],
                      w_down[l].astype(BF16), w_ple[l].astype(BF16), w_ple_gate[l].astype(BF16),
                      final_norm_g, final_norm=(l == depth - 1))
    return h
```

```python
import functools

import jax
import jax.numpy as jnp
from jax import lax
from jax.experimental import pallas as pl
from jax.experimental.pallas import tpu as pltpu

F32 = jnp.float32
BF16 = jnp.bfloat16

D_MODEL = 1024
PLE_DIM = 256
EPS = 1e-6
NEG = -1e30
TINY = 1e-30

SSD_HEAD_DIM = 64
SSD_H = 16
SSD_G = 2
SSD_HG = 8
SSD_N = 128
SSD_CONV = 5
SSD_GW = SSD_HG * SSD_HEAD_DIM
SSD_XBC = D_MODEL + 2 * SSD_G * SSD_N
HG_H = 8
ML_H = 8
HEAD = 128
D_FF = 2816
FFN_CONV = 3

LANES = 128
SUBLANES = 8
VMEM_LIMIT = 56 * 1024 * 1024

SSD_CHUNK = 128
HG_CHUNK = 64
ML_CHUNK = 128
ML_UNROLL = 8
HG_UNROLL = 16
HG_SAFE_DECAY = 60.0

_OFF_Z = 0
_OFF_XBC = 1024
_OFF_DT = _OFF_XBC + SSD_XBC
_OFF_GQ = _OFF_DT + 2 * SSD_H
_OFF_GF = _OFF_GQ + 1024
_OFF_GI = _OFF_GF + 2048
_OFF_GG = _OFF_GI + 1024
_OFF_MQ = _OFF_GG + 1024
_OFF_MK = _OFF_MQ + 1024
_OFF_MV = _OFF_MK + 1024
_OFF_MI = _OFF_MV + 1024
_OFF_MF = _OFF_MI + 2 * ML_H
_OFF_MO = _OFF_MF + 2 * ML_H
_OFF_GATES = _OFF_MO + 1024


def _dot(a, b):
    return jnp.dot(a, b, preferred_element_type=F32)


def _dot_nt(a, b):
    return lax.dot_general(a, b, (((1,), (1,)), ((), ())), preferred_element_type=F32)


def _dot_tn(a, b):
    return lax.dot_general(a, b, (((0,), (0,)), ((), ())), preferred_element_type=F32)


def _sigmoid(x):
    return 1.0 / (1.0 + jnp.exp(-x))


def _silu(x):
    return x * _sigmoid(x)


def _softplus(x):
    return jnp.maximum(x, 0.0) + jnp.log1p(jnp.exp(-jnp.abs(x)))


def _chunk_scan(x, chunk, reverse, op):
    n_rows = x.shape[0]
    row = lax.broadcasted_iota(jnp.int32, x.shape, 0) & (chunk - 1)
    sh = 1
    while sh < chunk:
        if reverse:
            shifted = pltpu.roll(x, n_rows - sh, axis=0)
            ok = row < chunk - sh
        else:
            shifted = pltpu.roll(x, sh, axis=0)
            ok = row >= sh
        if op == "add":
            x = x + jnp.where(ok, shifted, 0.0)
        else:
            x = jnp.maximum(x, jnp.where(ok, shifted, NEG))
        sh *= 2
    return x


def _tri_mask(n, reverse):
    t = lax.broadcasted_iota(jnp.int32, (n, n), 0)
    s = lax.broadcasted_iota(jnp.int32, (n, n), 1)
    return (s >= t) if reverse else (s <= t)


def _rmsnorm_kernel(h_ref, g_ref, o_ref):
    x = h_ref[...]
    ms = jnp.mean(x * x, axis=-1, keepdims=True)
    o_ref[...] = (x * lax.rsqrt(ms + EPS) * g_ref[...]).astype(o_ref.dtype)


def _rmsnorm_call(h2d, g, out_dtype, tm=1024):
    n_tok, d = h2d.shape
    tm = min(tm, n_tok)
    return pl.pallas_call(
        _rmsnorm_kernel,
        out_shape=jax.ShapeDtypeStruct((n_tok, d), out_dtype),
        grid=(n_tok // tm,),
        in_specs=[pl.BlockSpec((tm, d), lambda i: (i, 0)),
                  pl.BlockSpec((1, d), lambda i: (0, 0))],
        out_specs=pl.BlockSpec((tm, d), lambda i: (i, 0)),
        compiler_params=pltpu.CompilerParams(dimension_semantics=("parallel",),
                                             vmem_limit_bytes=VMEM_LIMIT),
        name="rmsnorm",
    )(h2d, g.reshape(1, d))


def _hg_att_safe(qc, kc, lfc, reverse):
    n = qc.shape[0]
    t = lax.broadcasted_iota(jnp.int32, (n, n), 0)
    s = lax.broadcasted_iota(jnp.int32, (n, n), 1)
    att = jnp.where(t == s, jnp.sum(qc * kc, axis=-1, keepdims=True), 0.0)
    half = 1
    while half < n:
        pre = _chunk_scan(lfc, half, False, "add")
        suf = _chunk_scan(lfc, half, True, "add")
        t_hi = (t & half) != 0
        s_hi = (s & half) != 0
        same_block = (t ^ s) < 2 * half
        if reverse:
            qe = qc * jnp.exp(suf)
            ke = kc * jnp.exp(pre - lfc)
            pair = jnp.logical_and(same_block, jnp.logical_and(jnp.logical_not(t_hi), s_hi))
        else:
            qe = qc * jnp.exp(pre)
            ke = kc * jnp.exp(suf - lfc)
            pair = jnp.logical_and(same_block, jnp.logical_and(t_hi, jnp.logical_not(s_hi)))
        att = att + jnp.where(pair, _dot_nt(qe.astype(BF16), ke.astype(BF16)), 0.0)
        half *= 2
    return att


def _hgrn2_kernel(u_ref, w_ref, lb_ref, ng_ref, o_ref,
                  q_s, v_s, k_s, lf_s, b_s, qe_s, ke_s, qi_s, kd_s, dec_s, upd_s, st_s, acc_s,
                  *, seq, chunk):
    n_chunks = seq // chunk
    proj = _dot(u_ref[...], w_ref[...])
    q_s[...] = _silu(proj[:, 0:HEAD])
    v_s[...] = proj[:, 3 * HEAD:4 * HEAD].astype(BF16)
    o_ref[...] = proj[:, 4 * HEAD:5 * HEAD]

    for d in range(2):
        f_raw = proj[:, (1 + d) * HEAD:(2 + d) * HEAD]
        lb = lb_ref[d:d + 1, :]
        f = lb + (1.0 - lb) * _sigmoid(f_raw)
        k_s[d] = (1.0 - lb) * _sigmoid(-f_raw)
        log_f = jnp.log(jnp.maximum(f, TINY))
        lf_s[d] = log_f
        b = _chunk_scan(log_f, chunk, d == 1, "add")
        b_s[d] = b

    def ref_span(d):
        first, mid, last = (b_s[d, pl.ds(r, n_chunks, stride=chunk), :] for r in (0, chunk // 2, chunk - 1))
        return jnp.maximum(jnp.abs(first - mid), jnp.abs(mid - last))

    mild_decay = jnp.max(jnp.maximum(ref_span(0), ref_span(1))) <= HG_SAFE_DECAY

    for d in range(2):
        reverse = d == 1
        mask = _tri_mask(chunk, reverse)
        end_row = 0 if reverse else chunk - 1

        def chunk_rows(c):
            r0 = pl.multiple_of(c * chunk, chunk)
            return r0, pl.ds(r0, chunk)

        def scale_body(c, carry, d=d, end_row=end_row):
            r0, rows = chunk_rows(c)
            qc = q_s[rows, :]
            kc = k_s[d, rows, :]
            bc = b_s[d, rows, :]
            ref = b_s[d, pl.ds(r0 + chunk // 2, 1), :]
            bend = b_s[d, pl.ds(r0 + end_row, 1), :]
            qe_s[rows, :] = (qc * jnp.exp(bc - ref)).astype(BF16)
            ke_s[rows, :] = (kc * jnp.exp(ref - bc)).astype(BF16)
            qi_s[rows, :] = (qc * jnp.exp(bc)).astype(BF16)
            kd_s[rows, :] = (kc * jnp.exp(bend - bc)).astype(BF16)
            dec_s[c] = jnp.broadcast_to(jnp.exp(bend), (HEAD, HEAD)).T
            return carry

        lax.fori_loop(0, n_chunks, scale_body, 0, unroll=HG_UNROLL)

        def upd_body(c, carry):
            _, rows = chunk_rows(c)
            upd_s[c] = _dot_tn(kd_s[rows, :], v_s[rows, :])
            return carry

        lax.fori_loop(0, n_chunks, upd_body, 0, unroll=HG_UNROLL)

        def rec_body(ci, st, reverse=reverse):
            c = (n_chunks - 1 - ci) if reverse else ci
            st_s[c] = st.astype(BF16)
            return st * dec_s[c] + upd_s[c]

        lax.fori_loop(0, n_chunks, rec_body, jnp.zeros((HEAD, HEAD), F32))

        def out_body(c, carry, fast, d=d, mask=mask, reverse=reverse):
            _, rows = chunk_rows(c)
            if fast:
                att = jnp.where(mask, _dot_nt(qe_s[rows, :], ke_s[rows, :]), 0.0)
            else:
                att = _hg_att_safe(q_s[rows, :], k_s[d, rows, :], lf_s[d, rows, :], reverse)
            o = _dot(jnp.concatenate([qi_s[rows, :], att.astype(BF16)], axis=1),
                     jnp.concatenate([st_s[c], v_s[rows, :]], axis=0))
            if d == 0:
                acc_s[rows, :] = o
            else:
                acc_s[rows, :] = acc_s[rows, :] + o
            return carry

        @pl.when(mild_decay)
        def _():
            lax.fori_loop(0, n_chunks, functools.partial(out_body, fast=True), 0, unroll=HG_UNROLL)

        @pl.when(jnp.logical_not(mild_decay))
        def _():
            lax.fori_loop(0, n_chunks, functools.partial(out_body, fast=False), 0)

    o = acc_s[...]
    o = o * lax.rsqrt(jnp.mean(o * o, axis=-1, keepdims=True) + EPS) * ng_ref[...]
    o_ref[...] = o * _silu(o_ref[...])


def _hgrn2_call(u, w, lb, ng, *, chunk=HG_CHUNK):
    n_b, seq, d = u.shape
    kern = functools.partial(_hgrn2_kernel, seq=seq, chunk=chunk)
    return pl.pallas_call(
        kern,
        out_shape=jax.ShapeDtypeStruct((n_b, seq, HG_H * HEAD), F32),
        grid=(n_b, HG_H),
        in_specs=[pl.BlockSpec((None, seq, d), lambda b, h: (b, 0, 0)),
                  pl.BlockSpec((None, d, 5 * HEAD), lambda b, h: (h, 0, 0)),
                  pl.BlockSpec((None, 2, HEAD), lambda b, h: (h, 0, 0)),
                  pl.BlockSpec((None, 1, HEAD), lambda b, h: (h, 0, 0))],
        out_specs=pl.BlockSpec((None, seq, HEAD), lambda b, h: (b, 0, h)),
        scratch_shapes=[pltpu.VMEM((seq, HEAD), F32),
                        pltpu.VMEM((seq, HEAD), BF16),
                        pltpu.VMEM((2, seq, HEAD), F32),
                        pltpu.VMEM((2, seq, HEAD), F32),
                        pltpu.VMEM((2, seq, HEAD), F32),
                        pltpu.VMEM((seq, HEAD), BF16),
                        pltpu.VMEM((seq, HEAD), BF16),
                        pltpu.VMEM((seq, HEAD), BF16),
                        pltpu.VMEM((seq, HEAD), BF16),
                        pltpu.VMEM((seq // chunk, HEAD, HEAD), F32),
                        pltpu.VMEM((seq // chunk, HEAD, HEAD), F32),
                        pltpu.VMEM((seq // chunk, HEAD, HEAD), BF16),
                        pltpu.VMEM((seq, HEAD), F32)],
        compiler_params=pltpu.CompilerParams(dimension_semantics=("parallel", "arbitrary"),
                                             vmem_limit_bytes=VMEM_LIMIT),
        name="hgrn2_branch",
    )(u, w, lb, ng)


def _log_sigmoid(x):
    return jnp.minimum(x, 0.0) - jnp.log1p(jnp.exp(-jnp.abs(x)))


def _mlstm_kernel(u_ref, w_ref, bias_ref, ng_ref, o_ref,
                  qb_s, kb_s, va_s, bt_s, cm_s, qw_s, kw_s, mx_s, en_s, gt_s, dec_s, upd_s, ct_s,
                  acc_s, *, seq, chunk):
    n_chunks = seq // chunk
    proj = _dot(u_ref[...], w_ref[...])
    q = proj[:, 0:HEAD]
    k = proj[:, HEAD:2 * HEAD] * (HEAD ** -0.5)
    qb_s[...] = q.astype(BF16)
    kb_s[...] = k.astype(BF16)
    va_s[:, 0:HEAD] = proj[:, 2 * HEAD:3 * HEAD].astype(BF16)
    va_s[:, HEAD:2 * HEAD] = jnp.ones((seq, HEAD), BF16)
    o_ref[...] = proj[:, 3 * HEAD:4 * HEAD]

    fwd_lane = lax.broadcasted_iota(jnp.int32, (seq, LANES), 1) == 0
    it = proj[:, 4 * HEAD:5 * HEAD] + bias_ref[0:1, :]
    log_f = _log_sigmoid(proj[:, 5 * HEAD:6 * HEAD] + bias_ref[1:2, :])
    bt = jnp.where(fwd_lane, _chunk_scan(log_f, chunk, False, "add"), _chunk_scan(log_f, chunk, True, "add"))
    g = it - bt
    cm = jnp.where(fwd_lane, _chunk_scan(g, chunk, False, "max"), _chunk_scan(g, chunk, True, "max"))
    bt_s[...] = bt
    cm_s[...] = cm

    fwd_lane_c = lax.broadcasted_iota(jnp.int32, (n_chunks, LANES), 1) == 0
    chunk_id = lax.broadcasted_iota(jnp.int32, (n_chunks, LANES), 0)

    def chunk_ends(ref):
        return jnp.where(fwd_lane_c, ref[pl.ds(chunk - 1, n_chunks, stride=chunk), :],
                         ref[pl.ds(0, n_chunks, stride=chunk), :])

    def scan_chunks(x, op):
        return jnp.where(fwd_lane_c, _chunk_scan(x, n_chunks, False, op), _chunk_scan(x, n_chunks, True, op))

    bt_end = chunk_ends(bt_s)
    p_sum = scan_chunks(bt_end, "add")
    m_end = p_sum + scan_chunks(chunk_ends(cm_s) - (p_sum - bt_end), "max")
    m_prev = jnp.where(fwd_lane_c,
                       jnp.where(chunk_id == 0, NEG, pltpu.roll(m_end, 1, axis=0)),
                       jnp.where(chunk_id == n_chunks - 1, NEG, pltpu.roll(m_end, n_chunks - 1, axis=0)))
    decay = jnp.exp(bt_end + m_prev - m_end)

    def per_token(x):
        return jnp.concatenate([jnp.broadcast_to(x[c:c + 1, :], (chunk, LANES)) for c in range(n_chunks)],
                               axis=0)

    m_prev_t = per_token(m_prev)
    mx = jnp.maximum(m_prev_t, cm)
    w_inter = jnp.exp(m_prev_t - mx)
    en = jnp.exp(-(bt + mx))
    wk = jnp.exp(per_token(bt_end) - bt + it - per_token(m_end))

    g_t = g.T
    for c in range(n_chunks):
        gt_s[c] = g_t[0:SUBLANES, c * chunk:(c + 1) * chunk]
    for d in range(2):
        def lanes(x, d=d):
            return jnp.broadcast_to(x[:, d:d + 1], (x.shape[0], LANES))
        qw_s[d] = (q * lanes(w_inter)).astype(BF16)
        kw_s[d] = (k * lanes(wk)).astype(BF16)
        mx_s[d] = lanes(mx)
        en_s[d] = lanes(en)
        dec_s[d] = lanes(decay)

    for d in range(2):
        reverse = d == 1
        mask = _tri_mask(chunk, reverse)

        def chunk_rows(c):
            return pl.ds(pl.multiple_of(c * chunk, chunk), chunk)

        def upd_body(c, carry, d=d):
            rows = chunk_rows(c)
            upd_s[c] = _dot_tn(kw_s[d, rows, :], va_s[rows, :])
            return carry

        lax.fori_loop(0, n_chunks, upd_body, 0, unroll=ML_UNROLL)

        def rec_body(ci, ct, reverse=reverse, d=d):
            c = (n_chunks - 1 - ci) if reverse else ci
            ct_s[c] = ct.astype(BF16)
            dec = dec_s[d, pl.ds(c, 1), :]
            return ct * jnp.concatenate([dec, dec], axis=1) + upd_s[c]

        lax.fori_loop(0, n_chunks, rec_body, jnp.zeros((HEAD, 2 * HEAD), F32))

        def out_body(c, carry, mask=mask, d=d):
            rows = chunk_rows(c)
            p = jnp.exp(jnp.where(mask, gt_s[c][d:d + 1, :] - mx_s[d, rows, :], NEG))
            sc = (_dot_nt(qb_s[rows, :], kb_s[rows, :]) * p).astype(BF16)
            tot = _dot(jnp.concatenate([qw_s[d, rows, :], sc], axis=1),
                       jnp.concatenate([ct_s[c], va_s[rows, :]], axis=0))
            hval = tot[:, 0:HEAD] / jnp.maximum(jnp.abs(tot[:, HEAD:2 * HEAD]), en_s[d, rows, :])
            if d == 0:
                acc_s[rows, :] = hval
            else:
                acc_s[rows, :] = acc_s[rows, :] + hval
            return carry

        lax.fori_loop(0, n_chunks, out_body, 0, unroll=ML_UNROLL)

    hsum = acc_s[...]
    hc = hsum - jnp.mean(hsum, axis=-1, keepdims=True)
    hn = hc * lax.rsqrt(jnp.mean(hc * hc, axis=-1, keepdims=True) + EPS) * ng_ref[...]
    o_ref[...] = hn * _sigmoid(o_ref[...])


def _mlstm_call(u, w, bias, ng, *, chunk=ML_CHUNK):
    n_b, seq, d = u.shape
    n_chunks = seq // chunk
    kern = functools.partial(_mlstm_kernel, seq=seq, chunk=chunk)
    return pl.pallas_call(
        kern,
        out_shape=jax.ShapeDtypeStruct((n_b, seq, ML_H * HEAD), F32),
        grid=(n_b, ML_H),
        in_specs=[pl.BlockSpec((None, seq, d), lambda b, h: (b, 0, 0)),
                  pl.BlockSpec((None, d, 6 * HEAD), lambda b, h: (h, 0, 0)),
                  pl.BlockSpec((None, 2, HEAD), lambda b, h: (h, 0, 0)),
                  pl.BlockSpec((None, 1, HEAD), lambda b, h: (h, 0, 0))],
        out_specs=pl.BlockSpec((None, seq, HEAD), lambda b, h: (b, 0, h)),
        scratch_shapes=[pltpu.VMEM((seq, HEAD), BF16),
                        pltpu.VMEM((seq, HEAD), BF16),
                        pltpu.VMEM((seq, 2 * HEAD), BF16),
                        pltpu.VMEM((seq, LANES), F32),
                        pltpu.VMEM((seq, LANES), F32),
                        pltpu.VMEM((2, seq, HEAD), BF16),
                        pltpu.VMEM((2, seq, HEAD), BF16),
                        pltpu.VMEM((2, seq, LANES), F32),
                        pltpu.VMEM((2, seq, LANES), F32),
                        pltpu.VMEM((n_chunks, SUBLANES, chunk), F32),
                        pltpu.VMEM((2, n_chunks, LANES), F32),
                        pltpu.VMEM((n_chunks, HEAD, 2 * HEAD), F32),
                        pltpu.VMEM((n_chunks, HEAD, 2 * HEAD), BF16),
                        pltpu.VMEM((seq, HEAD), F32)],
        compiler_params=pltpu.CompilerParams(dimension_semantics=("parallel", "arbitrary"),
                                             vmem_limit_bytes=VMEM_LIMIT),
        name="mlstm_branch",
    )(u, w, bias, ng)


_SSD_W_Z = 0
_SSD_W_XBC = SSD_GW
_SSD_W_DT = SSD_GW + SSD_GW + 2 * SSD_N
_SSD_W_END = _SSD_W_DT + LANES
_SSD_XBC_G = SSD_GW + 2 * SSD_N
_SSD_PAD = SUBLANES
_SSD_PAIRS = SSD_HG // 2


def _pair_expand(blk, col0, lane_lo):
    return jnp.where(lane_lo, blk[:, col0:col0 + 1], blk[:, col0 + 1:col0 + 2])


def _ssd_kernel(u_ref, w_ref, cw_ref, cb_ref, dtb_ref, alog_ref, dsk_ref, ng_ref, o_ref,
                pad_s, xs_s, b_s, c_s, dt_s, a_s, at_s, cbm_s, st_s, *, seq, chunk, rblk):
    n_chunks = seq // chunk
    n_rblk = seq // rblk

    zero_rows = jnp.zeros((_SSD_PAD, _SSD_XBC_G), F32)
    pad_s[0:_SSD_PAD, :] = zero_rows
    pad_s[_SSD_PAD + seq:2 * _SSD_PAD + seq, :] = zero_rows
    for i in range(n_rblk):
        r0 = i * rblk
        pad_s[_SSD_PAD + r0:_SSD_PAD + r0 + rblk, :] = _dot(
            u_ref[r0:r0 + rblk, :], w_ref[:, _SSD_W_XBC:_SSD_W_DT])
    for i in range(n_rblk):
        r0 = i * rblk
        acc = cb_ref[...]
        for j in range(SSD_CONV):
            off = _SSD_PAD + r0 + j - SSD_CONV // 2
            acc = acc + pad_s[off:off + rblk, :] * cw_ref[j:j + 1, :]
        xbc = _silu(acc)
        xs_s[r0:r0 + rblk, :] = xbc[:, 0:SSD_GW]
        b_s[r0:r0 + rblk, :] = xbc[:, SSD_GW:SSD_GW + SSD_N].astype(BF16)
        c_s[r0:r0 + rblk, :] = xbc[:, SSD_GW + SSD_N:SSD_GW + 2 * SSD_N].astype(BF16)

    dt = _softplus(_dot(u_ref[...], w_ref[:, _SSD_W_DT:_SSD_W_END]) + dtb_ref[...])
    dt_s[...] = dt
    log_a = dt * (-jnp.exp(alog_ref[...]))
    lane = lax.broadcasted_iota(jnp.int32, (seq, LANES), 1)
    acum = jnp.where(lane < SSD_HG,
                     _chunk_scan(log_a, chunk, False, "add"),
                     _chunk_scan(log_a, chunk, True, "add"))
    a_s[...] = acum
    acum_t = acum.T
    for c in range(n_chunks):
        at_s[c] = acum_t[0:2 * SSD_HG, c * chunk:(c + 1) * chunk]
        r0 = c * chunk
        cbm_s[r0:r0 + chunk, :] = _dot_nt(c_s[r0:r0 + chunk, :], b_s[r0:r0 + chunk, :])

    lane_lo = lax.broadcasted_iota(jnp.int32, (chunk, LANES), 1) < SSD_HEAD_DIM
    lane_lo_row = lax.broadcasted_iota(jnp.int32, (1, LANES), 1) < SSD_HEAD_DIM

    for d in range(2):
        reverse = d == 1
        mask = _tri_mask(chunk, reverse)
        end_row = 0 if reverse else chunk - 1
        st_s[...] = jnp.zeros_like(st_s)

        def body(ci, carry, reverse=reverse, mask=mask, end_row=end_row, d=d):
            c = (n_chunks - 1 - ci) if reverse else ci
            r0 = pl.multiple_of(c * chunk, chunk)
            rows = pl.ds(r0, chunk)
            blk = a_s[rows, :]
            blk_t = at_s[c]
            dt_blk = dt_s[rows, :]
            cbm = cbm_s[rows, :]
            cc = c_s[rows, :]
            bc = b_s[rows, :]
            y_inter = _dot(cc, st_s[...].astype(BF16))
            end_blk = blk[end_row:end_row + 1, :]
            xw_tiles = []
            for p in range(_SSD_PAIRS):
                col0 = d * SSD_HG + 2 * p
                cols = slice(p * LANES, (p + 1) * LANES)
                a_pair = _pair_expand(blk, col0, lane_lo)
                xdt = xs_s[rows, cols] * _pair_expand(dt_blk, col0, lane_lo)
                y = y_inter[:, cols] * jnp.exp(a_pair)
                for hh in range(2):
                    col = col0 + hh
                    seg = jnp.exp(jnp.where(mask, blk[:, col:col + 1] - blk_t[col:col + 1, :], NEG))
                    keep = lane_lo if hh == 0 else jnp.logical_not(lane_lo)
                    x_h = jnp.where(keep, xdt, 0.0).astype(BF16)
                    y = y + _dot((cbm * seg).astype(BF16), x_h)
                if d == 0:
                    o_ref[rows, cols] = y
                else:
                    o_ref[rows, cols] = o_ref[rows, cols] + y
                end_pair = jnp.where(lane_lo_row, end_blk[:, col0:col0 + 1], end_blk[:, col0 + 1:col0 + 2])
                xw_tiles.append((xdt * jnp.exp(end_pair - a_pair)).astype(BF16))
                st_s[:, cols] = st_s[:, cols] * jnp.exp(end_pair)
            st_s[...] = st_s[...] + _dot_tn(bc, jnp.concatenate(xw_tiles, axis=1))
            return carry

        lax.fori_loop(0, n_chunks, body, 0)

    for i in range(n_rblk):
        r0 = i * rblk
        rows = slice(r0, r0 + rblk)
        z = _dot(u_ref[rows, :], w_ref[:, _SSD_W_Z:_SSD_W_XBC])
        y = (o_ref[rows, :] + xs_s[rows, :] * dsk_ref[...]) * _silu(z)
        o_ref[rows, :] = y * lax.rsqrt(jnp.mean(y * y, axis=-1, keepdims=True) + EPS) * ng_ref[...]


def _ssd_call(u, w, cw, cb, dtb, alog, dsk, ng, *, chunk=SSD_CHUNK):
    n_b, seq, d = u.shape
    rblk = min(seq, 512)
    kern = functools.partial(_ssd_kernel, seq=seq, chunk=chunk, rblk=rblk)
    vec = lambda n: pl.BlockSpec((None, 1, n), lambda b, g: (g, 0, 0))
    return pl.pallas_call(
        kern,
        out_shape=jax.ShapeDtypeStruct((n_b, seq, SSD_G * SSD_GW), F32),
        grid=(n_b, SSD_G),
        in_specs=[pl.BlockSpec((None, seq, d), lambda b, g: (b, 0, 0)),
                  pl.BlockSpec((None, d, _SSD_W_END), lambda b, g: (g, 0, 0)),
                  pl.BlockSpec((None, SSD_CONV, _SSD_XBC_G), lambda b, g: (g, 0, 0)),
                  vec(_SSD_XBC_G), vec(LANES), vec(LANES), vec(SSD_GW), vec(SSD_GW)],
        out_specs=pl.BlockSpec((None, seq, SSD_GW), lambda b, g: (b, 0, g)),
        scratch_shapes=[pltpu.VMEM((seq + 2 * _SSD_PAD, _SSD_XBC_G), F32),
                        pltpu.VMEM((seq, SSD_GW), F32),
                        pltpu.VMEM((seq, SSD_N), BF16),
                        pltpu.VMEM((seq, SSD_N), BF16),
                        pltpu.VMEM((seq, LANES), F32),
                        pltpu.VMEM((seq, LANES), F32),
                        pltpu.VMEM((seq // chunk, 2 * SSD_HG, chunk), F32),
                        pltpu.VMEM((seq, chunk), F32),
                        pltpu.VMEM((SSD_N, SSD_GW), F32)],
        compiler_params=pltpu.CompilerParams(dimension_semantics=("parallel", "arbitrary"),
                                             vmem_limit_bytes=VMEM_LIMIT),
        name="ssd_branch",
    )(u, w, cw, cb, dtb, alog, dsk, ng)


def _merge_kernel(u_ref, h_ref, y0_ref, y1_ref, y2_ref, wg_ref, wb_ref, wo_ref, o_ref, acc_s):
    br = pl.program_id(1)
    gate = _sigmoid(_dot(u_ref[...], wg_ref[...]))

    def branch(y_ref):
        return gate * _dot(y_ref[...].astype(BF16), wb_ref[...])

    @pl.when(br == 0)
    def _():
        acc_s[...] = branch(y0_ref)

    @pl.when(br == 1)
    def _():
        acc_s[...] = acc_s[...] + branch(y1_ref)

    @pl.when(br == 2)
    def _():
        merged = acc_s[...] + branch(y2_ref)
        o_ref[...] = h_ref[...] + _dot(merged.astype(BF16), wo_ref[...])


def _merge_call(u2d, h2d, y0, y1, y2, wg, wb, wo, tm=512):
    n_tok, d = h2d.shape
    tm = min(tm, n_tok)
    tok = lambda: pl.BlockSpec((tm, d), lambda i, r: (i, 0))
    return pl.pallas_call(
        _merge_kernel,
        out_shape=jax.ShapeDtypeStruct((n_tok, d), F32),
        grid=(n_tok // tm, 3),
        in_specs=[tok(), tok(), tok(), tok(), tok(),
                  pl.BlockSpec((None, d, d), lambda i, r: (r, 0, 0)),
                  pl.BlockSpec((None, d, d), lambda i, r: (r, 0, 0)),
                  pl.BlockSpec((d, d), lambda i, r: (0, 0))],
        out_specs=tok(),
        scratch_shapes=[pltpu.VMEM((tm, d), F32)],
        compiler_params=pltpu.CompilerParams(dimension_semantics=("parallel", "arbitrary"),
                                             vmem_limit_bytes=VMEM_LIMIT),
        name="merge",
    )(u2d, h2d, y0, y1, y2, wg, wb, wo)


_FFN_CW = 256
_FFN_NJ = D_FF // _FFN_CW
_FFN_HALO = SUBLANES

def _ffn_kernel(h_ref, hp_ref, hn_ref, p_ref, g_ref, wa_ref, wv_ref, cwa_ref, cwv_ref,
                cba_ref, cbv_ref, wd_ref, wple_ref, wpg_ref, fg_ref, o_ref, u_s,
                *, tm, n_tiles, final_norm):
    i = pl.program_id(1)
    j = pl.program_id(2)

    def norm(x):
        return (x * lax.rsqrt(jnp.mean(x * x, axis=-1, keepdims=True) + EPS) * g_ref[...]).astype(BF16)

    @pl.when(j == 0)
    def _():
        up = jnp.where(i > 0, 1.0, 0.0)
        dn = jnp.where(i < n_tiles - 1, 1.0, 0.0)
        u_s[0:_FFN_HALO, :] = norm(hp_ref[...] * up)
        u_s[_FFN_HALO:_FFN_HALO + tm, :] = norm(h_ref[...])
        u_s[_FFN_HALO + tm:2 * _FFN_HALO + tm, :] = norm(hn_ref[...] * dn)

    def conv(w_ref, cw_ref, cb_ref):
        up = _dot(u_s[...], w_ref[...])
        acc = cb_ref[...]
        for t in range(FFN_CONV):
            off = _FFN_HALO + t - FFN_CONV // 2
            acc = acc + up[off:off + tm, :] * cw_ref[t:t + 1, :]
        return acc

    act = (_silu(conv(wa_ref, cwa_ref, cba_ref)) * conv(wv_ref, cwv_ref, cbv_ref)).astype(BF16)
    part = _dot(act, wd_ref[...])

    @pl.when(j == 0)
    def _():
        o_ref[...] = h_ref[...] + part

    @pl.when(j > 0)
    def _():
        o_ref[...] = o_ref[...] + part

    @pl.when(j == _FFN_NJ - 1)
    def _():
        h2 = o_ref[...]
        ple = _dot(p_ref[...].astype(BF16), wple_ref[...])
        h3 = h2 + ple * _sigmoid(_dot(h2.astype(BF16), wpg_ref[...]))
        if final_norm:
            h3 = h3 * lax.rsqrt(jnp.mean(h3 * h3, axis=-1, keepdims=True) + EPS) * fg_ref[...]
        o_ref[...] = h3


def _ffn_call(h, p, g, w_up, cw, cb, w_down, w_ple, w_pg, fg, *, final_norm, tm=1024):
    n_b, seq, d = h.shape
    tm = min(tm, seq)
    n_tiles = seq // tm
    hb = tm // _FFN_HALO
    n_hb = seq // _FFN_HALO
    kern = functools.partial(_ffn_kernel, tm=tm, n_tiles=n_tiles, final_norm=final_norm)
    cst = lambda shape: pl.BlockSpec(shape, lambda b, i, j: tuple(0 for _ in shape))
    return pl.pallas_call(
        kern,
        out_shape=jax.ShapeDtypeStruct((n_b, seq, d), F32),
        grid=(n_b, n_tiles, _FFN_NJ),
        in_specs=[pl.BlockSpec((None, tm, d), lambda b, i, j: (b, i, 0)),
                  pl.BlockSpec((None, _FFN_HALO, d), lambda b, i, j: (b, jnp.maximum(i * hb - 1, 0), 0)),
                  pl.BlockSpec((None, _FFN_HALO, d), lambda b, i, j: (b, jnp.minimum((i + 1) * hb, n_hb - 1), 0)),
                  pl.BlockSpec((None, tm, PLE_DIM), lambda b, i, j: (b, i, 0)),
                  cst((1, d)),
                  pl.BlockSpec((d, _FFN_CW), lambda b, i, j: (0, j)),
                  pl.BlockSpec((d, _FFN_CW), lambda b, i, j: (0, _FFN_NJ + j)),
                  pl.BlockSpec((FFN_CONV, _FFN_CW), lambda b, i, j: (0, j)),
                  pl.BlockSpec((FFN_CONV, _FFN_CW), lambda b, i, j: (0, _FFN_NJ + j)),
                  pl.BlockSpec((1, _FFN_CW), lambda b, i, j: (0, j)),
                  pl.BlockSpec((1, _FFN_CW), lambda b, i, j: (0, _FFN_NJ + j)),
                  pl.BlockSpec((_FFN_CW, d), lambda b, i, j: (j, 0)),
                  cst((PLE_DIM, d)), cst((d, d)), cst((1, d))],
        out_specs=pl.BlockSpec((None, tm, d), lambda b, i, j: (b, i, 0)),
        scratch_shapes=[pltpu.VMEM((tm + 2 * _FFN_HALO, d), BF16)],
        compiler_params=pltpu.CompilerParams(
            dimension_semantics=("parallel", "parallel", "arbitrary"),
            vmem_limit_bytes=VMEM_LIMIT),
        name="convffn_ple",
    )(h, h, h, p, g.reshape(1, d), w_up, w_up, cw, cw, cb.reshape(1, -1), cb.reshape(1, -1),
      w_down, w_ple, w_pg, fg.reshape(1, d))


def _cols(w, start, width):
    return lax.slice_in_dim(w, start, start + width, axis=1)


def _pad_cols(w, width):
    return jnp.pad(w, ((0, 0), (0, width - w.shape[1])))


def _ssd_params(w_in, conv_w, conv_b, dt_bias, a_log, d_skip, norm_g):
    ws, cws, cbs, dtbs, alogs = [], [], [], [], []
    for g in range(SSD_G):
        dt_cols = jnp.concatenate(
            [_cols(w_in, _OFF_DT + dd * SSD_H + g * SSD_HG, SSD_HG) for dd in range(2)], axis=1)
        ws.append(jnp.concatenate([
            _cols(w_in, _OFF_Z + g * SSD_GW, SSD_GW),
            _cols(w_in, _OFF_XBC + g * SSD_GW, SSD_GW),
            _cols(w_in, _OFF_XBC + D_MODEL + g * SSD_N, SSD_N),
            _cols(w_in, _OFF_XBC + D_MODEL + SSD_G * SSD_N + g * SSD_N, SSD_N),
            _pad_cols(dt_cols, LANES)], axis=1))
        pick = lambda a: jnp.concatenate([
            _cols(a, g * SSD_GW, SSD_GW),
            _cols(a, D_MODEL + g * SSD_N, SSD_N),
            _cols(a, D_MODEL + SSD_G * SSD_N + g * SSD_N, SSD_N)], axis=1)
        cws.append(pick(conv_w))
        cbs.append(pick(conv_b.reshape(1, -1)))
        head_row = lambda a: _pad_cols(
            jnp.concatenate([a[dd, g * SSD_HG:(g + 1) * SSD_HG] for dd in range(2)]).reshape(1, -1), LANES)
        dtbs.append(head_row(dt_bias))
        alogs.append(head_row(a_log))
    dsk = jnp.repeat(d_skip, SSD_HEAD_DIM).reshape(SSD_G, 1, SSD_GW)
    return (jnp.stack(ws).astype(BF16), jnp.stack(cws), jnp.stack(cbs), jnp.stack(dtbs),
            jnp.stack(alogs), dsk, norm_g.reshape(SSD_G, 1, SSD_GW))


def _hgrn2_params(w_in, lb, norm_g):
    ws = [jnp.concatenate([
        _cols(w_in, _OFF_GQ + h * HEAD, HEAD),
        _cols(w_in, _OFF_GF + h * HEAD, HEAD),
        _cols(w_in, _OFF_GF + D_MODEL + h * HEAD, HEAD),
        _cols(w_in, _OFF_GI + h * HEAD, HEAD),
        _cols(w_in, _OFF_GG + h * HEAD, HEAD)], axis=1) for h in range(HG_H)]
    lbs = lb.reshape(2, HG_H, HEAD).swapaxes(0, 1)
    return jnp.stack(ws).astype(BF16), lbs, norm_g.reshape(HG_H, 1, HEAD)


def _mlstm_params(w_in, i_bias, f_bias, norm_g):
    ws, biases = [], []
    for h in range(ML_H):
        both_dirs = lambda off: jnp.concatenate([_cols(w_in, off + h, 1), _cols(w_in, off + ML_H + h, 1)], axis=1)
        ws.append(jnp.concatenate([
            _cols(w_in, _OFF_MQ + h * HEAD, HEAD),
            _cols(w_in, _OFF_MK + h * HEAD, HEAD),
            _cols(w_in, _OFF_MV + h * HEAD, HEAD),
            _cols(w_in, _OFF_MO + h * HEAD, HEAD),
            _pad_cols(both_dirs(_OFF_MI), LANES),
            _pad_cols(both_dirs(_OFF_MF), LANES)], axis=1))
        biases.append(_pad_cols(jnp.stack([i_bias[:, h], f_bias[:, h]]), LANES))
    return jnp.stack(ws).astype(BF16), jnp.stack(biases), norm_g.reshape(ML_H, 1, HEAD)


def kernel(x, p, norm_mix_g, w_in, ssd_conv_w, ssd_conv_b, ssd_dt_bias, ssd_a_log, ssd_d, ssd_norm_g, hg_lb_raw, hg_norm_g, ml_i_bias, ml_f_bias, ml_norm_g, w_br_ssd, w_br_hg, w_br_ml, w_out, norm_ffn_g, w_up, ffn_conv_w, ffn_conv_b, w_down, w_ple, w_ple_gate, final_norm_g):
    n_b, seq, d = x.shape
    depth = w_in.shape[0]
    lb_soft = jax.nn.softmax(hg_lb_raw.astype(F32), axis=0)
    hg_lb = jnp.cumsum(lb_soft, axis=0) - lb_soft[0:1]
    h = x
    for l in range(depth):
        u2d = _rmsnorm_call(h.reshape(n_b * seq, d), norm_mix_g[l], BF16)
        u = u2d.reshape(n_b, seq, d)
        y_ssd = _ssd_call(u, *_ssd_params(w_in[l], ssd_conv_w[l], ssd_conv_b[l], ssd_dt_bias[l],
                                          ssd_a_log[l], ssd_d[l], ssd_norm_g[l]))
        y_hg = _hgrn2_call(u, *_hgrn2_params(w_in[l], hg_lb[l], hg_norm_g[l]))
        y_ml = _mlstm_call(u, *_mlstm_params(w_in[l], ml_i_bias[l], ml_f_bias[l], ml_norm_g[l]))
        wg = jnp.stack([_cols(w_in[l], _OFF_GATES + r * d, d) for r in range(3)]).astype(BF16)
        wb = jnp.stack([w_br_ssd[l], w_br_hg[l], w_br_ml[l]]).astype(BF16)
        tok = lambda a: a.reshape(n_b * seq, d)
        h = _merge_call(u2d, tok(h), tok(y_ssd), tok(y_hg), tok(y_ml), wg, wb,
                        w_out[l].astype(BF16)).reshape(n_b, seq, d)
        h = _ffn_call(h, p[l], norm_ffn_g[l], w_up[l].astype(BF16), ffn_conv_w[l], ffn_conv_b[l],
                      w_down[l].astype(BF16), w_ple[l].astype(BF16), w_ple_gate[l].astype(BF16),
                      final_norm_g, final_norm=(l == depth - 1))
    return h
```

```python
import functools

import jax
import jax.numpy as jnp
from jax import lax
from jax.experimental import pallas as pl
from jax.experimental.pallas import tpu as pltpu

F32 = jnp.float32
BF16 = jnp.bfloat16

D_MODEL = 1024
PLE_DIM = 256
EPS = 1e-6
NEG = -1e30
TINY = 1e-30

SSD_HEAD_DIM = 64
SSD_H = 16
SSD_G = 2
SSD_HG = 8
SSD_N = 128
SSD_CONV = 5
SSD_GW = SSD_HG * SSD_HEAD_DIM
SSD_XBC = D_MODEL + 2 * SSD_G * SSD_N
HG_H = 8
ML_H = 8
HEAD = 128
D_FF = 2816
FFN_CONV = 3

LANES = 128
SUBLANES = 8
VMEM_LIMIT = 56 * 1024 * 1024

SSD_CHUNK = 128
HG_CHUNK = 64
ML_CHUNK = 128
SSD_UNROLL = 2
ML_UNROLL = 8
HG_UNROLL = 16
HG_SAFE_DECAY = 60.0

_OFF_Z = 0
_OFF_XBC = 1024
_OFF_DT = _OFF_XBC + SSD_XBC
_OFF_GQ = _OFF_DT + 2 * SSD_H
_OFF_GF = _OFF_GQ + 1024
_OFF_GI = _OFF_GF + 2048
_OFF_GG = _OFF_GI + 1024
_OFF_MQ = _OFF_GG + 1024
_OFF_MK = _OFF_MQ + 1024
_OFF_MV = _OFF_MK + 1024
_OFF_MI = _OFF_MV + 1024
_OFF_MF = _OFF_MI + 2 * ML_H
_OFF_MO = _OFF_MF + 2 * ML_H
_OFF_GATES = _OFF_MO + 1024


def _dot(a, b):
    return jnp.dot(a, b, preferred_element_type=F32)


def _dot_nt(a, b):
    return lax.dot_general(a, b, (((1,), (1,)), ((), ())), preferred_element_type=F32)


def _dot_tn(a, b):
    return lax.dot_general(a, b, (((0,), (0,)), ((), ())), preferred_element_type=F32)


def _sigmoid(x):
    return 1.0 / (1.0 + jnp.exp(-x))


def _silu(x):
    return x * _sigmoid(x)


def _softplus(x):
    return jnp.maximum(x, 0.0) + jnp.log1p(jnp.exp(-jnp.abs(x)))


def _chunk_scan(x, chunk, reverse, op):
    n_rows = x.shape[0]
    row = lax.broadcasted_iota(jnp.int32, x.shape, 0) & (chunk - 1)
    sh = 1
    while sh < chunk:
        if reverse:
            shifted = pltpu.roll(x, n_rows - sh, axis=0)
            ok = row < chunk - sh
        else:
            shifted = pltpu.roll(x, sh, axis=0)
            ok = row >= sh
        if op == "add":
            x = x + jnp.where(ok, shifted, 0.0)
        else:
            x = jnp.maximum(x, jnp.where(ok, shifted, NEG))
        sh *= 2
    return x


def _lane_scan(x, seg, reverse, op, start=1):
    n_lanes = x.shape[1]
    pos = lax.broadcasted_iota(jnp.int32, x.shape, 1) & (seg - 1)
    sh = start
    while sh < seg:
        if reverse:
            shifted = pltpu.roll(x, n_lanes - sh, axis=1)
            ok = pos < seg - sh
        else:
            shifted = pltpu.roll(x, sh, axis=1)
            ok = pos >= sh
        if op == "add":
            x = x + jnp.where(ok, shifted, 0.0)
        else:
            x = jnp.maximum(x, jnp.where(ok, shifted, NEG))
        sh *= 2
    return x


def _tri_mask(n, reverse):
    t = lax.broadcasted_iota(jnp.int32, (n, n), 0)
    s = lax.broadcasted_iota(jnp.int32, (n, n), 1)
    return (s >= t) if reverse else (s <= t)


def _rmsnorm_kernel(h_ref, g_ref, o_ref):
    x = h_ref[...]
    ms = jnp.mean(x * x, axis=-1, keepdims=True)
    o_ref[...] = (x * lax.rsqrt(ms + EPS) * g_ref[...]).astype(o_ref.dtype)


def _rmsnorm_call(h2d, g, out_dtype, tm=1024):
    n_tok, d = h2d.shape
    tm = min(tm, n_tok)
    return pl.pallas_call(
        _rmsnorm_kernel,
        out_shape=jax.ShapeDtypeStruct((n_tok, d), out_dtype),
        grid=(n_tok // tm,),
        in_specs=[pl.BlockSpec((tm, d), lambda i: (i, 0)),
                  pl.BlockSpec((1, d), lambda i: (0, 0))],
        out_specs=pl.BlockSpec((tm, d), lambda i: (i, 0)),
        compiler_params=pltpu.CompilerParams(dimension_semantics=("parallel",),
                                             vmem_limit_bytes=VMEM_LIMIT),
        name="rmsnorm",
    )(h2d, g.reshape(1, d))


def _hg_att_safe(qc, kc, lfc, reverse):
    n = qc.shape[0]
    t = lax.broadcasted_iota(jnp.int32, (n, n), 0)
    s = lax.broadcasted_iota(jnp.int32, (n, n), 1)
    att = jnp.where(t == s, jnp.sum(qc * kc, axis=-1, keepdims=True), 0.0)
    half = 1
    while half < n:
        pre = _chunk_scan(lfc, half, False, "add")
        suf = _chunk_scan(lfc, half, True, "add")
        t_hi = (t & half) != 0
        s_hi = (s & half) != 0
        same_block = (t ^ s) < 2 * half
        if reverse:
            qe = qc * jnp.exp(suf)
            ke = kc * jnp.exp(pre - lfc)
            pair = jnp.logical_and(same_block, jnp.logical_and(jnp.logical_not(t_hi), s_hi))
        else:
            qe = qc * jnp.exp(pre)
            ke = kc * jnp.exp(suf - lfc)
            pair = jnp.logical_and(same_block, jnp.logical_and(t_hi, jnp.logical_not(s_hi)))
        att = att + jnp.where(pair, _dot_nt(qe.astype(BF16), ke.astype(BF16)), 0.0)
        half *= 2
    return att


def _hgrn2_kernel(u_ref, w_ref, lb_ref, ng_ref, o_ref,
                  q_s, v_s, k_s, lf_s, b_s, qe_s, ke_s, qi_s, kd_s, dec_s, upd_s, st_s, acc_s,
                  *, seq, chunk):
    n_chunks = seq // chunk
    proj = _dot(u_ref[...], w_ref[...])
    q_s[...] = _silu(proj[:, 0:HEAD])
    v_s[...] = proj[:, 3 * HEAD:4 * HEAD].astype(BF16)
    o_ref[...] = proj[:, 4 * HEAD:5 * HEAD]

    for d in range(2):
        f_raw = proj[:, (1 + d) * HEAD:(2 + d) * HEAD]
        lb = lb_ref[d:d + 1, :]
        f = lb + (1.0 - lb) * _sigmoid(f_raw)
        k_s[d] = (1.0 - lb) * _sigmoid(-f_raw)
        log_f = jnp.log(jnp.maximum(f, TINY))
        lf_s[d] = log_f
        b = _chunk_scan(log_f, chunk, d == 1, "add")
        b_s[d] = b

    def ref_span(d):
        first, mid, last = (b_s[d, pl.ds(r, n_chunks, stride=chunk), :] for r in (0, chunk // 2, chunk - 1))
        return jnp.maximum(jnp.abs(first - mid), jnp.abs(mid - last))

    mild_decay = jnp.max(jnp.maximum(ref_span(0), ref_span(1))) <= HG_SAFE_DECAY

    for d in range(2):
        reverse = d == 1
        mask = _tri_mask(chunk, reverse)
        end_row = 0 if reverse else chunk - 1

        def chunk_rows(c):
            r0 = pl.multiple_of(c * chunk, chunk)
            return r0, pl.ds(r0, chunk)

        def scale_body(c, carry, d=d, end_row=end_row):
            r0, rows = chunk_rows(c)
            qc = q_s[rows, :]
            kc = k_s[d, rows, :]
            bc = b_s[d, rows, :]
            ref = b_s[d, pl.ds(r0 + chunk // 2, 1), :]
            bend = b_s[d, pl.ds(r0 + end_row, 1), :]
            qe_s[rows, :] = (qc * jnp.exp(bc - ref)).astype(BF16)
            ke_s[rows, :] = (kc * jnp.exp(ref - bc)).astype(BF16)
            qi_s[rows, :] = (qc * jnp.exp(bc)).astype(BF16)
            kd_s[rows, :] = (kc * jnp.exp(bend - bc)).astype(BF16)
            dec_s[c] = jnp.broadcast_to(jnp.exp(bend), (HEAD, HEAD)).T
            return carry

        lax.fori_loop(0, n_chunks, scale_body, 0, unroll=HG_UNROLL)

        def upd_body(c, carry):
            _, rows = chunk_rows(c)
            upd_s[c] = _dot_tn(kd_s[rows, :], v_s[rows, :])
            return carry

        lax.fori_loop(0, n_chunks, upd_body, 0, unroll=HG_UNROLL)

        def rec_body(ci, st, reverse=reverse):
            c = (n_chunks - 1 - ci) if reverse else ci
            st_s[c] = st.astype(BF16)
            return st * dec_s[c] + upd_s[c]

        lax.fori_loop(0, n_chunks, rec_body, jnp.zeros((HEAD, HEAD), F32))

        def out_body(c, carry, fast, d=d, mask=mask, reverse=reverse):
            _, rows = chunk_rows(c)
            if fast:
                att = jnp.where(mask, _dot_nt(qe_s[rows, :], ke_s[rows, :]), 0.0)
            else:
                att = _hg_att_safe(q_s[rows, :], k_s[d, rows, :], lf_s[d, rows, :], reverse)
            o = _dot(jnp.concatenate([qi_s[rows, :], att.astype(BF16)], axis=1),
                     jnp.concatenate([st_s[c], v_s[rows, :]], axis=0))
            if d == 0:
                acc_s[rows, :] = o
            else:
                acc_s[rows, :] = acc_s[rows, :] + o
            return carry

        @pl.when(mild_decay)
        def _():
            lax.fori_loop(0, n_chunks, functools.partial(out_body, fast=True), 0, unroll=HG_UNROLL)

        @pl.when(jnp.logical_not(mild_decay))
        def _():
            lax.fori_loop(0, n_chunks, functools.partial(out_body, fast=False), 0)

    o = acc_s[...]
    o = o * lax.rsqrt(jnp.mean(o * o, axis=-1, keepdims=True) + EPS) * ng_ref[...]
    o_ref[...] = o * _silu(o_ref[...])


def _hgrn2_call(u, w, lb, ng, *, chunk=HG_CHUNK):
    n_b, seq, d = u.shape
    kern = functools.partial(_hgrn2_kernel, seq=seq, chunk=chunk)
    return pl.pallas_call(
        kern,
        out_shape=jax.ShapeDtypeStruct((n_b, seq, HG_H * HEAD), F32),
        grid=(n_b, HG_H),
        in_specs=[pl.BlockSpec((None, seq, d), lambda b, h: (b, 0, 0)),
                  pl.BlockSpec((None, d, 5 * HEAD), lambda b, h: (h, 0, 0)),
                  pl.BlockSpec((None, 2, HEAD), lambda b, h: (h, 0, 0)),
                  pl.BlockSpec((None, 1, HEAD), lambda b, h: (h, 0, 0))],
        out_specs=pl.BlockSpec((None, seq, HEAD), lambda b, h: (b, 0, h)),
        scratch_shapes=[pltpu.VMEM((seq, HEAD), F32),
                        pltpu.VMEM((seq, HEAD), BF16),
                        pltpu.VMEM((2, seq, HEAD), F32),
                        pltpu.VMEM((2, seq, HEAD), F32),
                        pltpu.VMEM((2, seq, HEAD), F32),
                        pltpu.VMEM((seq, HEAD), BF16),
                        pltpu.VMEM((seq, HEAD), BF16),
                        pltpu.VMEM((seq, HEAD), BF16),
                        pltpu.VMEM((seq, HEAD), BF16),
                        pltpu.VMEM((seq // chunk, HEAD, HEAD), F32),
                        pltpu.VMEM((seq // chunk, HEAD, HEAD), F32),
                        pltpu.VMEM((seq // chunk, HEAD, HEAD), BF16),
                        pltpu.VMEM((seq, HEAD), F32)],
        compiler_params=pltpu.CompilerParams(dimension_semantics=("parallel", "arbitrary"),
                                             vmem_limit_bytes=VMEM_LIMIT),
        name="hgrn2_branch",
    )(u, w, lb, ng)


def _log_sigmoid(x):
    return jnp.minimum(x, 0.0) - jnp.log1p(jnp.exp(-jnp.abs(x)))


def _mlstm_kernel(u_ref, w_ref, bias_ref, ng_ref, o_ref,
                  qb_s, kb_s, va_s, tm_s, qw_s, kw_s, mx_s, en_s, gt_s, dec_s, upd_s, ct_s,
                  acc_s, *, seq, chunk):
    n_chunks = seq // chunk
    proj = _dot(u_ref[...], w_ref[...])
    q = proj[:, 0:HEAD]
    k = proj[:, HEAD:2 * HEAD] * (HEAD ** -0.5)
    qb_s[...] = q.astype(BF16)
    kb_s[...] = k.astype(BF16)
    va_s[:, 0:HEAD] = proj[:, 2 * HEAD:3 * HEAD].astype(BF16)
    va_s[:, HEAD:2 * HEAD] = jnp.ones((seq, HEAD), BF16)
    o_ref[...] = proj[:, 3 * HEAD:4 * HEAD]

    x = (proj[:, 4 * HEAD:5 * HEAD] + bias_ref[...]).T[0:SUBLANES, :]
    fwd_row = lax.broadcasted_iota(jnp.int32, (SUBLANES, seq), 0) == 0
    pos = lax.broadcasted_iota(jnp.int32, (SUBLANES, seq), 1)

    def scans(y, op, seg, start=1):
        return _lane_scan(y, seg, False, op, start), _lane_scan(y, seg, True, op, start)

    it = x
    log_f = pltpu.roll(_log_sigmoid(x), SUBLANES - 2, axis=0)
    pre, suf = scans(log_f, "add", chunk)
    bt = jnp.where(fwd_row, pre, suf)
    bt_end = pre + suf - log_f
    g = it - bt
    g_pre, g_suf = scans(g, "max", chunk)
    cm = jnp.where(fwd_row, g_pre, g_suf)
    cm_end = jnp.maximum(g_pre, g_suf)

    p_pre, p_suf = scans(bt_end, "add", seq, chunk)
    p_sum = jnp.where(fwd_row, p_pre, p_suf)
    x_pre, x_suf = scans(cm_end - (p_sum - bt_end), "max", seq, chunk)
    m_end = p_sum + jnp.where(fwd_row, x_pre, x_suf)
    m_prev = jnp.where(fwd_row,
                       jnp.where(pos < chunk, NEG, pltpu.roll(m_end, chunk, axis=1)),
                       jnp.where(pos >= seq - chunk, NEG, pltpu.roll(m_end, seq - chunk, axis=1)))
    mx = jnp.maximum(m_prev, cm)
    w_inter = jnp.exp(m_prev - mx)
    wk = jnp.exp(bt_end - bt + it - m_end)
    en = jnp.exp(-(bt + mx))
    decay = jnp.exp(bt_end + m_prev - m_end)

    for c in range(n_chunks):
        gt_s[c] = g[:, c * chunk:(c + 1) * chunk]
    row = lax.broadcasted_iota(jnp.int32, (SUBLANES, seq), 0)
    packed = jnp.where(row < 2, w_inter,
                       jnp.where(row < 4, pltpu.roll(wk, 2, axis=0),
                                 jnp.where(row < 6, pltpu.roll(mx, 4, axis=0), pltpu.roll(en, 6, axis=0))))
    tm_s[...] = jnp.concatenate(
        [packed, decay, jnp.zeros((LANES - 2 * SUBLANES, seq), F32)], axis=0).T
    for d in range(2):
        def lanes(col, ref_rows=slice(None)):
            block = tm_s[ref_rows, :]
            return jnp.broadcast_to(block[:, col:col + 1], (block.shape[0], LANES))
        qw_s[d] = (q * lanes(d)).astype(BF16)
        kw_s[d] = (k * lanes(2 + d)).astype(BF16)
        mx_s[d] = lanes(4 + d)
        en_s[d] = lanes(6 + d)
        dec_s[d] = lanes(8 + d, pl.ds(0, n_chunks, stride=chunk))

    for d in range(2):
        reverse = d == 1
        mask = _tri_mask(chunk, reverse)

        def chunk_rows(c):
            return pl.ds(pl.multiple_of(c * chunk, chunk), chunk)

        def upd_body(c, carry, d=d):
            rows = chunk_rows(c)
            upd_s[c] = _dot_tn(kw_s[d, rows, :], va_s[rows, :])
            return carry

        lax.fori_loop(0, n_chunks, upd_body, 0, unroll=ML_UNROLL)

        def rec_body(ci, ct, reverse=reverse, d=d):
            c = (n_chunks - 1 - ci) if reverse else ci
            ct_s[c] = ct.astype(BF16)
            dec = dec_s[d, pl.ds(c, 1), :]
            return ct * jnp.concatenate([dec, dec], axis=1) + upd_s[c]

        lax.fori_loop(0, n_chunks, rec_body, jnp.zeros((HEAD, 2 * HEAD), F32))

        def out_body(c, carry, mask=mask, d=d):
            rows = chunk_rows(c)
            p = jnp.exp(jnp.where(mask, gt_s[c][d:d + 1, :] - mx_s[d, rows, :], NEG))
            sc = (_dot_nt(qb_s[rows, :], kb_s[rows, :]) * p).astype(BF16)
            tot = _dot(jnp.concatenate([qw_s[d, rows, :], sc], axis=1),
                       jnp.concatenate([ct_s[c], va_s[rows, :]], axis=0))
            hval = tot[:, 0:HEAD] / jnp.maximum(jnp.abs(tot[:, HEAD:2 * HEAD]), en_s[d, rows, :])
            if d == 0:
                acc_s[rows, :] = hval
            else:
                acc_s[rows, :] = acc_s[rows, :] + hval
            return carry

        lax.fori_loop(0, n_chunks, out_body, 0, unroll=ML_UNROLL)

    hsum = acc_s[...]
    hc = hsum - jnp.mean(hsum, axis=-1, keepdims=True)
    hn = hc * lax.rsqrt(jnp.mean(hc * hc, axis=-1, keepdims=True) + EPS) * ng_ref[...]
    o_ref[...] = hn * _sigmoid(o_ref[...])


def _mlstm_call(u, w, bias, ng, *, chunk=ML_CHUNK):
    n_b, seq, d = u.shape
    n_chunks = seq // chunk
    kern = functools.partial(_mlstm_kernel, seq=seq, chunk=chunk)
    return pl.pallas_call(
        kern,
        out_shape=jax.ShapeDtypeStruct((n_b, seq, ML_H * HEAD), F32),
        grid=(n_b, ML_H),
        in_specs=[pl.BlockSpec((None, seq, d), lambda b, h: (b, 0, 0)),
                  pl.BlockSpec((None, d, 5 * HEAD), lambda b, h: (h, 0, 0)),
                  pl.BlockSpec((None, 1, HEAD), lambda b, h: (h, 0, 0)),
                  pl.BlockSpec((None, 1, HEAD), lambda b, h: (h, 0, 0))],
        out_specs=pl.BlockSpec((None, seq, HEAD), lambda b, h: (b, 0, h)),
        scratch_shapes=[pltpu.VMEM((seq, HEAD), BF16),
                        pltpu.VMEM((seq, HEAD), BF16),
                        pltpu.VMEM((seq, 2 * HEAD), BF16),
                        pltpu.VMEM((seq, LANES), F32),
                        pltpu.VMEM((2, seq, HEAD), BF16),
                        pltpu.VMEM((2, seq, HEAD), BF16),
                        pltpu.VMEM((2, seq, LANES), F32),
                        pltpu.VMEM((2, seq, LANES), F32),
                        pltpu.VMEM((n_chunks, SUBLANES, chunk), F32),
                        pltpu.VMEM((2, n_chunks, LANES), F32),
                        pltpu.VMEM((n_chunks, HEAD, 2 * HEAD), F32),
                        pltpu.VMEM((n_chunks, HEAD, 2 * HEAD), BF16),
                        pltpu.VMEM((seq, HEAD), F32)],
        compiler_params=pltpu.CompilerParams(dimension_semantics=("parallel", "arbitrary"),
                                             vmem_limit_bytes=VMEM_LIMIT),
        name="mlstm_branch",
    )(u, w, bias, ng)


_SSD_W_Z = 0
_SSD_W_XBC = SSD_GW
_SSD_W_DT = SSD_GW + SSD_GW + 2 * SSD_N
_SSD_W_END = _SSD_W_DT + LANES
_SSD_XBC_G = SSD_GW + 2 * SSD_N
_SSD_PAD = SUBLANES
_SSD_PAIRS = SSD_HG // 2


def _ssd_kernel(u_ref, w_ref, cw_ref, cb_ref, dtb_ref, alog_ref, dsk_ref, ng_ref, o_ref,
                pad_s, xs_s, b_s, c_s, dt_s, a_s, at_s, dtt_s, cbm_s, ea_s, xw_s, dec_s, cur_s, st_s,
                *, seq, chunk, rblk):
    n_chunks = seq // chunk
    n_rblk = seq // rblk

    zero_rows = jnp.zeros((_SSD_PAD, _SSD_XBC_G), F32)
    pad_s[0:_SSD_PAD, :] = zero_rows
    pad_s[_SSD_PAD + seq:2 * _SSD_PAD + seq, :] = zero_rows
    for i in range(n_rblk):
        r0 = i * rblk
        pad_s[_SSD_PAD + r0:_SSD_PAD + r0 + rblk, :] = _dot(
            u_ref[r0:r0 + rblk, :], w_ref[:, _SSD_W_XBC:_SSD_W_DT])
    for i in range(n_rblk):
        r0 = i * rblk
        acc = cb_ref[...]
        for j in range(SSD_CONV):
            off = _SSD_PAD + r0 + j - SSD_CONV // 2
            acc = acc + pad_s[off:off + rblk, :] * cw_ref[j:j + 1, :]
        xbc = _silu(acc)
        xs_s[r0:r0 + rblk, :] = xbc[:, 0:SSD_GW]
        b_s[r0:r0 + rblk, :] = xbc[:, SSD_GW:SSD_GW + SSD_N].astype(BF16)
        c_s[r0:r0 + rblk, :] = xbc[:, SSD_GW + SSD_N:SSD_GW + 2 * SSD_N].astype(BF16)

    dt = _softplus(_dot(u_ref[...], w_ref[:, _SSD_W_DT:_SSD_W_END]) + dtb_ref[...])
    dt_s[...] = dt
    log_a = dt * (-jnp.exp(alog_ref[...]))
    lane = lax.broadcasted_iota(jnp.int32, (seq, LANES), 1)
    acum = jnp.where(lane < SSD_HG,
                     _chunk_scan(log_a, chunk, False, "add"),
                     _chunk_scan(log_a, chunk, True, "add"))
    a_s[...] = acum
    acum_t = acum.T
    dt_t = dt.T
    for c in range(n_chunks):
        at_s[c] = acum_t[0:2 * SSD_HG, c * chunk:(c + 1) * chunk]
        dtt_s[c] = dt_t[0:2 * SSD_HG, c * chunk:(c + 1) * chunk]
        r0 = c * chunk
        cbm_s[r0:r0 + chunk, :] = _dot_nt(c_s[r0:r0 + chunk, :], b_s[r0:r0 + chunk, :])

    fwd_lane_c = lax.broadcasted_iota(jnp.int32, (n_chunks, LANES), 1) < SSD_HG
    a_end = jnp.where(fwd_lane_c, a_s[pl.ds(chunk - 1, n_chunks, stride=chunk), :],
                      a_s[pl.ds(0, n_chunks, stride=chunk), :])
    lane_lo = lax.broadcasted_iota(jnp.int32, (chunk, LANES), 1) < SSD_HEAD_DIM

    def split3(x):
        hi = x.astype(BF16)
        r1 = x - hi.astype(F32)
        mid = r1.astype(BF16)
        return [hi, mid, (r1 - mid.astype(F32)).astype(BF16)]

    for d in range(2):
        reverse = d == 1
        mask = _tri_mask(chunk, reverse)
        spread = (lax.broadcasted_iota(jnp.int32, (LANES, SSD_GW), 1) // SSD_HEAD_DIM
                  == lax.broadcasted_iota(jnp.int32, (LANES, SSD_GW), 0) - d * SSD_HG).astype(BF16)
        spread2 = jnp.concatenate([spread, spread], axis=0)
        dec_s[...] = _dot(jnp.concatenate(split3(jnp.exp(a_end)), axis=1),
                          jnp.concatenate([spread2, spread], axis=0))
        for i in range(n_rblk):
            r0 = i * rblk
            rows = slice(r0, r0 + rblk)
            a_blk = a_s[rows, :]
            end_tok = jnp.concatenate(
                [jnp.broadcast_to(a_end[c:c + 1, :], (chunk, LANES))
                 for c in range(r0 // chunk, (r0 + rblk) // chunk)], axis=0)
            ea_s[rows, :] = _dot(jnp.exp(a_blk).astype(BF16), spread).astype(BF16)
            wt = split3(dt_s[rows, :] * jnp.exp(end_tok - a_blk))
            xw_s[rows, :] = (xs_s[rows, :] * _dot(jnp.concatenate(wt[0:2], axis=1), spread2)).astype(BF16)

        def chunk_rows(c):
            return pl.ds(pl.multiple_of(c * chunk, chunk), chunk)

        def upd_body(c, carry):
            rows = chunk_rows(c)
            pad_s[rows, 0:SSD_GW] = _dot_tn(b_s[rows, :], xw_s[rows, :])
            return carry

        lax.fori_loop(0, n_chunks, upd_body, 0, unroll=SSD_UNROLL)

        cur_s[...] = jnp.zeros_like(cur_s)

        def rec_body(ci, carry, reverse=reverse):
            c = (n_chunks - 1 - ci) if reverse else ci
            cur = cur_s[...]
            st_s[c] = cur.astype(BF16)
            cur_s[...] = cur * dec_s[pl.ds(c, 1), :] + pad_s[chunk_rows(c), 0:SSD_GW]
            return carry

        lax.fori_loop(0, n_chunks, rec_body, 0)

        def out_body(c, carry, mask=mask, d=d):
            rows = chunk_rows(c)
            blk = a_s[rows, :]
            blk_t = at_s[c]
            dtt = dtt_s[c]
            cbm = cbm_s[rows, :]
            y_inter = _dot(c_s[rows, :], st_s[c])
            for p in range(_SSD_PAIRS):
                cols = slice(p * LANES, (p + 1) * LANES)
                xs_pair = xs_s[rows, cols]
                m_tiles, x_tiles = [], []
                for hh in range(2):
                    col = d * SSD_HG + 2 * p + hh
                    seg = jnp.exp(jnp.where(mask, blk[:, col:col + 1] - blk_t[col:col + 1, :], NEG))
                    m_tiles.append((cbm * seg * dtt[col:col + 1, :]).astype(BF16))
                    keep = lane_lo if hh == 0 else jnp.logical_not(lane_lo)
                    x_tiles.append(jnp.where(keep, xs_pair, 0.0).astype(BF16))
                y = (y_inter[:, cols] * ea_s[rows, cols].astype(F32)
                     + _dot(jnp.concatenate(m_tiles, axis=1), jnp.concatenate(x_tiles, axis=0)))
                if d == 0:
                    o_ref[rows, cols] = y
                else:
                    o_ref[rows, cols] = o_ref[rows, cols] + y
            return carry

        lax.fori_loop(0, n_chunks, out_body, 0, unroll=SSD_UNROLL)

    for i in range(n_rblk):
        r0 = i * rblk
        rows = slice(r0, r0 + rblk)
        z = _dot(u_ref[rows, :], w_ref[:, _SSD_W_Z:_SSD_W_XBC])
        y = (o_ref[rows, :] + xs_s[rows, :] * dsk_ref[...]) * _silu(z)
        o_ref[rows, :] = y * lax.rsqrt(jnp.mean(y * y, axis=-1, keepdims=True) + EPS) * ng_ref[...]


def _ssd_call(u, w, cw, cb, dtb, alog, dsk, ng, *, chunk=SSD_CHUNK):
    n_b, seq, d = u.shape
    rblk = min(seq, 512)
    kern = functools.partial(_ssd_kernel, seq=seq, chunk=chunk, rblk=rblk)
    vec = lambda n: pl.BlockSpec((None, 1, n), lambda b, g: (g, 0, 0))
    return pl.pallas_call(
        kern,
        out_shape=jax.ShapeDtypeStruct((n_b, seq, SSD_G * SSD_GW), F32),
        grid=(n_b, SSD_G),
        in_specs=[pl.BlockSpec((None, seq, d), lambda b, g: (b, 0, 0)),
                  pl.BlockSpec((None, d, _SSD_W_END), lambda b, g: (g, 0, 0)),
                  pl.BlockSpec((None, SSD_CONV, _SSD_XBC_G), lambda b, g: (g, 0, 0)),
                  vec(_SSD_XBC_G), vec(LANES), vec(LANES), vec(SSD_GW), vec(SSD_GW)],
        out_specs=pl.BlockSpec((None, seq, SSD_GW), lambda b, g: (b, 0, g)),
        scratch_shapes=[pltpu.VMEM((seq + 2 * _SSD_PAD, _SSD_XBC_G), F32),
                        pltpu.VMEM((seq, SSD_GW), F32),
                        pltpu.VMEM((seq, SSD_N), BF16),
                        pltpu.VMEM((seq, SSD_N), BF16),
                        pltpu.VMEM((seq, LANES), F32),
                        pltpu.VMEM((seq, LANES), F32),
                        pltpu.VMEM((seq // chunk, 2 * SSD_HG, chunk), F32),
                        pltpu.VMEM((seq // chunk, 2 * SSD_HG, chunk), F32),
                        pltpu.VMEM((seq, chunk), F32),
                        pltpu.VMEM((seq, SSD_GW), BF16),
                        pltpu.VMEM((seq, SSD_GW), BF16),
                        pltpu.VMEM((seq // chunk, SSD_GW), F32),
                        pltpu.VMEM((SSD_N, SSD_GW), F32),
                        pltpu.VMEM((seq // chunk, SSD_N, SSD_GW), BF16)],
        compiler_params=pltpu.CompilerParams(dimension_semantics=("parallel", "arbitrary"),
                                             vmem_limit_bytes=VMEM_LIMIT),
        name="ssd_branch",
    )(u, w, cw, cb, dtb, alog, dsk, ng)


def _merge_kernel(u_ref, h_ref, y0_ref, y1_ref, y2_ref, wg_ref, wb_ref, wo_ref, o_ref, acc_s):
    br = pl.program_id(1)
    gate = _sigmoid(_dot(u_ref[...], wg_ref[...]))

    def branch(y_ref):
        return gate * _dot(y_ref[...].astype(BF16), wb_ref[...])

    @pl.when(br == 0)
    def _():
        acc_s[...] = branch(y0_ref)

    @pl.when(br == 1)
    def _():
        acc_s[...] = acc_s[...] + branch(y1_ref)

    @pl.when(br == 2)
    def _():
        merged = acc_s[...] + branch(y2_ref)
        o_ref[...] = h_ref[...] + _dot(merged.astype(BF16), wo_ref[...])


def _merge_call(u2d, h2d, y0, y1, y2, wg, wb, wo, tm=512):
    n_tok, d = h2d.shape
    tm = min(tm, n_tok)
    tok = lambda: pl.BlockSpec((tm, d), lambda i, r: (i, 0))
    return pl.pallas_call(
        _merge_kernel,
        out_shape=jax.ShapeDtypeStruct((n_tok, d), F32),
        grid=(n_tok // tm, 3),
        in_specs=[tok(), tok(), tok(), tok(), tok(),
                  pl.BlockSpec((None, d, d), lambda i, r: (r, 0, 0)),
                  pl.BlockSpec((None, d, d), lambda i, r: (r, 0, 0)),
                  pl.BlockSpec((d, d), lambda i, r: (0, 0))],
        out_specs=tok(),
        scratch_shapes=[pltpu.VMEM((tm, d), F32)],
        compiler_params=pltpu.CompilerParams(dimension_semantics=("parallel", "arbitrary"),
                                             vmem_limit_bytes=VMEM_LIMIT),
        name="merge",
    )(u2d, h2d, y0, y1, y2, wg, wb, wo)


_FFN_CW = 256
_FFN_NJ = D_FF // _FFN_CW
_FFN_HALO = SUBLANES

def _ffn_kernel(h_ref, hp_ref, hn_ref, p_ref, g_ref, wa_ref, wv_ref, cwa_ref, cwv_ref,
                cba_ref, cbv_ref, wd_ref, wple_ref, wpg_ref, fg_ref, o_ref, u_s,
                *, tm, n_tiles, final_norm):
    i = pl.program_id(1)
    j = pl.program_id(2)

    def norm(x):
        return (x * lax.rsqrt(jnp.mean(x * x, axis=-1, keepdims=True) + EPS) * g_ref[...]).astype(BF16)

    @pl.when(j == 0)
    def _():
        up = jnp.where(i > 0, 1.0, 0.0)
        dn = jnp.where(i < n_tiles - 1, 1.0, 0.0)
        u_s[0:_FFN_HALO, :] = norm(hp_ref[...] * up)
        u_s[_FFN_HALO:_FFN_HALO + tm, :] = norm(h_ref[...])
        u_s[_FFN_HALO + tm:2 * _FFN_HALO + tm, :] = norm(hn_ref[...] * dn)

    def conv(w_ref, cw_ref, cb_ref):
        up = _dot(u_s[...], w_ref[...])
        acc = cb_ref[...]
        for t in range(FFN_CONV):
            off = _FFN_HALO + t - FFN_CONV // 2
            acc = acc + up[off:off + tm, :] * cw_ref[t:t + 1, :]
        return acc

    act = (_silu(conv(wa_ref, cwa_ref, cba_ref)) * conv(wv_ref, cwv_ref, cbv_ref)).astype(BF16)
    part = _dot(act, wd_ref[...])

    @pl.when(j == 0)
    def _():
        o_ref[...] = h_ref[...] + part

    @pl.when(j > 0)
    def _():
        o_ref[...] = o_ref[...] + part

    @pl.when(j == _FFN_NJ - 1)
    def _():
        h2 = o_ref[...]
        ple = _dot(p_ref[...].astype(BF16), wple_ref[...])
        h3 = h2 + ple * _sigmoid(_dot(h2.astype(BF16), wpg_ref[...]))
        if final_norm:
            h3 = h3 * lax.rsqrt(jnp.mean(h3 * h3, axis=-1, keepdims=True) + EPS) * fg_ref[...]
        o_ref[...] = h3


def _ffn_call(h, p, g, w_up, cw, cb, w_down, w_ple, w_pg, fg, *, final_norm, tm=1024):
    n_b, seq, d = h.shape
    tm = min(tm, seq)
    n_tiles = seq // tm
    hb = tm // _FFN_HALO
    n_hb = seq // _FFN_HALO
    kern = functools.partial(_ffn_kernel, tm=tm, n_tiles=n_tiles, final_norm=final_norm)
    cst = lambda shape: pl.BlockSpec(shape, lambda b, i, j: tuple(0 for _ in shape))
    return pl.pallas_call(
        kern,
        out_shape=jax.ShapeDtypeStruct((n_b, seq, d), F32),
        grid=(n_b, n_tiles, _FFN_NJ),
        in_specs=[pl.BlockSpec((None, tm, d), lambda b, i, j: (b, i, 0)),
                  pl.BlockSpec((None, _FFN_HALO, d), lambda b, i, j: (b, jnp.maximum(i * hb - 1, 0), 0)),
                  pl.BlockSpec((None, _FFN_HALO, d), lambda b, i, j: (b, jnp.minimum((i + 1) * hb, n_hb - 1), 0)),
                  pl.BlockSpec((None, tm, PLE_DIM), lambda b, i, j: (b, i, 0)),
                  cst((1, d)),
                  pl.BlockSpec((d, _FFN_CW), lambda b, i, j: (0, j)),
                  pl.BlockSpec((d, _FFN_CW), lambda b, i, j: (0, _FFN_NJ + j)),
                  pl.BlockSpec((FFN_CONV, _FFN_CW), lambda b, i, j: (0, j)),
                  pl.BlockSpec((FFN_CONV, _FFN_CW), lambda b, i, j: (0, _FFN_NJ + j)),
                  pl.BlockSpec((1, _FFN_CW), lambda b, i, j: (0, j)),
                  pl.BlockSpec((1, _FFN_CW), lambda b, i, j: (0, _FFN_NJ + j)),
                  pl.BlockSpec((_FFN_CW, d), lambda b, i, j: (j, 0)),
                  cst((PLE_DIM, d)), cst((d, d)), cst((1, d))],
        out_specs=pl.BlockSpec((None, tm, d), lambda b, i, j: (b, i, 0)),
        scratch_shapes=[pltpu.VMEM((tm + 2 * _FFN_HALO, d), BF16)],
        compiler_params=pltpu.CompilerParams(
            dimension_semantics=("parallel", "parallel", "arbitrary"),
            vmem_limit_bytes=VMEM_LIMIT),
        name="convffn_ple",
    )(h, h, h, p, g.reshape(1, d), w_up, w_up, cw, cw, cb.reshape(1, -1), cb.reshape(1, -1),
      w_down, w_ple, w_pg, fg.reshape(1, d))


def _cols(w, start, width):
    return lax.slice_in_dim(w, start, start + width, axis=1)


def _pad_cols(w, width):
    return jnp.pad(w, ((0, 0), (0, width - w.shape[1])))


def _ssd_params(w_in, conv_w, conv_b, dt_bias, a_log, d_skip, norm_g):
    ws, cws, cbs, dtbs, alogs = [], [], [], [], []
    for g in range(SSD_G):
        dt_cols = jnp.concatenate(
            [_cols(w_in, _OFF_DT + dd * SSD_H + g * SSD_HG, SSD_HG) for dd in range(2)], axis=1)
        ws.append(jnp.concatenate([
            _cols(w_in, _OFF_Z + g * SSD_GW, SSD_GW),
            _cols(w_in, _OFF_XBC + g * SSD_GW, SSD_GW),
            _cols(w_in, _OFF_XBC + D_MODEL + g * SSD_N, SSD_N),
            _cols(w_in, _OFF_XBC + D_MODEL + SSD_G * SSD_N + g * SSD_N, SSD_N),
            _pad_cols(dt_cols, LANES)], axis=1))
        pick = lambda a: jnp.concatenate([
            _cols(a, g * SSD_GW, SSD_GW),
            _cols(a, D_MODEL + g * SSD_N, SSD_N),
            _cols(a, D_MODEL + SSD_G * SSD_N + g * SSD_N, SSD_N)], axis=1)
        cws.append(pick(conv_w))
        cbs.append(pick(conv_b.reshape(1, -1)))
        head_row = lambda a: _pad_cols(
            jnp.concatenate([a[dd, g * SSD_HG:(g + 1) * SSD_HG] for dd in range(2)]).reshape(1, -1), LANES)
        dtbs.append(head_row(dt_bias))
        alogs.append(head_row(a_log))
    dsk = jnp.repeat(d_skip, SSD_HEAD_DIM).reshape(SSD_G, 1, SSD_GW)
    return (jnp.stack(ws).astype(BF16), jnp.stack(cws), jnp.stack(cbs), jnp.stack(dtbs),
            jnp.stack(alogs), dsk, norm_g.reshape(SSD_G, 1, SSD_GW))


def _hgrn2_params(w_in, lb, norm_g):
    ws = [jnp.concatenate([
        _cols(w_in, _OFF_GQ + h * HEAD, HEAD),
        _cols(w_in, _OFF_GF + h * HEAD, HEAD),
        _cols(w_in, _OFF_GF + D_MODEL + h * HEAD, HEAD),
        _cols(w_in, _OFF_GI + h * HEAD, HEAD),
        _cols(w_in, _OFF_GG + h * HEAD, HEAD)], axis=1) for h in range(HG_H)]
    lbs = lb.reshape(2, HG_H, HEAD).swapaxes(0, 1)
    return jnp.stack(ws).astype(BF16), lbs, norm_g.reshape(HG_H, 1, HEAD)


def _mlstm_params(w_in, i_bias, f_bias, norm_g):
    ws, biases = [], []
    for h in range(ML_H):
        gate_cols = jnp.concatenate([
            _cols(w_in, _OFF_MI + h, 1), _cols(w_in, _OFF_MI + ML_H + h, 1),
            _cols(w_in, _OFF_MF + h, 1), _cols(w_in, _OFF_MF + ML_H + h, 1)], axis=1)
        ws.append(jnp.concatenate([
            _cols(w_in, _OFF_MQ + h * HEAD, HEAD),
            _cols(w_in, _OFF_MK + h * HEAD, HEAD),
            _cols(w_in, _OFF_MV + h * HEAD, HEAD),
            _cols(w_in, _OFF_MO + h * HEAD, HEAD),
            _pad_cols(gate_cols, LANES)], axis=1))
        biases.append(_pad_cols(
            jnp.stack([i_bias[0, h], i_bias[1, h], f_bias[0, h], f_bias[1, h]]).reshape(1, 4), LANES))
    return jnp.stack(ws).astype(BF16), jnp.stack(biases), norm_g.reshape(ML_H, 1, HEAD)


def kernel(x, p, norm_mix_g, w_in, ssd_conv_w, ssd_conv_b, ssd_dt_bias, ssd_a_log, ssd_d, ssd_norm_g, hg_lb_raw, hg_norm_g, ml_i_bias, ml_f_bias, ml_norm_g, w_br_ssd, w_br_hg, w_br_ml, w_out, norm_ffn_g, w_up, ffn_conv_w, ffn_conv_b, w_down, w_ple, w_ple_gate, final_norm_g):
    n_b, seq, d = x.shape
    depth = w_in.shape[0]
    lb_soft = jax.nn.softmax(hg_lb_raw.astype(F32), axis=0)
    hg_lb = jnp.cumsum(lb_soft, axis=0) - lb_soft[0:1]
    h = x
    for l in range(depth):
        u2d = _rmsnorm_call(h.reshape(n_b * seq, d), norm_mix_g[l], BF16)
        u = u2d.reshape(n_b, seq, d)
        y_ssd = _ssd_call(u, *_ssd_params(w_in[l], ssd_conv_w[l], ssd_conv_b[l], ssd_dt_bias[l],
                                          ssd_a_log[l], ssd_d[l], ssd_norm_g[l]))
        y_hg = _hgrn2_call(u, *_hgrn2_params(w_in[l], hg_lb[l], hg_norm_g[l]))
        y_ml = _mlstm_call(u, *_mlstm_params(w_in[l], ml_i_bias[l], ml_f_bias[l], ml_norm_g[l]))
        wg = jnp.stack([_cols(w_in[l], _OFF_GATES + r * d, d) for r in range(3)]).astype(BF16)
        wb = jnp.stack([w_br_ssd[l], w_br_hg[l], w_br_ml[l]]).astype(BF16)
        tok = lambda a: a.reshape(n_b * seq, d)
        h = _merge_call(u2d, tok(h), tok(y_ssd), tok(y_hg), tok(y_ml), wg, wb,
                        w_out[l].astype(BF16)).reshape(n_b, seq, d)
        h = _ffn_call(h, p[l], norm_ffn_g[l], w_up[l].astype(BF16), ffn_conv_w[l], ffn_conv_b[l],
                      w_down[l].astype(BF16), w_ple[l].astype(BF16), w_ple_gate[l].astype(BF16),
                      final_norm_g, final_norm=(l == depth - 1))
    return h
```

```python
import functools

import jax
import jax.numpy as jnp
from jax import lax
from jax.experimental import pallas as pl
from jax.experimental.pallas import tpu as pltpu

F32 = jnp.float32
BF16 = jnp.bfloat16

D_MODEL = 1024
PLE_DIM = 256
EPS = 1e-6
NEG = -1e30
TINY = 1e-30

SSD_HEAD_DIM = 64
SSD_H = 16
SSD_G = 2
SSD_HG = 8
SSD_N = 128
SSD_CONV = 5
SSD_GW = SSD_HG * SSD_HEAD_DIM
SSD_XBC = D_MODEL + 2 * SSD_G * SSD_N
HG_H = 8
ML_H = 8
HEAD = 128
D_FF = 2816
FFN_CONV = 3

LANES = 128
SUBLANES = 8
VMEM_LIMIT = 56 * 1024 * 1024

SSD_CHUNK = 128
HG_CHUNK = 64
ML_CHUNK = 128
SSD_UNROLL = 2
ML_UNROLL = 8
HG_UNROLL = 16
HG_SAFE_DECAY = 60.0

_OFF_Z = 0
_OFF_XBC = 1024
_OFF_DT = _OFF_XBC + SSD_XBC
_OFF_GQ = _OFF_DT + 2 * SSD_H
_OFF_GF = _OFF_GQ + 1024
_OFF_GI = _OFF_GF + 2048
_OFF_GG = _OFF_GI + 1024
_OFF_MQ = _OFF_GG + 1024
_OFF_MK = _OFF_MQ + 1024
_OFF_MV = _OFF_MK + 1024
_OFF_MI = _OFF_MV + 1024
_OFF_MF = _OFF_MI + 2 * ML_H
_OFF_MO = _OFF_MF + 2 * ML_H
_OFF_GATES = _OFF_MO + 1024


def _dot(a, b):
    return jnp.dot(a, b, preferred_element_type=F32)


def _dot_nt(a, b):
    return lax.dot_general(a, b, (((1,), (1,)), ((), ())), preferred_element_type=F32)


def _dot_tn(a, b):
    return lax.dot_general(a, b, (((0,), (0,)), ((), ())), preferred_element_type=F32)


def _sigmoid(x):
    return 1.0 / (1.0 + jnp.exp(-x))


def _silu(x):
    return x * _sigmoid(x)


def _softplus(x):
    return jnp.maximum(x, 0.0) + jnp.log1p(jnp.exp(-jnp.abs(x)))


def _chunk_scan(x, chunk, reverse, op):
    n_rows = x.shape[0]
    row = lax.broadcasted_iota(jnp.int32, x.shape, 0) & (chunk - 1)
    sh = 1
    while sh < chunk:
        if reverse:
            shifted = pltpu.roll(x, n_rows - sh, axis=0)
            ok = row < chunk - sh
        else:
            shifted = pltpu.roll(x, sh, axis=0)
            ok = row >= sh
        if op == "add":
            x = x + jnp.where(ok, shifted, 0.0)
        else:
            x = jnp.maximum(x, jnp.where(ok, shifted, NEG))
        sh *= 2
    return x


def _lane_scan(x, seg, reverse, op, start=1):
    n_lanes = x.shape[1]
    pos = lax.broadcasted_iota(jnp.int32, x.shape, 1) & (seg - 1)
    sh = start
    while sh < seg:
        if reverse:
            shifted = pltpu.roll(x, n_lanes - sh, axis=1)
            ok = pos < seg - sh
        else:
            shifted = pltpu.roll(x, sh, axis=1)
            ok = pos >= sh
        if op == "add":
            x = x + jnp.where(ok, shifted, 0.0)
        else:
            x = jnp.maximum(x, jnp.where(ok, shifted, NEG))
        sh *= 2
    return x


def _tri_mask(n, reverse):
    t = lax.broadcasted_iota(jnp.int32, (n, n), 0)
    s = lax.broadcasted_iota(jnp.int32, (n, n), 1)
    return (s >= t) if reverse else (s <= t)


def _rmsnorm_kernel(h_ref, g_ref, o_ref):
    x = h_ref[...]
    ms = jnp.mean(x * x, axis=-1, keepdims=True)
    o_ref[...] = (x * lax.rsqrt(ms + EPS) * g_ref[...]).astype(o_ref.dtype)


def _rmsnorm_call(h2d, g, out_dtype, tm=1024):
    n_tok, d = h2d.shape
    tm = min(tm, n_tok)
    return pl.pallas_call(
        _rmsnorm_kernel,
        out_shape=jax.ShapeDtypeStruct((n_tok, d), out_dtype),
        grid=(n_tok // tm,),
        in_specs=[pl.BlockSpec((tm, d), lambda i: (i, 0)),
                  pl.BlockSpec((1, d), lambda i: (0, 0))],
        out_specs=pl.BlockSpec((tm, d), lambda i: (i, 0)),
        compiler_params=pltpu.CompilerParams(dimension_semantics=("parallel",),
                                             vmem_limit_bytes=VMEM_LIMIT),
        name="rmsnorm",
    )(h2d, g.reshape(1, d))


def _hg_att_safe(qc, kc, lfc, reverse):
    n = qc.shape[0]
    t = lax.broadcasted_iota(jnp.int32, (n, n), 0)
    s = lax.broadcasted_iota(jnp.int32, (n, n), 1)
    att = jnp.where(t == s, jnp.sum(qc * kc, axis=-1, keepdims=True), 0.0)
    half = 1
    while half < n:
        pre = _chunk_scan(lfc, half, False, "add")
        suf = _chunk_scan(lfc, half, True, "add")
        t_hi = (t & half) != 0
        s_hi = (s & half) != 0
        same_block = (t ^ s) < 2 * half
        if reverse:
            qe = qc * jnp.exp(suf)
            ke = kc * jnp.exp(pre - lfc)
            pair = jnp.logical_and(same_block, jnp.logical_and(jnp.logical_not(t_hi), s_hi))
        else:
            qe = qc * jnp.exp(pre)
            ke = kc * jnp.exp(suf - lfc)
            pair = jnp.logical_and(same_block, jnp.logical_and(t_hi, jnp.logical_not(s_hi)))
        att = att + jnp.where(pair, _dot_nt(qe.astype(BF16), ke.astype(BF16)), 0.0)
        half *= 2
    return att


def _hgrn2_kernel(u_ref, w_ref, lb_ref, ng_ref, o_ref,
                  q_s, v_s, k_s, lf_s, b_s, qe_s, ke_s, qi_s, kd_s, dec_s, upd_s, st_s, acc_s,
                  *, seq, chunk):
    n_chunks = seq // chunk
    proj = _dot(u_ref[...], w_ref[...])
    q_s[...] = _silu(proj[:, 0:HEAD])
    v_s[...] = proj[:, 3 * HEAD:4 * HEAD].astype(BF16)
    o_ref[...] = proj[:, 4 * HEAD:5 * HEAD]

    for d in range(2):
        f_raw = proj[:, (1 + d) * HEAD:(2 + d) * HEAD]
        lb = lb_ref[d:d + 1, :]
        f = lb + (1.0 - lb) * _sigmoid(f_raw)
        k_s[d] = (1.0 - lb) * _sigmoid(-f_raw)
        log_f = jnp.log(jnp.maximum(f, TINY))
        lf_s[d] = log_f
        b = _chunk_scan(log_f, chunk, d == 1, "add")
        b_s[d] = b

    def ref_span(d):
        first, mid, last = (b_s[d, pl.ds(r, n_chunks, stride=chunk), :] for r in (0, chunk // 2, chunk - 1))
        return jnp.maximum(jnp.abs(first - mid), jnp.abs(mid - last))

    mild_decay = jnp.max(jnp.maximum(ref_span(0), ref_span(1))) <= HG_SAFE_DECAY

    for d in range(2):
        reverse = d == 1
        mask = _tri_mask(chunk, reverse)
        end_row = 0 if reverse else chunk - 1

        def chunk_rows(c):
            r0 = pl.multiple_of(c * chunk, chunk)
            return r0, pl.ds(r0, chunk)

        def scale_body(c, carry, d=d, end_row=end_row):
            r0, rows = chunk_rows(c)
            qc = q_s[rows, :]
            kc = k_s[d, rows, :]
            bc = b_s[d, rows, :]
            ref = b_s[d, pl.ds(r0 + chunk // 2, 1), :]
            bend = b_s[d, pl.ds(r0 + end_row, 1), :]
            qe_s[rows, :] = (qc * jnp.exp(bc - ref)).astype(BF16)
            ke_s[rows, :] = (kc * jnp.exp(ref - bc)).astype(BF16)
            qi_s[rows, :] = (qc * jnp.exp(bc)).astype(BF16)
            kd_s[rows, :] = (kc * jnp.exp(bend - bc)).astype(BF16)
            dec_s[c] = jnp.broadcast_to(jnp.exp(bend), (HEAD, HEAD)).T
            return carry

        lax.fori_loop(0, n_chunks, scale_body, 0, unroll=HG_UNROLL)

        def upd_body(c, carry):
            _, rows = chunk_rows(c)
            upd_s[c] = _dot_tn(kd_s[rows, :], v_s[rows, :])
            return carry

        lax.fori_loop(0, n_chunks, upd_body, 0, unroll=HG_UNROLL)

        def rec_body(ci, st, reverse=reverse):
            c = (n_chunks - 1 - ci) if reverse else ci
            st_s[c] = st.astype(BF16)
            return st * dec_s[c] + upd_s[c]

        lax.fori_loop(0, n_chunks, rec_body, jnp.zeros((HEAD, HEAD), F32))

        def emit(c, rows, att, d=d):
            o = _dot(jnp.concatenate([qi_s[rows, :], att.astype(BF16)], axis=1),
                     jnp.concatenate([st_s[c], v_s[rows, :]], axis=0))
            if d == 0:
                acc_s[rows, :] = o
            else:
                acc_s[rows, :] = acc_s[rows, :] + o

        @pl.when(mild_decay)
        def _(mask=mask):
            for g0 in range(0, n_chunks, HG_UNROLL):
                group = [(c, slice(c * chunk, (c + 1) * chunk)) for c in range(g0, min(g0 + HG_UNROLL, n_chunks))]
                atts = [jnp.where(mask, _dot_nt(qe_s[rows, :], ke_s[rows, :]), 0.0) for _, rows in group]
                for (c, rows), att in zip(group, atts):
                    emit(c, rows, att)

        @pl.when(jnp.logical_not(mild_decay))
        def _(d=d, reverse=reverse):
            def safe_body(c, carry):
                _, rows = chunk_rows(c)
                emit(c, rows, _hg_att_safe(q_s[rows, :], k_s[d, rows, :], lf_s[d, rows, :], reverse))
                return carry

            lax.fori_loop(0, n_chunks, safe_body, 0)

    o = acc_s[...]
    o = o * lax.rsqrt(jnp.mean(o * o, axis=-1, keepdims=True) + EPS) * ng_ref[...]
    o_ref[...] = o * _silu(o_ref[...])


def _hgrn2_call(u, w, lb, ng, *, chunk=HG_CHUNK):
    n_b, seq, d = u.shape
    kern = functools.partial(_hgrn2_kernel, seq=seq, chunk=chunk)
    return pl.pallas_call(
        kern,
        out_shape=jax.ShapeDtypeStruct((n_b, seq, HG_H * HEAD), F32),
        grid=(n_b, HG_H),
        in_specs=[pl.BlockSpec((None, seq, d), lambda b, h: (b, 0, 0)),
                  pl.BlockSpec((None, d, 5 * HEAD), lambda b, h: (h, 0, 0)),
                  pl.BlockSpec((None, 2, HEAD), lambda b, h: (h, 0, 0)),
                  pl.BlockSpec((None, 1, HEAD), lambda b, h: (h, 0, 0))],
        out_specs=pl.BlockSpec((None, seq, HEAD), lambda b, h: (b, 0, h)),
        scratch_shapes=[pltpu.VMEM((seq, HEAD), F32),
                        pltpu.VMEM((seq, HEAD), BF16),
                        pltpu.VMEM((2, seq, HEAD), F32),
                        pltpu.VMEM((2, seq, HEAD), F32),
                        pltpu.VMEM((2, seq, HEAD), F32),
                        pltpu.VMEM((seq, HEAD), BF16),
                        pltpu.VMEM((seq, HEAD), BF16),
                        pltpu.VMEM((seq, HEAD), BF16),
                        pltpu.VMEM((seq, HEAD), BF16),
                        pltpu.VMEM((seq // chunk, HEAD, HEAD), F32),
                        pltpu.VMEM((seq // chunk, HEAD, HEAD), F32),
                        pltpu.VMEM((seq // chunk, HEAD, HEAD), BF16),
                        pltpu.VMEM((seq, HEAD), F32)],
        compiler_params=pltpu.CompilerParams(dimension_semantics=("parallel", "arbitrary"),
                                             vmem_limit_bytes=VMEM_LIMIT),
        name="hgrn2_branch",
    )(u, w, lb, ng)


def _log_sigmoid(x):
    return jnp.minimum(x, 0.0) - jnp.log1p(jnp.exp(-jnp.abs(x)))


def _mlstm_kernel(u_ref, w_ref, bias_ref, ng_ref, o_ref,
                  qb_s, kb_s, va_s, tm_s, qw_s, kw_s, mx_s, en_s, gt_s, dec_s, upd_s, ct_s,
                  acc_s, *, seq, chunk):
    n_chunks = seq // chunk
    proj = _dot(u_ref[...], w_ref[...])
    q = proj[:, 0:HEAD]
    k = proj[:, HEAD:2 * HEAD] * (HEAD ** -0.5)
    qb_s[...] = q.astype(BF16)
    kb_s[...] = k.astype(BF16)
    va_s[:, 0:HEAD] = proj[:, 2 * HEAD:3 * HEAD].astype(BF16)
    va_s[:, HEAD:2 * HEAD] = jnp.ones((seq, HEAD), BF16)
    o_ref[...] = proj[:, 3 * HEAD:4 * HEAD]

    x = (proj[:, 4 * HEAD:5 * HEAD] + bias_ref[...]).T[0:SUBLANES, :]
    fwd_row = lax.broadcasted_iota(jnp.int32, (SUBLANES, seq), 0) == 0
    pos = lax.broadcasted_iota(jnp.int32, (SUBLANES, seq), 1)

    def scans(y, op, seg, start=1):
        return _lane_scan(y, seg, False, op, start), _lane_scan(y, seg, True, op, start)

    it = x
    log_f = pltpu.roll(_log_sigmoid(x), SUBLANES - 2, axis=0)
    pre, suf = scans(log_f, "add", chunk)
    bt = jnp.where(fwd_row, pre, suf)
    bt_end = pre + suf - log_f
    g = it - bt
    g_pre, g_suf = scans(g, "max", chunk)
    cm = jnp.where(fwd_row, g_pre, g_suf)
    cm_end = jnp.maximum(g_pre, g_suf)

    p_pre, p_suf = scans(bt_end, "add", seq, chunk)
    p_sum = jnp.where(fwd_row, p_pre, p_suf)
    x_pre, x_suf = scans(cm_end - (p_sum - bt_end), "max", seq, chunk)
    m_end = p_sum + jnp.where(fwd_row, x_pre, x_suf)
    m_prev = jnp.where(fwd_row,
                       jnp.where(pos < chunk, NEG, pltpu.roll(m_end, chunk, axis=1)),
                       jnp.where(pos >= seq - chunk, NEG, pltpu.roll(m_end, seq - chunk, axis=1)))
    mx = jnp.maximum(m_prev, cm)
    w_inter = jnp.exp(m_prev - mx)
    wk = jnp.exp(bt_end - bt + it - m_end)
    en = jnp.exp(-(bt + mx))
    decay = jnp.exp(bt_end + m_prev - m_end)

    for c in range(n_chunks):
        gt_s[c] = g[:, c * chunk:(c + 1) * chunk]
    row = lax.broadcasted_iota(jnp.int32, (SUBLANES, seq), 0)
    packed = jnp.where(row < 2, w_inter,
                       jnp.where(row < 4, pltpu.roll(wk, 2, axis=0),
                                 jnp.where(row < 6, pltpu.roll(mx, 4, axis=0), pltpu.roll(en, 6, axis=0))))
    tm_s[...] = jnp.concatenate(
        [packed, decay, jnp.zeros((LANES - 2 * SUBLANES, seq), F32)], axis=0).T
    for d in range(2):
        def lanes(col, ref_rows=slice(None)):
            block = tm_s[ref_rows, :]
            return jnp.broadcast_to(block[:, col:col + 1], (block.shape[0], LANES))
        qw_s[d] = (q * lanes(d)).astype(BF16)
        kw_s[d] = (k * lanes(2 + d)).astype(BF16)
        mx_s[d] = lanes(4 + d)
        en_s[d] = lanes(6 + d)
        dec_s[d] = lanes(8 + d, pl.ds(0, n_chunks, stride=chunk))

    for d in range(2):
        reverse = d == 1
        mask = _tri_mask(chunk, reverse)

        def chunk_rows(c):
            return pl.ds(pl.multiple_of(c * chunk, chunk), chunk)

        def upd_body(c, carry, d=d):
            rows = chunk_rows(c)
            upd_s[c] = _dot_tn(kw_s[d, rows, :], va_s[rows, :])
            return carry

        lax.fori_loop(0, n_chunks, upd_body, 0, unroll=ML_UNROLL)

        def rec_body(ci, ct, reverse=reverse, d=d):
            c = (n_chunks - 1 - ci) if reverse else ci
            ct_s[c] = ct.astype(BF16)
            dec = dec_s[d, pl.ds(c, 1), :]
            return ct * jnp.concatenate([dec, dec], axis=1) + upd_s[c]

        lax.fori_loop(0, n_chunks, rec_body, jnp.zeros((HEAD, 2 * HEAD), F32))

        def scores(c, rows):
            p = jnp.exp(jnp.where(mask, gt_s[c][d:d + 1, :] - mx_s[d, rows, :], NEG))
            return (_dot_nt(qb_s[rows, :], kb_s[rows, :]) * p).astype(BF16)

        def emit(c, rows, sc):
            tot = _dot(jnp.concatenate([qw_s[d, rows, :], sc], axis=1),
                       jnp.concatenate([ct_s[c], va_s[rows, :]], axis=0))
            hval = tot[:, 0:HEAD] / jnp.maximum(jnp.abs(tot[:, HEAD:2 * HEAD]), en_s[d, rows, :])
            if d == 0:
                acc_s[rows, :] = hval
            else:
                acc_s[rows, :] = acc_s[rows, :] + hval

        for g0 in range(0, n_chunks, ML_UNROLL):
            group = [(c, slice(c * chunk, (c + 1) * chunk)) for c in range(g0, min(g0 + ML_UNROLL, n_chunks))]
            staged = [scores(c, rows) for c, rows in group]
            for (c, rows), sc in zip(group, staged):
                emit(c, rows, sc)

    hsum = acc_s[...]
    hc = hsum - jnp.mean(hsum, axis=-1, keepdims=True)
    hn = hc * lax.rsqrt(jnp.mean(hc * hc, axis=-1, keepdims=True) + EPS) * ng_ref[...]
    o_ref[...] = hn * _sigmoid(o_ref[...])


def _mlstm_call(u, w, bias, ng, *, chunk=ML_CHUNK):
    n_b, seq, d = u.shape
    n_chunks = seq // chunk
    kern = functools.partial(_mlstm_kernel, seq=seq, chunk=chunk)
    return pl.pallas_call(
        kern,
        out_shape=jax.ShapeDtypeStruct((n_b, seq, ML_H * HEAD), F32),
        grid=(n_b, ML_H),
        in_specs=[pl.BlockSpec((None, seq, d), lambda b, h: (b, 0, 0)),
                  pl.BlockSpec((None, d, 5 * HEAD), lambda b, h: (h, 0, 0)),
                  pl.BlockSpec((None, 1, HEAD), lambda b, h: (h, 0, 0)),
                  pl.BlockSpec((None, 1, HEAD), lambda b, h: (h, 0, 0))],
        out_specs=pl.BlockSpec((None, seq, HEAD), lambda b, h: (b, 0, h)),
        scratch_shapes=[pltpu.VMEM((seq, HEAD), BF16),
                        pltpu.VMEM((seq, HEAD), BF16),
                        pltpu.VMEM((seq, 2 * HEAD), BF16),
                        pltpu.VMEM((seq, LANES), F32),
                        pltpu.VMEM((2, seq, HEAD), BF16),
                        pltpu.VMEM((2, seq, HEAD), BF16),
                        pltpu.VMEM((2, seq, LANES), F32),
                        pltpu.VMEM((2, seq, LANES), F32),
                        pltpu.VMEM((n_chunks, SUBLANES, chunk), F32),
                        pltpu.VMEM((2, n_chunks, LANES), F32),
                        pltpu.VMEM((n_chunks, HEAD, 2 * HEAD), F32),
                        pltpu.VMEM((n_chunks, HEAD, 2 * HEAD), BF16),
                        pltpu.VMEM((seq, HEAD), F32)],
        compiler_params=pltpu.CompilerParams(dimension_semantics=("parallel", "arbitrary"),
                                             vmem_limit_bytes=VMEM_LIMIT),
        name="mlstm_branch",
    )(u, w, bias, ng)


_SSD_W_Z = 0
_SSD_W_XBC = SSD_GW
_SSD_W_DT = SSD_GW + SSD_GW + 2 * SSD_N
_SSD_W_END = _SSD_W_DT + LANES
_SSD_XBC_G = SSD_GW + 2 * SSD_N
_SSD_PAD = SUBLANES
_SSD_PAIRS = SSD_HG // 2


def _ssd_kernel(u_ref, w_ref, cw_ref, cb_ref, dtb_ref, alog_ref, dsk_ref, ng_ref, o_ref,
                pad_s, xs_s, b_s, c_s, dt_s, a_s, at_s, dtt_s, cbm_s, ea_s, xw_s, dec_s, cur_s, st_s,
                *, seq, chunk, rblk):
    n_chunks = seq // chunk
    n_rblk = seq // rblk

    zero_rows = jnp.zeros((_SSD_PAD, _SSD_XBC_G), F32)
    pad_s[0:_SSD_PAD, :] = zero_rows
    pad_s[_SSD_PAD + seq:2 * _SSD_PAD + seq, :] = zero_rows
    for i in range(n_rblk):
        r0 = i * rblk
        pad_s[_SSD_PAD + r0:_SSD_PAD + r0 + rblk, :] = _dot(
            u_ref[r0:r0 + rblk, :], w_ref[:, _SSD_W_XBC:_SSD_W_DT])
    for i in range(n_rblk):
        r0 = i * rblk
        acc = cb_ref[...]
        for j in range(SSD_CONV):
            off = _SSD_PAD + r0 + j - SSD_CONV // 2
            acc = acc + pad_s[off:off + rblk, :] * cw_ref[j:j + 1, :]
        xbc = _silu(acc)
        xs_s[r0:r0 + rblk, :] = xbc[:, 0:SSD_GW]
        b_s[r0:r0 + rblk, :] = xbc[:, SSD_GW:SSD_GW + SSD_N].astype(BF16)
        c_s[r0:r0 + rblk, :] = xbc[:, SSD_GW + SSD_N:SSD_GW + 2 * SSD_N].astype(BF16)

    dt = _softplus(_dot(u_ref[...], w_ref[:, _SSD_W_DT:_SSD_W_END]) + dtb_ref[...])
    dt_s[...] = dt
    log_a = dt * (-jnp.exp(alog_ref[...]))
    lane = lax.broadcasted_iota(jnp.int32, (seq, LANES), 1)
    acum = jnp.where(lane < SSD_HG,
                     _chunk_scan(log_a, chunk, False, "add"),
                     _chunk_scan(log_a, chunk, True, "add"))
    a_s[...] = acum
    acum_t = acum.T
    dt_t = dt.T
    for c in range(n_chunks):
        at_s[c] = acum_t[0:2 * SSD_HG, c * chunk:(c + 1) * chunk]
        dtt_s[c] = dt_t[0:2 * SSD_HG, c * chunk:(c + 1) * chunk]
        r0 = c * chunk
        cbm_s[r0:r0 + chunk, :] = _dot_nt(c_s[r0:r0 + chunk, :], b_s[r0:r0 + chunk, :])

    fwd_lane_c = lax.broadcasted_iota(jnp.int32, (n_chunks, LANES), 1) < SSD_HG
    a_end = jnp.where(fwd_lane_c, a_s[pl.ds(chunk - 1, n_chunks, stride=chunk), :],
                      a_s[pl.ds(0, n_chunks, stride=chunk), :])
    lane_lo = lax.broadcasted_iota(jnp.int32, (chunk, LANES), 1) < SSD_HEAD_DIM

    def split3(x):
        hi = x.astype(BF16)
        r1 = x - hi.astype(F32)
        mid = r1.astype(BF16)
        return [hi, mid, (r1 - mid.astype(F32)).astype(BF16)]

    for d in range(2):
        reverse = d == 1
        mask = _tri_mask(chunk, reverse)
        spread = (lax.broadcasted_iota(jnp.int32, (LANES, SSD_GW), 1) // SSD_HEAD_DIM
                  == lax.broadcasted_iota(jnp.int32, (LANES, SSD_GW), 0) - d * SSD_HG).astype(BF16)
        spread2 = jnp.concatenate([spread, spread], axis=0)
        dec_s[...] = _dot(jnp.concatenate(split3(jnp.exp(a_end)), axis=1),
                          jnp.concatenate([spread2, spread], axis=0))
        for i in range(n_rblk):
            r0 = i * rblk
            rows = slice(r0, r0 + rblk)
            a_blk = a_s[rows, :]
            end_tok = jnp.concatenate(
                [jnp.broadcast_to(a_end[c:c + 1, :], (chunk, LANES))
                 for c in range(r0 // chunk, (r0 + rblk) // chunk)], axis=0)
            ea_s[rows, :] = _dot(jnp.exp(a_blk).astype(BF16), spread).astype(BF16)
            wt = split3(dt_s[rows, :] * jnp.exp(end_tok - a_blk))
            xw_s[rows, :] = (xs_s[rows, :] * _dot(jnp.concatenate(wt[0:2], axis=1), spread2)).astype(BF16)

        def chunk_rows(c):
            return pl.ds(pl.multiple_of(c * chunk, chunk), chunk)

        def upd_body(c, carry):
            rows = chunk_rows(c)
            pad_s[rows, 0:SSD_GW] = _dot_tn(b_s[rows, :], xw_s[rows, :])
            return carry

        lax.fori_loop(0, n_chunks, upd_body, 0, unroll=SSD_UNROLL)

        cur_s[...] = jnp.zeros_like(cur_s)

        def rec_body(ci, carry, reverse=reverse):
            c = (n_chunks - 1 - ci) if reverse else ci
            cur = cur_s[...]
            st_s[c] = cur.astype(BF16)
            cur_s[...] = cur * dec_s[pl.ds(c, 1), :] + pad_s[chunk_rows(c), 0:SSD_GW]
            return carry

        lax.fori_loop(0, n_chunks, rec_body, 0)

        def out_body(c, carry, mask=mask, d=d):
            rows = chunk_rows(c)
            blk = a_s[rows, :]
            blk_t = at_s[c]
            dtt = dtt_s[c]
            cbm = cbm_s[rows, :]
            y_inter = _dot(c_s[rows, :], st_s[c])
            for p in range(_SSD_PAIRS):
                cols = slice(p * LANES, (p + 1) * LANES)
                xs_pair = xs_s[rows, cols]
                m_tiles, x_tiles = [], []
                for hh in range(2):
                    col = d * SSD_HG + 2 * p + hh
                    seg = jnp.exp(jnp.where(mask, blk[:, col:col + 1] - blk_t[col:col + 1, :], NEG))
                    m_tiles.append((cbm * seg * dtt[col:col + 1, :]).astype(BF16))
                    keep = lane_lo if hh == 0 else jnp.logical_not(lane_lo)
                    x_tiles.append(jnp.where(keep, xs_pair, 0.0).astype(BF16))
                y = (y_inter[:, cols] * ea_s[rows, cols].astype(F32)
                     + _dot(jnp.concatenate(m_tiles, axis=1), jnp.concatenate(x_tiles, axis=0)))
                if d == 0:
                    o_ref[rows, cols] = y
                else:
                    o_ref[rows, cols] = o_ref[rows, cols] + y
            return carry

        lax.fori_loop(0, n_chunks, out_body, 0, unroll=SSD_UNROLL)

    for i in range(n_rblk):
        r0 = i * rblk
        rows = slice(r0, r0 + rblk)
        z = _dot(u_ref[rows, :], w_ref[:, _SSD_W_Z:_SSD_W_XBC])
        y = (o_ref[rows, :] + xs_s[rows, :] * dsk_ref[...]) * _silu(z)
        o_ref[rows, :] = y * lax.rsqrt(jnp.mean(y * y, axis=-1, keepdims=True) + EPS) * ng_ref[...]


def _ssd_call(u, w, cw, cb, dtb, alog, dsk, ng, *, chunk=SSD_CHUNK):
    n_b, seq, d = u.shape
    rblk = min(seq, 512)
    kern = functools.partial(_ssd_kernel, seq=seq, chunk=chunk, rblk=rblk)
    vec = lambda n: pl.BlockSpec((None, 1, n), lambda b, g: (g, 0, 0))
    return pl.pallas_call(
        kern,
        out_shape=jax.ShapeDtypeStruct((n_b, seq, SSD_G * SSD_GW), F32),
        grid=(n_b, SSD_G),
        in_specs=[pl.BlockSpec((None, seq, d), lambda b, g: (b, 0, 0)),
                  pl.BlockSpec((None, d, _SSD_W_END), lambda b, g: (g, 0, 0)),
                  pl.BlockSpec((None, SSD_CONV, _SSD_XBC_G), lambda b, g: (g, 0, 0)),
                  vec(_SSD_XBC_G), vec(LANES), vec(LANES), vec(SSD_GW), vec(SSD_GW)],
        out_specs=pl.BlockSpec((None, seq, SSD_GW), lambda b, g: (b, 0, g)),
        scratch_shapes=[pltpu.VMEM((seq + 2 * _SSD_PAD, _SSD_XBC_G), F32),
                        pltpu.VMEM((seq, SSD_GW), F32),
                        pltpu.VMEM((seq, SSD_N), BF16),
                        pltpu.VMEM((seq, SSD_N), BF16),
                        pltpu.VMEM((seq, LANES), F32),
                        pltpu.VMEM((seq, LANES), F32),
                        pltpu.VMEM((seq // chunk, 2 * SSD_HG, chunk), F32),
                        pltpu.VMEM((seq // chunk, 2 * SSD_HG, chunk), F32),
                        pltpu.VMEM((seq, chunk), F32),
                        pltpu.VMEM((seq, SSD_GW), BF16),
                        pltpu.VMEM((seq, SSD_GW), BF16),
                        pltpu.VMEM((seq // chunk, SSD_GW), F32),
                        pltpu.VMEM((SSD_N, SSD_GW), F32),
                        pltpu.VMEM((seq // chunk, SSD_N, SSD_GW), BF16)],
        compiler_params=pltpu.CompilerParams(dimension_semantics=("parallel", "arbitrary"),
                                             vmem_limit_bytes=VMEM_LIMIT),
        name="ssd_branch",
    )(u, w, cw, cb, dtb, alog, dsk, ng)


def _merge_kernel(u_ref, h_ref, y0_ref, y1_ref, y2_ref, wg_ref, wb_ref, wo_ref, o_ref, acc_s):
    br = pl.program_id(1)
    gate = _sigmoid(_dot(u_ref[...], wg_ref[...]))

    def branch(y_ref):
        return gate * _dot(y_ref[...].astype(BF16), wb_ref[...])

    @pl.when(br == 0)
    def _():
        acc_s[...] = branch(y0_ref)

    @pl.when(br == 1)
    def _():
        acc_s[...] = acc_s[...] + branch(y1_ref)

    @pl.when(br == 2)
    def _():
        merged = acc_s[...] + branch(y2_ref)
        o_ref[...] = h_ref[...] + _dot(merged.astype(BF16), wo_ref[...])


def _merge_call(u2d, h2d, y0, y1, y2, wg, wb, wo, tm=512):
    n_tok, d = h2d.shape
    tm = min(tm, n_tok)
    tok = lambda: pl.BlockSpec((tm, d), lambda i, r: (i, 0))
    return pl.pallas_call(
        _merge_kernel,
        out_shape=jax.ShapeDtypeStruct((n_tok, d), F32),
        grid=(n_tok // tm, 3),
        in_specs=[tok(), tok(), tok(), tok(), tok(),
                  pl.BlockSpec((None, d, d), lambda i, r: (r, 0, 0)),
                  pl.BlockSpec((None, d, d), lambda i, r: (r, 0, 0)),
                  pl.BlockSpec((d, d), lambda i, r: (0, 0))],
        out_specs=tok(),
        scratch_shapes=[pltpu.VMEM((tm, d), F32)],
        compiler_params=pltpu.CompilerParams(dimension_semantics=("parallel", "arbitrary"),
                                             vmem_limit_bytes=VMEM_LIMIT),
        name="merge",
    )(u2d, h2d, y0, y1, y2, wg, wb, wo)


_FFN_CW = 256
_FFN_NJ = D_FF // _FFN_CW
_FFN_HALO = SUBLANES

def _ffn_kernel(h_ref, hp_ref, hn_ref, p_ref, g_ref, wa_ref, wv_ref, cwa_ref, cwv_ref,
                cba_ref, cbv_ref, wd_ref, wple_ref, wpg_ref, fg_ref, o_ref, u_s,
                *, tm, n_tiles, final_norm):
    i = pl.program_id(1)
    j = pl.program_id(2)

    def norm(x):
        return (x * lax.rsqrt(jnp.mean(x * x, axis=-1, keepdims=True) + EPS) * g_ref[...]).astype(BF16)

    @pl.when(j == 0)
    def _():
        up = jnp.where(i > 0, 1.0, 0.0)
        dn = jnp.where(i < n_tiles - 1, 1.0, 0.0)
        u_s[0:_FFN_HALO, :] = norm(hp_ref[...] * up)
        u_s[_FFN_HALO:_FFN_HALO + tm, :] = norm(h_ref[...])
        u_s[_FFN_HALO + tm:2 * _FFN_HALO + tm, :] = norm(hn_ref[...] * dn)

    def conv(w_ref, cw_ref, cb_ref):
        up = _dot(u_s[...], w_ref[...])
        acc = cb_ref[...]
        for t in range(FFN_CONV):
            off = _FFN_HALO + t - FFN_CONV // 2
            acc = acc + up[off:off + tm, :] * cw_ref[t:t + 1, :]
        return acc

    act = (_silu(conv(wa_ref, cwa_ref, cba_ref)) * conv(wv_ref, cwv_ref, cbv_ref)).astype(BF16)
    part = _dot(act, wd_ref[...])

    @pl.when(j == 0)
    def _():
        o_ref[...] = h_ref[...] + part

    @pl.when(j > 0)
    def _():
        o_ref[...] = o_ref[...] + part

    @pl.when(j == _FFN_NJ - 1)
    def _():
        h2 = o_ref[...]
        ple = _dot(p_ref[...].astype(BF16), wple_ref[...])
        h3 = h2 + ple * _sigmoid(_dot(h2.astype(BF16), wpg_ref[...]))
        if final_norm:
            h3 = h3 * lax.rsqrt(jnp.mean(h3 * h3, axis=-1, keepdims=True) + EPS) * fg_ref[...]
        o_ref[...] = h3


def _ffn_call(h, p, g, w_up, cw, cb, w_down, w_ple, w_pg, fg, *, final_norm, tm=1024):
    n_b, seq, d = h.shape
    tm = min(tm, seq)
    n_tiles = seq // tm
    hb = tm // _FFN_HALO
    n_hb = seq // _FFN_HALO
    kern = functools.partial(_ffn_kernel, tm=tm, n_tiles=n_tiles, final_norm=final_norm)
    cst = lambda shape: pl.BlockSpec(shape, lambda b, i, j: tuple(0 for _ in shape))
    return pl.pallas_call(
        kern,
        out_shape=jax.ShapeDtypeStruct((n_b, seq, d), F32),
        grid=(n_b, n_tiles, _FFN_NJ),
        in_specs=[pl.BlockSpec((None, tm, d), lambda b, i, j: (b, i, 0)),
                  pl.BlockSpec((None, _FFN_HALO, d), lambda b, i, j: (b, jnp.maximum(i * hb - 1, 0), 0)),
                  pl.BlockSpec((None, _FFN_HALO, d), lambda b, i, j: (b, jnp.minimum((i + 1) * hb, n_hb - 1), 0)),
                  pl.BlockSpec((None, tm, PLE_DIM), lambda b, i, j: (b, i, 0)),
                  cst((1, d)),
                  pl.BlockSpec((d, _FFN_CW), lambda b, i, j: (0, j)),
                  pl.BlockSpec((d, _FFN_CW), lambda b, i, j: (0, _FFN_NJ + j)),
                  pl.BlockSpec((FFN_CONV, _FFN_CW), lambda b, i, j: (0, j)),
                  pl.BlockSpec((FFN_CONV, _FFN_CW), lambda b, i, j: (0, _FFN_NJ + j)),
                  pl.BlockSpec((1, _FFN_CW), lambda b, i, j: (0, j)),
                  pl.BlockSpec((1, _FFN_CW), lambda b, i, j: (0, _FFN_NJ + j)),
                  pl.BlockSpec((_FFN_CW, d), lambda b, i, j: (j, 0)),
                  cst((PLE_DIM, d)), cst((d, d)), cst((1, d))],
        out_specs=pl.BlockSpec((None, tm, d), lambda b, i, j: (b, i, 0)),
        scratch_shapes=[pltpu.VMEM((tm + 2 * _FFN_HALO, d), BF16)],
        compiler_params=pltpu.CompilerParams(
            dimension_semantics=("parallel", "parallel", "arbitrary"),
            vmem_limit_bytes=VMEM_LIMIT),
        name="convffn_ple",
    )(h, h, h, p, g.reshape(1, d), w_up, w_up, cw, cw, cb.reshape(1, -1), cb.reshape(1, -1),
      w_down, w_ple, w_pg, fg.reshape(1, d))


def _cols(w, start, width):
    return lax.slice_in_dim(w, start, start + width, axis=1)


def _pad_cols(w, width):
    return jnp.pad(w, ((0, 0), (0, width - w.shape[1])))


def _ssd_params(w_in, conv_w, conv_b, dt_bias, a_log, d_skip, norm_g):
    ws, cws, cbs, dtbs, alogs = [], [], [], [], []
    for g in range(SSD_G):
        dt_cols = jnp.concatenate(
            [_cols(w_in, _OFF_DT + dd * SSD_H + g * SSD_HG, SSD_HG) for dd in range(2)], axis=1)
        ws.append(jnp.concatenate([
            _cols(w_in, _OFF_Z + g * SSD_GW, SSD_GW),
            _cols(w_in, _OFF_XBC + g * SSD_GW, SSD_GW),
            _cols(w_in, _OFF_XBC + D_MODEL + g * SSD_N, SSD_N),
            _cols(w_in, _OFF_XBC + D_MODEL + SSD_G * SSD_N + g * SSD_N, SSD_N),
            _pad_cols(dt_cols, LANES)], axis=1))
        pick = lambda a: jnp.concatenate([
            _cols(a, g * SSD_GW, SSD_GW),
            _cols(a, D_MODEL + g * SSD_N, SSD_N),
            _cols(a, D_MODEL + SSD_G * SSD_N + g * SSD_N, SSD_N)], axis=1)
        cws.append(pick(conv_w))
        cbs.append(pick(conv_b.reshape(1, -1)))
        head_row = lambda a: _pad_cols(
            jnp.concatenate([a[dd, g * SSD_HG:(g + 1) * SSD_HG] for dd in range(2)]).reshape(1, -1), LANES)
        dtbs.append(head_row(dt_bias))
        alogs.append(head_row(a_log))
    dsk = jnp.repeat(d_skip, SSD_HEAD_DIM).reshape(SSD_G, 1, SSD_GW)
    return (jnp.stack(ws).astype(BF16), jnp.stack(cws), jnp.stack(cbs), jnp.stack(dtbs),
            jnp.stack(alogs), dsk, norm_g.reshape(SSD_G, 1, SSD_GW))


def _hgrn2_params(w_in, lb, norm_g):
    ws = [jnp.concatenate([
        _cols(w_in, _OFF_GQ + h * HEAD, HEAD),
        _cols(w_in, _OFF_GF + h * HEAD, HEAD),
        _cols(w_in, _OFF_GF + D_MODEL + h * HEAD, HEAD),
        _cols(w_in, _OFF_GI + h * HEAD, HEAD),
        _cols(w_in, _OFF_GG + h * HEAD, HEAD)], axis=1) for h in range(HG_H)]
    lbs = lb.reshape(2, HG_H, HEAD).swapaxes(0, 1)
    return jnp.stack(ws).astype(BF16), lbs, norm_g.reshape(HG_H, 1, HEAD)


def _mlstm_params(w_in, i_bias, f_bias, norm_g):
    ws, biases = [], []
    for h in range(ML_H):
        gate_cols = jnp.concatenate([
            _cols(w_in, _OFF_MI + h, 1), _cols(w_in, _OFF_MI + ML_H + h, 1),
            _cols(w_in, _OFF_MF + h, 1), _cols(w_in, _OFF_MF + ML_H + h, 1)], axis=1)
        ws.append(jnp.concatenate([
            _cols(w_in, _OFF_MQ + h * HEAD, HEAD),
            _cols(w_in, _OFF_MK + h * HEAD, HEAD),
            _cols(w_in, _OFF_MV + h * HEAD, HEAD),
            _cols(w_in, _OFF_MO + h * HEAD, HEAD),
            _pad_cols(gate_cols, LANES)], axis=1))
        biases.append(_pad_cols(
            jnp.stack([i_bias[0, h], i_bias[1, h], f_bias[0, h], f_bias[1, h]]).reshape(1, 4), LANES))
    return jnp.stack(ws).astype(BF16), jnp.stack(biases), norm_g.reshape(ML_H, 1, HEAD)


def kernel(x, p, norm_mix_g, w_in, ssd_conv_w, ssd_conv_b, ssd_dt_bias, ssd_a_log, ssd_d, ssd_norm_g, hg_lb_raw, hg_norm_g, ml_i_bias, ml_f_bias, ml_norm_g, w_br_ssd, w_br_hg, w_br_ml, w_out, norm_ffn_g, w_up, ffn_conv_w, ffn_conv_b, w_down, w_ple, w_ple_gate, final_norm_g):
    n_b, seq, d = x.shape
    depth = w_in.shape[0]
    lb_soft = jax.nn.softmax(hg_lb_raw.astype(F32), axis=0)
    hg_lb = jnp.cumsum(lb_soft, axis=0) - lb_soft[0:1]
    h = x
    for l in range(depth):
        u2d = _rmsnorm_call(h.reshape(n_b * seq, d), norm_mix_g[l], BF16)
        u = u2d.reshape(n_b, seq, d)
        y_ssd = _ssd_call(u, *_ssd_params(w_in[l], ssd_conv_w[l], ssd_conv_b[l], ssd_dt_bias[l],
                                          ssd_a_log[l], ssd_d[l], ssd_norm_g[l]))
        y_hg = _hgrn2_call(u, *_hgrn2_params(w_in[l], hg_lb[l], hg_norm_g[l]))
        y_ml = _mlstm_call(u, *_mlstm_params(w_in[l], ml_i_bias[l], ml_f_bias[l], ml_norm_g[l]))
        wg = jnp.stack([_cols(w_in[l], _OFF_GATES + r * d, d) for r in range(3)]).astype(BF16)
        wb = jnp.stack([w_br_ssd[l], w_br_hg[l], w_br_ml[l]]).astype(BF16)
        tok = lambda a: a.reshape(n_b * seq, d)
        h = _merge_call(u2d, tok(h), tok(y_ssd), tok(y_hg), tok(y_ml), wg, wb,
                        w_out[l].astype(BF16)).reshape(n_b, seq, d)
        h = _ffn_call(h, p[l], norm_ffn_g[l], w_up[l].astype(BF16), ffn_conv_w[l], ffn_conv_b[l],
                      w_down[l].astype(BF16), w_ple[l].astype(BF16), w_ple_gate[l].astype(BF16),
                      final_norm_g, final_norm=(l == depth - 1))
    return h
```

```python
import functools

import jax
import jax.numpy as jnp
from jax import lax
from jax.experimental import pallas as pl
from jax.experimental.pallas import tpu as pltpu

F32 = jnp.float32
BF16 = jnp.bfloat16

D_MODEL = 1024
PLE_DIM = 256
EPS = 1e-6
NEG = -1e30
TINY = 1e-30

SSD_HEAD_DIM = 64
SSD_H = 16
SSD_G = 2
SSD_HG = 8
SSD_N = 128
SSD_CONV = 5
SSD_GW = SSD_HG * SSD_HEAD_DIM
SSD_XBC = D_MODEL + 2 * SSD_G * SSD_N
HG_H = 8
ML_H = 8
HEAD = 128
D_FF = 2816
FFN_CONV = 3

LANES = 128
SUBLANES = 8
VMEM_LIMIT = 56 * 1024 * 1024

SSD_CHUNK = 128
HG_CHUNK = 64
ML_CHUNK = 128
SSD_UNROLL = 2
ML_UNROLL = 8
HG_UNROLL = 16
HG_SAFE_DECAY = 60.0

_OFF_Z = 0
_OFF_XBC = 1024
_OFF_DT = _OFF_XBC + SSD_XBC
_OFF_GQ = _OFF_DT + 2 * SSD_H
_OFF_GF = _OFF_GQ + 1024
_OFF_GI = _OFF_GF + 2048
_OFF_GG = _OFF_GI + 1024
_OFF_MQ = _OFF_GG + 1024
_OFF_MK = _OFF_MQ + 1024
_OFF_MV = _OFF_MK + 1024
_OFF_MI = _OFF_MV + 1024
_OFF_MF = _OFF_MI + 2 * ML_H
_OFF_MO = _OFF_MF + 2 * ML_H
_OFF_GATES = _OFF_MO + 1024


def _dot(a, b):
    return jnp.dot(a, b, preferred_element_type=F32)


def _dot_nt(a, b):
    return lax.dot_general(a, b, (((1,), (1,)), ((), ())), preferred_element_type=F32)


def _dot_tn(a, b):
    return lax.dot_general(a, b, (((0,), (0,)), ((), ())), preferred_element_type=F32)


def _sigmoid(x):
    return 1.0 / (1.0 + jnp.exp(-x))


def _silu(x):
    return x * _sigmoid(x)


def _softplus(x):
    return jnp.maximum(x, 0.0) + jnp.log1p(jnp.exp(-jnp.abs(x)))


def _chunk_scan(x, chunk, reverse, op):
    n_rows = x.shape[0]
    row = lax.broadcasted_iota(jnp.int32, x.shape, 0) & (chunk - 1)
    sh = 1
    while sh < chunk:
        if reverse:
            shifted = pltpu.roll(x, n_rows - sh, axis=0)
            ok = row < chunk - sh
        else:
            shifted = pltpu.roll(x, sh, axis=0)
            ok = row >= sh
        if op == "add":
            x = x + jnp.where(ok, shifted, 0.0)
        else:
            x = jnp.maximum(x, jnp.where(ok, shifted, NEG))
        sh *= 2
    return x


def _lane_scan(x, seg, reverse, op, start=1):
    n_lanes = x.shape[1]
    pos = lax.broadcasted_iota(jnp.int32, x.shape, 1) & (seg - 1)
    sh = start
    while sh < seg:
        if reverse:
            shifted = pltpu.roll(x, n_lanes - sh, axis=1)
            ok = pos < seg - sh
        else:
            shifted = pltpu.roll(x, sh, axis=1)
            ok = pos >= sh
        if op == "add":
            x = x + jnp.where(ok, shifted, 0.0)
        else:
            x = jnp.maximum(x, jnp.where(ok, shifted, NEG))
        sh *= 2
    return x


def _tri_mask(n, reverse):
    t = lax.broadcasted_iota(jnp.int32, (n, n), 0)
    s = lax.broadcasted_iota(jnp.int32, (n, n), 1)
    return (s >= t) if reverse else (s <= t)


def _rmsnorm_kernel(h_ref, g_ref, o_ref):
    x = h_ref[...]
    ms = jnp.mean(x * x, axis=-1, keepdims=True)
    o_ref[...] = (x * lax.rsqrt(ms + EPS) * g_ref[...]).astype(o_ref.dtype)


def _rmsnorm_call(h2d, g, out_dtype, tm=1024):
    n_tok, d = h2d.shape
    tm = min(tm, n_tok)
    return pl.pallas_call(
        _rmsnorm_kernel,
        out_shape=jax.ShapeDtypeStruct((n_tok, d), out_dtype),
        grid=(n_tok // tm,),
        in_specs=[pl.BlockSpec((tm, d), lambda i: (i, 0)),
                  pl.BlockSpec((1, d), lambda i: (0, 0))],
        out_specs=pl.BlockSpec((tm, d), lambda i: (i, 0)),
        compiler_params=pltpu.CompilerParams(dimension_semantics=("parallel",),
                                             vmem_limit_bytes=VMEM_LIMIT),
        name="rmsnorm",
    )(h2d, g.reshape(1, d))


def _hg_att_safe(qc, kc, lfc, reverse):
    n = qc.shape[0]
    t = lax.broadcasted_iota(jnp.int32, (n, n), 0)
    s = lax.broadcasted_iota(jnp.int32, (n, n), 1)
    att = jnp.where(t == s, jnp.sum(qc * kc, axis=-1, keepdims=True), 0.0)
    half = 1
    while half < n:
        pre = _chunk_scan(lfc, half, False, "add")
        suf = _chunk_scan(lfc, half, True, "add")
        t_hi = (t & half) != 0
        s_hi = (s & half) != 0
        same_block = (t ^ s) < 2 * half
        if reverse:
            qe = qc * jnp.exp(suf)
            ke = kc * jnp.exp(pre - lfc)
            pair = jnp.logical_and(same_block, jnp.logical_and(jnp.logical_not(t_hi), s_hi))
        else:
            qe = qc * jnp.exp(pre)
            ke = kc * jnp.exp(suf - lfc)
            pair = jnp.logical_and(same_block, jnp.logical_and(t_hi, jnp.logical_not(s_hi)))
        att = att + jnp.where(pair, _dot_nt(qe.astype(BF16), ke.astype(BF16)), 0.0)
        half *= 2
    return att


def _hgrn2_kernel(u_ref, w_ref, lb_ref, ng_ref, o_ref,
                  q_s, v_s, k_s, lf_s, b_s, qe_s, ke_s, qi_s, kd_s, dec_s, upd_s, st_s, acc_s,
                  *, seq, chunk):
    n_chunks = seq // chunk
    proj = _dot(u_ref[...], w_ref[...])
    q_s[...] = _silu(proj[:, 0:HEAD])
    v_s[...] = proj[:, 3 * HEAD:4 * HEAD].astype(BF16)
    o_ref[...] = proj[:, 4 * HEAD:5 * HEAD]

    for d in range(2):
        f_raw = proj[:, (1 + d) * HEAD:(2 + d) * HEAD]
        lb = lb_ref[d:d + 1, :]
        f = lb + (1.0 - lb) * _sigmoid(f_raw)
        k_s[d] = (1.0 - lb) * _sigmoid(-f_raw)
        log_f = jnp.log(jnp.maximum(f, TINY))
        lf_s[d] = log_f
        b = _chunk_scan(log_f, chunk, d == 1, "add")
        b_s[d] = b

    def ref_span(d):
        first, mid, last = (b_s[d, pl.ds(r, n_chunks, stride=chunk), :] for r in (0, chunk // 2, chunk - 1))
        return jnp.maximum(jnp.abs(first - mid), jnp.abs(mid - last))

    mild_decay = jnp.max(jnp.maximum(ref_span(0), ref_span(1))) <= HG_SAFE_DECAY

    for d in range(2):
        reverse = d == 1
        mask = _tri_mask(chunk, reverse)
        end_row = 0 if reverse else chunk - 1

        def chunk_rows(c):
            r0 = pl.multiple_of(c * chunk, chunk)
            return r0, pl.ds(r0, chunk)

        def scale_body(c, carry, d=d, end_row=end_row):
            r0, rows = chunk_rows(c)
            qc = q_s[rows, :]
            kc = k_s[d, rows, :]
            bc = b_s[d, rows, :]
            ref = b_s[d, pl.ds(r0 + chunk // 2, 1), :]
            bend = b_s[d, pl.ds(r0 + end_row, 1), :]
            qe_s[rows, :] = (qc * jnp.exp(bc - ref)).astype(BF16)
            ke_s[rows, :] = (kc * jnp.exp(ref - bc)).astype(BF16)
            qi_s[rows, :] = (qc * jnp.exp(bc)).astype(BF16)
            kd_s[rows, :] = (kc * jnp.exp(bend - bc)).astype(BF16)
            dec_s[c] = jnp.broadcast_to(jnp.exp(bend), (HEAD, HEAD)).T
            return carry

        lax.fori_loop(0, n_chunks, scale_body, 0, unroll=HG_UNROLL)

        def upd_body(c, carry):
            _, rows = chunk_rows(c)
            upd_s[c] = _dot_tn(kd_s[rows, :], v_s[rows, :])
            return carry

        lax.fori_loop(0, n_chunks, upd_body, 0, unroll=HG_UNROLL)

        def rec_body(ci, st, reverse=reverse):
            c = (n_chunks - 1 - ci) if reverse else ci
            st_s[c] = st.astype(BF16)
            return st * dec_s[c] + upd_s[c]

        lax.fori_loop(0, n_chunks, rec_body, jnp.zeros((HEAD, HEAD), F32))

        def emit(c, rows, att, d=d):
            o = _dot(jnp.concatenate([qi_s[rows, :], att.astype(BF16)], axis=1),
                     jnp.concatenate([st_s[c], v_s[rows, :]], axis=0))
            if d == 0:
                acc_s[rows, :] = o
            else:
                acc_s[rows, :] = acc_s[rows, :] + o

        @pl.when(mild_decay)
        def _(mask=mask):
            for g0 in range(0, n_chunks, HG_UNROLL):
                group = [(c, slice(c * chunk, (c + 1) * chunk)) for c in range(g0, min(g0 + HG_UNROLL, n_chunks))]
                atts = [jnp.where(mask, _dot_nt(qe_s[rows, :], ke_s[rows, :]), 0.0) for _, rows in group]
                for (c, rows), att in zip(group, atts):
                    emit(c, rows, att)

        @pl.when(jnp.logical_not(mild_decay))
        def _(d=d, reverse=reverse):
            def safe_body(c, carry):
                _, rows = chunk_rows(c)
                emit(c, rows, _hg_att_safe(q_s[rows, :], k_s[d, rows, :], lf_s[d, rows, :], reverse))
                return carry

            lax.fori_loop(0, n_chunks, safe_body, 0)

    o = acc_s[...]
    o = o * lax.rsqrt(jnp.mean(o * o, axis=-1, keepdims=True) + EPS) * ng_ref[...]
    o_ref[...] = o * _silu(o_ref[...])


def _hgrn2_call(u, w, lb, ng, *, chunk=HG_CHUNK):
    n_b, seq, d = u.shape
    kern = functools.partial(_hgrn2_kernel, seq=seq, chunk=chunk)
    return pl.pallas_call(
        kern,
        out_shape=jax.ShapeDtypeStruct((n_b, seq, HG_H * HEAD), F32),
        grid=(n_b, HG_H),
        in_specs=[pl.BlockSpec((None, seq, d), lambda b, h: (b, 0, 0)),
                  pl.BlockSpec((None, d, 5 * HEAD), lambda b, h: (h, 0, 0)),
                  pl.BlockSpec((None, 2, HEAD), lambda b, h: (h, 0, 0)),
                  pl.BlockSpec((None, 1, HEAD), lambda b, h: (h, 0, 0))],
        out_specs=pl.BlockSpec((None, seq, HEAD), lambda b, h: (b, 0, h)),
        scratch_shapes=[pltpu.VMEM((seq, HEAD), F32),
                        pltpu.VMEM((seq, HEAD), BF16),
                        pltpu.VMEM((2, seq, HEAD), F32),
                        pltpu.VMEM((2, seq, HEAD), F32),
                        pltpu.VMEM((2, seq, HEAD), F32),
                        pltpu.VMEM((seq, HEAD), BF16),
                        pltpu.VMEM((seq, HEAD), BF16),
                        pltpu.VMEM((seq, HEAD), BF16),
                        pltpu.VMEM((seq, HEAD), BF16),
                        pltpu.VMEM((seq // chunk, HEAD, HEAD), F32),
                        pltpu.VMEM((seq // chunk, HEAD, HEAD), F32),
                        pltpu.VMEM((seq // chunk, HEAD, HEAD), BF16),
                        pltpu.VMEM((seq, HEAD), F32)],
        compiler_params=pltpu.CompilerParams(dimension_semantics=("parallel", "arbitrary"),
                                             vmem_limit_bytes=VMEM_LIMIT),
        name="hgrn2_branch",
    )(u, w, lb, ng)


def _log_sigmoid(x):
    return jnp.minimum(x, 0.0) - jnp.log1p(jnp.exp(-jnp.abs(x)))


def _mlstm_kernel(u_ref, w_ref, bias_ref, ng_ref, o_ref,
                  qb_s, kb_s, va_s, tm_s, qw_s, kw_s, mx_s, en_s, gt_s, dec_s, upd_s, ct_s,
                  acc_s, *, seq, chunk):
    n_chunks = seq // chunk
    proj = _dot(u_ref[...], w_ref[...])
    q = proj[:, 0:HEAD]
    k = proj[:, HEAD:2 * HEAD] * (HEAD ** -0.5)
    qb_s[...] = q.astype(BF16)
    kb_s[...] = k.astype(BF16)
    va_s[:, 0:HEAD] = proj[:, 2 * HEAD:3 * HEAD].astype(BF16)
    va_s[:, HEAD:2 * HEAD] = jnp.ones((seq, HEAD), BF16)
    o_ref[...] = proj[:, 3 * HEAD:4 * HEAD]

    x = (proj[:, 4 * HEAD:5 * HEAD] + bias_ref[...]).T[0:SUBLANES, :]
    fwd_row = lax.broadcasted_iota(jnp.int32, (SUBLANES, seq), 0) == 0
    pos = lax.broadcasted_iota(jnp.int32, (SUBLANES, seq), 1)

    def scans(y, op, seg, start=1):
        return _lane_scan(y, seg, False, op, start), _lane_scan(y, seg, True, op, start)

    it = x
    log_f = pltpu.roll(_log_sigmoid(x), SUBLANES - 2, axis=0)
    pre, suf = scans(log_f, "add", chunk)
    bt = jnp.where(fwd_row, pre, suf)
    bt_end = pre + suf - log_f
    g = it - bt
    g_pre, g_suf = scans(g, "max", chunk)
    cm = jnp.where(fwd_row, g_pre, g_suf)
    cm_end = jnp.maximum(g_pre, g_suf)

    p_pre, p_suf = scans(bt_end, "add", seq, chunk)
    p_sum = jnp.where(fwd_row, p_pre, p_suf)
    x_pre, x_suf = scans(cm_end - (p_sum - bt_end), "max", seq, chunk)
    m_end = p_sum + jnp.where(fwd_row, x_pre, x_suf)
    m_prev = jnp.where(fwd_row,
                       jnp.where(pos < chunk, NEG, pltpu.roll(m_end, chunk, axis=1)),
                       jnp.where(pos >= seq - chunk, NEG, pltpu.roll(m_end, seq - chunk, axis=1)))
    mx = jnp.maximum(m_prev, cm)
    w_inter = jnp.exp(m_prev - mx)
    wk = jnp.exp(bt_end - bt + it - m_end)
    en = jnp.exp(-(bt + mx))
    decay = jnp.exp(bt_end + m_prev - m_end)

    for c in range(n_chunks):
        gt_s[c] = g[:, c * chunk:(c + 1) * chunk]
    row = lax.broadcasted_iota(jnp.int32, (SUBLANES, seq), 0)
    packed = jnp.where(row < 2, w_inter,
                       jnp.where(row < 4, pltpu.roll(wk, 2, axis=0),
                                 jnp.where(row < 6, pltpu.roll(mx, 4, axis=0), pltpu.roll(en, 6, axis=0))))
    tm_s[...] = jnp.concatenate(
        [packed, decay, jnp.zeros((LANES - 2 * SUBLANES, seq), F32)], axis=0).T
    for d in range(2):
        def lanes(col, ref_rows=slice(None)):
            block = tm_s[ref_rows, :]
            return jnp.broadcast_to(block[:, col:col + 1], (block.shape[0], LANES))
        qw_s[d] = (q * lanes(d)).astype(BF16)
        kw_s[d] = (k * lanes(2 + d)).astype(BF16)
        mx_s[d] = lanes(4 + d)
        en_s[d] = lanes(6 + d)
        dec_s[d] = lanes(8 + d, pl.ds(0, n_chunks, stride=chunk))

    for d in range(2):
        reverse = d == 1
        mask = _tri_mask(chunk, reverse)

        def chunk_rows(c):
            return pl.ds(pl.multiple_of(c * chunk, chunk), chunk)

        def upd_body(c, carry, d=d):
            rows = chunk_rows(c)
            upd_s[c] = _dot_tn(kw_s[d, rows, :], va_s[rows, :])
            return carry

        lax.fori_loop(0, n_chunks, upd_body, 0, unroll=ML_UNROLL)

        def rec_body(ci, ct, reverse=reverse, d=d):
            c = (n_chunks - 1 - ci) if reverse else ci
            ct_s[c] = ct.astype(BF16)
            dec = dec_s[d, pl.ds(c, 1), :]
            return ct * jnp.concatenate([dec, dec], axis=1) + upd_s[c]

        lax.fori_loop(0, n_chunks, rec_body, jnp.zeros((HEAD, 2 * HEAD), F32))

        def scores(c, rows):
            p = jnp.exp(jnp.where(mask, gt_s[c][d:d + 1, :] - mx_s[d, rows, :], NEG))
            return (_dot_nt(qb_s[rows, :], kb_s[rows, :]) * p).astype(BF16)

        def emit(c, rows, sc):
            tot = _dot(jnp.concatenate([qw_s[d, rows, :], sc], axis=1),
                       jnp.concatenate([ct_s[c], va_s[rows, :]], axis=0))
            hval = tot[:, 0:HEAD] / jnp.maximum(jnp.abs(tot[:, HEAD:2 * HEAD]), en_s[d, rows, :])
            if d == 0:
                acc_s[rows, :] = hval
            else:
                acc_s[rows, :] = acc_s[rows, :] + hval

        for g0 in range(0, n_chunks, ML_UNROLL):
            group = [(c, slice(c * chunk, (c + 1) * chunk)) for c in range(g0, min(g0 + ML_UNROLL, n_chunks))]
            staged = [scores(c, rows) for c, rows in group]
            for (c, rows), sc in zip(group, staged):
                emit(c, rows, sc)

    hsum = acc_s[...]
    hc = hsum - jnp.mean(hsum, axis=-1, keepdims=True)
    hn = hc * lax.rsqrt(jnp.mean(hc * hc, axis=-1, keepdims=True) + EPS) * ng_ref[...]
    o_ref[...] = hn * _sigmoid(o_ref[...])


def _mlstm_call(u, w, bias, ng, *, chunk=ML_CHUNK):
    n_b, seq, d = u.shape
    n_chunks = seq // chunk
    kern = functools.partial(_mlstm_kernel, seq=seq, chunk=chunk)
    return pl.pallas_call(
        kern,
        out_shape=jax.ShapeDtypeStruct((n_b, seq, ML_H * HEAD), F32),
        grid=(n_b, ML_H),
        in_specs=[pl.BlockSpec((None, seq, d), lambda b, h: (b, 0, 0)),
                  pl.BlockSpec((None, d, 5 * HEAD), lambda b, h: (h, 0, 0)),
                  pl.BlockSpec((None, 1, HEAD), lambda b, h: (h, 0, 0)),
                  pl.BlockSpec((None, 1, HEAD), lambda b, h: (h, 0, 0))],
        out_specs=pl.BlockSpec((None, seq, HEAD), lambda b, h: (b, 0, h)),
        scratch_shapes=[pltpu.VMEM((seq, HEAD), BF16),
                        pltpu.VMEM((seq, HEAD), BF16),
                        pltpu.VMEM((seq, 2 * HEAD), BF16),
                        pltpu.VMEM((seq, LANES), F32),
                        pltpu.VMEM((2, seq, HEAD), BF16),
                        pltpu.VMEM((2, seq, HEAD), BF16),
                        pltpu.VMEM((2, seq, LANES), F32),
                        pltpu.VMEM((2, seq, LANES), F32),
                        pltpu.VMEM((n_chunks, SUBLANES, chunk), F32),
                        pltpu.VMEM((2, n_chunks, LANES), F32),
                        pltpu.VMEM((n_chunks, HEAD, 2 * HEAD), F32),
                        pltpu.VMEM((n_chunks, HEAD, 2 * HEAD), BF16),
                        pltpu.VMEM((seq, HEAD), F32)],
        compiler_params=pltpu.CompilerParams(dimension_semantics=("parallel", "arbitrary"),
                                             vmem_limit_bytes=VMEM_LIMIT),
        name="mlstm_branch",
    )(u, w, bias, ng)


_SSD_W_Z = 0
_SSD_W_XBC = SSD_GW
_SSD_W_DT = SSD_GW + SSD_GW + 2 * SSD_N
_SSD_W_END = _SSD_W_DT + LANES
_SSD_XBC_G = SSD_GW + 2 * SSD_N
_SSD_PAD = SUBLANES
_SSD_PAIRS = SSD_HG // 2


def _ssd_kernel(u_ref, w_ref, cw_ref, cb_ref, dtb_ref, alog_ref, dsk_ref, ng_ref, o_ref,
                pad_s, xs_s, b_s, c_s, dt_s, a_s, at_s, dtt_s, cbm_s, ea_s, xw_s, dec_s, cur_s, st_s,
                *, seq, chunk, rblk):
    n_chunks = seq // chunk
    n_rblk = seq // rblk

    zero_rows = jnp.zeros((_SSD_PAD, _SSD_XBC_G), F32)
    pad_s[0:_SSD_PAD, :] = zero_rows
    pad_s[_SSD_PAD + seq:2 * _SSD_PAD + seq, :] = zero_rows
    for i in range(n_rblk):
        r0 = i * rblk
        pad_s[_SSD_PAD + r0:_SSD_PAD + r0 + rblk, :] = _dot(
            u_ref[r0:r0 + rblk, :], w_ref[:, _SSD_W_XBC:_SSD_W_DT])
    for i in range(n_rblk):
        r0 = i * rblk
        acc = cb_ref[...]
        for j in range(SSD_CONV):
            off = _SSD_PAD + r0 + j - SSD_CONV // 2
            acc = acc + pad_s[off:off + rblk, :] * cw_ref[j:j + 1, :]
        xbc = _silu(acc)
        xs_s[r0:r0 + rblk, :] = xbc[:, 0:SSD_GW]
        b_s[r0:r0 + rblk, :] = xbc[:, SSD_GW:SSD_GW + SSD_N].astype(BF16)
        c_s[r0:r0 + rblk, :] = xbc[:, SSD_GW + SSD_N:SSD_GW + 2 * SSD_N].astype(BF16)

    dt = _softplus(_dot(u_ref[...], w_ref[:, _SSD_W_DT:_SSD_W_END]) + dtb_ref[...])
    dt_s[...] = dt
    log_a = dt * (-jnp.exp(alog_ref[...]))
    lane = lax.broadcasted_iota(jnp.int32, (seq, LANES), 1)
    acum = jnp.where(lane < SSD_HG,
                     _chunk_scan(log_a, chunk, False, "add"),
                     _chunk_scan(log_a, chunk, True, "add"))
    a_s[...] = acum
    acum_t = acum.T
    dt_t = dt.T
    for c in range(n_chunks):
        at_s[c] = acum_t[0:2 * SSD_HG, c * chunk:(c + 1) * chunk]
        dtt_s[c] = dt_t[0:2 * SSD_HG, c * chunk:(c + 1) * chunk]
        r0 = c * chunk
        cbm_s[r0:r0 + chunk, :] = _dot_nt(c_s[r0:r0 + chunk, :], b_s[r0:r0 + chunk, :])

    fwd_lane_c = lax.broadcasted_iota(jnp.int32, (n_chunks, LANES), 1) < SSD_HG
    a_end = jnp.where(fwd_lane_c, a_s[pl.ds(chunk - 1, n_chunks, stride=chunk), :],
                      a_s[pl.ds(0, n_chunks, stride=chunk), :])
    lane_lo = lax.broadcasted_iota(jnp.int32, (chunk, LANES), 1) < SSD_HEAD_DIM

    def split3(x):
        hi = x.astype(BF16)
        r1 = x - hi.astype(F32)
        mid = r1.astype(BF16)
        return [hi, mid, (r1 - mid.astype(F32)).astype(BF16)]

    for d in range(2):
        reverse = d == 1
        mask = _tri_mask(chunk, reverse)
        spread = (lax.broadcasted_iota(jnp.int32, (LANES, SSD_GW), 1) // SSD_HEAD_DIM
                  == lax.broadcasted_iota(jnp.int32, (LANES, SSD_GW), 0) - d * SSD_HG).astype(BF16)
        spread2 = jnp.concatenate([spread, spread], axis=0)
        dec_s[...] = _dot(jnp.concatenate(split3(jnp.exp(a_end)), axis=1),
                          jnp.concatenate([spread2, spread], axis=0))
        for i in range(n_rblk):
            r0 = i * rblk
            rows = slice(r0, r0 + rblk)
            a_blk = a_s[rows, :]
            end_tok = jnp.concatenate(
                [jnp.broadcast_to(a_end[c:c + 1, :], (chunk, LANES))
                 for c in range(r0 // chunk, (r0 + rblk) // chunk)], axis=0)
            ea_s[rows, :] = _dot(jnp.exp(a_blk).astype(BF16), spread).astype(BF16)
            wt = split3(dt_s[rows, :] * jnp.exp(end_tok - a_blk))
            xw_s[rows, :] = (xs_s[rows, :] * _dot(jnp.concatenate(wt[0:2], axis=1), spread2)).astype(BF16)

        def chunk_rows(c):
            return pl.ds(pl.multiple_of(c * chunk, chunk), chunk)

        for c in range(n_chunks):
            rows = slice(c * chunk, (c + 1) * chunk)
            pad_s[rows, 0:SSD_GW] = _dot_tn(b_s[rows, :], xw_s[rows, :])

        cur_s[...] = jnp.zeros_like(cur_s)

        def rec_body(ci, carry, reverse=reverse):
            c = (n_chunks - 1 - ci) if reverse else ci
            cur = cur_s[...]
            st_s[c] = cur.astype(BF16)
            cur_s[...] = cur * dec_s[pl.ds(c, 1), :] + pad_s[chunk_rows(c), 0:SSD_GW]
            return carry

        lax.fori_loop(0, n_chunks, rec_body, 0)

        def out_body(c, carry, mask=mask, d=d):
            rows = chunk_rows(c)
            blk = a_s[rows, :]
            blk_t = at_s[c]
            dtt = dtt_s[c]
            cbm = cbm_s[rows, :]
            y_inter = _dot(c_s[rows, :], st_s[c])
            for p in range(_SSD_PAIRS):
                cols = slice(p * LANES, (p + 1) * LANES)
                xs_pair = xs_s[rows, cols]
                m_tiles, x_tiles = [], []
                for hh in range(2):
                    col = d * SSD_HG + 2 * p + hh
                    seg = jnp.exp(jnp.where(mask, blk[:, col:col + 1] - blk_t[col:col + 1, :], NEG))
                    m_tiles.append((cbm * seg * dtt[col:col + 1, :]).astype(BF16))
                    keep = lane_lo if hh == 0 else jnp.logical_not(lane_lo)
                    x_tiles.append(jnp.where(keep, xs_pair, 0.0).astype(BF16))
                y = (y_inter[:, cols] * ea_s[rows, cols].astype(F32)
                     + _dot(jnp.concatenate(m_tiles, axis=1), jnp.concatenate(x_tiles, axis=0)))
                if d == 0:
                    o_ref[rows, cols] = y
                else:
                    o_ref[rows, cols] = o_ref[rows, cols] + y
            return carry

        lax.fori_loop(0, n_chunks, out_body, 0, unroll=SSD_UNROLL)

    for i in range(n_rblk):
        r0 = i * rblk
        rows = slice(r0, r0 + rblk)
        z = _dot(u_ref[rows, :], w_ref[:, _SSD_W_Z:_SSD_W_XBC])
        y = (o_ref[rows, :] + xs_s[rows, :] * dsk_ref[...]) * _silu(z)
        o_ref[rows, :] = y * lax.rsqrt(jnp.mean(y * y, axis=-1, keepdims=True) + EPS) * ng_ref[...]


def _ssd_call(u, w, cw, cb, dtb, alog, dsk, ng, *, chunk=SSD_CHUNK):
    n_b, seq, d = u.shape
    rblk = min(seq, 512)
    kern = functools.partial(_ssd_kernel, seq=seq, chunk=chunk, rblk=rblk)
    vec = lambda n: pl.BlockSpec((None, 1, n), lambda b, g: (g, 0, 0))
    return pl.pallas_call(
        kern,
        out_shape=jax.ShapeDtypeStruct((n_b, seq, SSD_G * SSD_GW), F32),
        grid=(n_b, SSD_G),
        in_specs=[pl.BlockSpec((None, seq, d), lambda b, g: (b, 0, 0)),
                  pl.BlockSpec((None, d, _SSD_W_END), lambda b, g: (g, 0, 0)),
                  pl.BlockSpec((None, SSD_CONV, _SSD_XBC_G), lambda b, g: (g, 0, 0)),
                  vec(_SSD_XBC_G), vec(LANES), vec(LANES), vec(SSD_GW), vec(SSD_GW)],
        out_specs=pl.BlockSpec((None, seq, SSD_GW), lambda b, g: (b, 0, g)),
        scratch_shapes=[pltpu.VMEM((seq + 2 * _SSD_PAD, _SSD_XBC_G), F32),
                        pltpu.VMEM((seq, SSD_GW), F32),
                        pltpu.VMEM((seq, SSD_N), BF16),
                        pltpu.VMEM((seq, SSD_N), BF16),
                        pltpu.VMEM((seq, LANES), F32),
                        pltpu.VMEM((seq, LANES), F32),
                        pltpu.VMEM((seq // chunk, 2 * SSD_HG, chunk), F32),
                        pltpu.VMEM((seq // chunk, 2 * SSD_HG, chunk), F32),
                        pltpu.VMEM((seq, chunk), F32),
                        pltpu.VMEM((seq, SSD_GW), BF16),
                        pltpu.VMEM((seq, SSD_GW), BF16),
                        pltpu.VMEM((seq // chunk, SSD_GW), F32),
                        pltpu.VMEM((SSD_N, SSD_GW), F32),
                        pltpu.VMEM((seq // chunk, SSD_N, SSD_GW), BF16)],
        compiler_params=pltpu.CompilerParams(dimension_semantics=("parallel", "arbitrary"),
                                             vmem_limit_bytes=VMEM_LIMIT),
        name="ssd_branch",
    )(u, w, cw, cb, dtb, alog, dsk, ng)


def _merge_kernel(u_ref, h_ref, y0_ref, y1_ref, y2_ref, wg_ref, wb_ref, wo_ref, o_ref):
    u = u_ref[...]
    merged = None
    for r, y_ref in enumerate((y0_ref, y1_ref, y2_ref)):
        term = _sigmoid(_dot(u, wg_ref[r])) * _dot(y_ref[...].astype(BF16), wb_ref[r])
        merged = term if merged is None else merged + term
    o_ref[...] = h_ref[...] + _dot(merged.astype(BF16), wo_ref[...])


def _merge_call(u2d, h2d, y0, y1, y2, wg, wb, wo, tm=512):
    n_tok, d = h2d.shape
    tm = min(tm, n_tok)
    tok = lambda: pl.BlockSpec((tm, d), lambda i: (i, 0))
    held = lambda shape: pl.BlockSpec(shape, lambda i: tuple(0 for _ in shape), pipeline_mode=pl.Buffered(1))
    return pl.pallas_call(
        _merge_kernel,
        out_shape=jax.ShapeDtypeStruct((n_tok, d), F32),
        grid=(n_tok // tm,),
        in_specs=[tok(), tok(), tok(), tok(), tok(),
                  held((3, d, d)), held((3, d, d)), held((d, d))],
        out_specs=tok(),
        compiler_params=pltpu.CompilerParams(dimension_semantics=("parallel",),
                                             vmem_limit_bytes=VMEM_LIMIT),
        name="merge",
    )(u2d, h2d, y0, y1, y2, wg, wb, wo)


_FFN_CW = 256
_FFN_NJ = D_FF // _FFN_CW
_FFN_HALO = SUBLANES

def _ffn_kernel(h_ref, hp_ref, hn_ref, p_ref, g_ref, wa_ref, wv_ref, cwa_ref, cwv_ref,
                cba_ref, cbv_ref, wd_ref, wple_ref, wpg_ref, fg_ref, o_ref, u_s,
                *, tm, n_tiles, final_norm):
    i = pl.program_id(1)
    j = pl.program_id(2)

    def norm(x):
        return (x * lax.rsqrt(jnp.mean(x * x, axis=-1, keepdims=True) + EPS) * g_ref[...]).astype(BF16)

    @pl.when(j == 0)
    def _():
        up = jnp.where(i > 0, 1.0, 0.0)
        dn = jnp.where(i < n_tiles - 1, 1.0, 0.0)
        u_s[0:_FFN_HALO, :] = norm(hp_ref[...] * up)
        u_s[_FFN_HALO:_FFN_HALO + tm, :] = norm(h_ref[...])
        u_s[_FFN_HALO + tm:2 * _FFN_HALO + tm, :] = norm(hn_ref[...] * dn)
        o_ref[...] = h_ref[...]

    def conv(w_ref, cw_ref, cb_ref):
        up = _dot(u_s[...], w_ref[...])
        acc = cb_ref[...]
        for t in range(FFN_CONV):
            off = _FFN_HALO + t - FFN_CONV // 2
            acc = acc + up[off:off + tm, :] * cw_ref[t:t + 1, :]
        return acc

    act = (_silu(conv(wa_ref, cwa_ref, cba_ref)) * conv(wv_ref, cwv_ref, cbv_ref)).astype(BF16)
    o_ref[...] += _dot(act, wd_ref[...])

    @pl.when(j == _FFN_NJ - 1)
    def _():
        h2 = o_ref[...]
        ple = _dot(p_ref[...].astype(BF16), wple_ref[...])
        h3 = h2 + ple * _sigmoid(_dot(h2.astype(BF16), wpg_ref[...]))
        if final_norm:
            h3 = h3 * lax.rsqrt(jnp.mean(h3 * h3, axis=-1, keepdims=True) + EPS) * fg_ref[...]
        o_ref[...] = h3


def _ffn_call(h, p, g, w_up, cw, cb, w_down, w_ple, w_pg, fg, *, final_norm, tm=1024):
    n_b, seq, d = h.shape
    tm = min(tm, seq)
    n_tiles = seq // tm
    hb = tm // _FFN_HALO
    n_hb = seq // _FFN_HALO
    kern = functools.partial(_ffn_kernel, tm=tm, n_tiles=n_tiles, final_norm=final_norm)
    cst = lambda shape: pl.BlockSpec(shape, lambda b, i, j: tuple(0 for _ in shape))
    return pl.pallas_call(
        kern,
        out_shape=jax.ShapeDtypeStruct((n_b, seq, d), F32),
        grid=(n_b, n_tiles, _FFN_NJ),
        in_specs=[pl.BlockSpec((None, tm, d), lambda b, i, j: (b, i, 0)),
                  pl.BlockSpec((None, _FFN_HALO, d), lambda b, i, j: (b, jnp.maximum(i * hb - 1, 0), 0)),
                  pl.BlockSpec((None, _FFN_HALO, d), lambda b, i, j: (b, jnp.minimum((i + 1) * hb, n_hb - 1), 0)),
                  pl.BlockSpec((None, tm, PLE_DIM), lambda b, i, j: (b, i, 0)),
                  cst((1, d)),
                  pl.BlockSpec((d, _FFN_CW), lambda b, i, j: (0, j)),
                  pl.BlockSpec((d, _FFN_CW), lambda b, i, j: (0, _FFN_NJ + j)),
                  pl.BlockSpec((FFN_CONV, _FFN_CW), lambda b, i, j: (0, j)),
                  pl.BlockSpec((FFN_CONV, _FFN_CW), lambda b, i, j: (0, _FFN_NJ + j)),
                  pl.BlockSpec((1, _FFN_CW), lambda b, i, j: (0, j)),
                  pl.BlockSpec((1, _FFN_CW), lambda b, i, j: (0, _FFN_NJ + j)),
                  pl.BlockSpec((_FFN_CW, d), lambda b, i, j: (j, 0)),
                  cst((PLE_DIM, d)), cst((d, d)), cst((1, d))],
        out_specs=pl.BlockSpec((None, tm, d), lambda b, i, j: (b, i, 0)),
        scratch_shapes=[pltpu.VMEM((tm + 2 * _FFN_HALO, d), BF16)],
        compiler_params=pltpu.CompilerParams(
            dimension_semantics=("parallel", "parallel", "arbitrary"),
            vmem_limit_bytes=VMEM_LIMIT),
        name="convffn_ple",
    )(h, h, h, p, g.reshape(1, d), w_up, w_up, cw, cw, cb.reshape(1, -1), cb.reshape(1, -1),
      w_down, w_ple, w_pg, fg.reshape(1, d))


def _cols(w, start, width):
    return lax.slice_in_dim(w, start, start + width, axis=1)


def _pad_cols(w, width):
    return jnp.pad(w, ((0, 0), (0, width - w.shape[1])))


def _ssd_params(w_in, conv_w, conv_b, dt_bias, a_log, d_skip, norm_g):
    ws, cws, cbs, dtbs, alogs = [], [], [], [], []
    for g in range(SSD_G):
        dt_cols = jnp.concatenate(
            [_cols(w_in, _OFF_DT + dd * SSD_H + g * SSD_HG, SSD_HG) for dd in range(2)], axis=1)
        ws.append(jnp.concatenate([
            _cols(w_in, _OFF_Z + g * SSD_GW, SSD_GW),
            _cols(w_in, _OFF_XBC + g * SSD_GW, SSD_GW),
            _cols(w_in, _OFF_XBC + D_MODEL + g * SSD_N, SSD_N),
            _cols(w_in, _OFF_XBC + D_MODEL + SSD_G * SSD_N + g * SSD_N, SSD_N),
            _pad_cols(dt_cols, LANES)], axis=1))
        pick = lambda a: jnp.concatenate([
            _cols(a, g * SSD_GW, SSD_GW),
            _cols(a, D_MODEL + g * SSD_N, SSD_N),
            _cols(a, D_MODEL + SSD_G * SSD_N + g * SSD_N, SSD_N)], axis=1)
        cws.append(pick(conv_w))
        cbs.append(pick(conv_b.reshape(1, -1)))
        head_row = lambda a: _pad_cols(
            jnp.concatenate([a[dd, g * SSD_HG:(g + 1) * SSD_HG] for dd in range(2)]).reshape(1, -1), LANES)
        dtbs.append(head_row(dt_bias))
        alogs.append(head_row(a_log))
    dsk = jnp.repeat(d_skip, SSD_HEAD_DIM).reshape(SSD_G, 1, SSD_GW)
    return (jnp.stack(ws).astype(BF16), jnp.stack(cws), jnp.stack(cbs), jnp.stack(dtbs),
            jnp.stack(alogs), dsk, norm_g.reshape(SSD_G, 1, SSD_GW))


def _hgrn2_params(w_in, lb, norm_g):
    ws = [jnp.concatenate([
        _cols(w_in, _OFF_GQ + h * HEAD, HEAD),
        _cols(w_in, _OFF_GF + h * HEAD, HEAD),
        _cols(w_in, _OFF_GF + D_MODEL + h * HEAD, HEAD),
        _cols(w_in, _OFF_GI + h * HEAD, HEAD),
        _cols(w_in, _OFF_GG + h * HEAD, HEAD)], axis=1) for h in range(HG_H)]
    lbs = lb.reshape(2, HG_H, HEAD).swapaxes(0, 1)
    return jnp.stack(ws).astype(BF16), lbs, norm_g.reshape(HG_H, 1, HEAD)


def _mlstm_params(w_in, i_bias, f_bias, norm_g):
    ws, biases = [], []
    for h in range(ML_H):
        gate_cols = jnp.concatenate([
            _cols(w_in, _OFF_MI + h, 1), _cols(w_in, _OFF_MI + ML_H + h, 1),
            _cols(w_in, _OFF_MF + h, 1), _cols(w_in, _OFF_MF + ML_H + h, 1)], axis=1)
        ws.append(jnp.concatenate([
            _cols(w_in, _OFF_MQ + h * HEAD, HEAD),
            _cols(w_in, _OFF_MK + h * HEAD, HEAD),
            _cols(w_in, _OFF_MV + h * HEAD, HEAD),
            _cols(w_in, _OFF_MO + h * HEAD, HEAD),
            _pad_cols(gate_cols, LANES)], axis=1))
        biases.append(_pad_cols(
            jnp.stack([i_bias[0, h], i_bias[1, h], f_bias[0, h], f_bias[1, h]]).reshape(1, 4), LANES))
    return jnp.stack(ws).astype(BF16), jnp.stack(biases), norm_g.reshape(ML_H, 1, HEAD)


def kernel(x, p, norm_mix_g, w_in, ssd_conv_w, ssd_conv_b, ssd_dt_bias, ssd_a_log, ssd_d, ssd_norm_g, hg_lb_raw, hg_norm_g, ml_i_bias, ml_f_bias, ml_norm_g, w_br_ssd, w_br_hg, w_br_ml, w_out, norm_ffn_g, w_up, ffn_conv_w, ffn_conv_b, w_down, w_ple, w_ple_gate, final_norm_g):
    n_b, seq, d = x.shape
    depth = w_in.shape[0]
    lb_soft = jax.nn.softmax(hg_lb_raw.astype(F32), axis=0)
    hg_lb = jnp.cumsum(lb_soft, axis=0) - lb_soft[0:1]
    h = x
    for l in range(depth):
        u2d = _rmsnorm_call(h.reshape(n_b * seq, d), norm_mix_g[l], BF16)
        u = u2d.reshape(n_b, seq, d)
        y_ssd = _ssd_call(u, *_ssd_params(w_in[l], ssd_conv_w[l], ssd_conv_b[l], ssd_dt_bias[l],
                                          ssd_a_log[l], ssd_d[l], ssd_norm_g[l]))
        y_hg = _hgrn2_call(u, *_hgrn2_params(w_in[l], hg_lb[l], hg_norm_g[l]))
        y_ml = _mlstm_call(u, *_mlstm_params(w_in[l], ml_i_bias[l], ml_f_bias[l], ml_norm_g[l]))
        wg = jnp.stack([_cols(w_in[l], _OFF_GATES + r * d, d) for r in range(3)]).astype(BF16)
        wb = jnp.stack([w_br_ssd[l], w_br_hg[l], w_br_ml[l]]).astype(BF16)
        tok = lambda a: a.reshape(n_b * seq, d)
        h = _merge_call(u2d, tok(h), tok(y_ssd), tok(y_hg), tok(y_ml), wg, wb,
                        w_out[l].astype(BF16)).reshape(n_b, seq, d)
        h = _ffn_call(h, p[l], norm_ffn_g[l], w_up[l].astype(BF16), ffn_conv_w[l], ffn_conv_b[l],
                      w_down[l].astype(BF16), w_ple[l].astype(BF16), w_ple_gate[l].astype(BF16),
                      final_norm_g, final_norm=(l == depth - 1))
    return h
```

```python
import functools

import jax
import jax.numpy as jnp
from jax import lax
from jax.experimental import pallas as pl
from jax.experimental.pallas import tpu as pltpu

F32 = jnp.float32
BF16 = jnp.bfloat16

D_MODEL = 1024
PLE_DIM = 256
EPS = 1e-6
NEG = -1e30
TINY = 1e-30

SSD_HEAD_DIM = 64
SSD_H = 16
SSD_G = 2
SSD_HG = 8
SSD_N = 128
SSD_CONV = 5
SSD_GW = SSD_HG * SSD_HEAD_DIM
SSD_XBC = D_MODEL + 2 * SSD_G * SSD_N
HG_H = 8
ML_H = 8
HEAD = 128
D_FF = 2816
FFN_CONV = 3

LANES = 128
SUBLANES = 8
VMEM_LIMIT = 56 * 1024 * 1024

SSD_CHUNK = 128
HG_CHUNK = 64
ML_CHUNK = 128
SSD_UNROLL = 2
ML_UNROLL = 8
HG_UNROLL = 16
HG_SAFE_DECAY = 60.0

_OFF_Z = 0
_OFF_XBC = 1024
_OFF_DT = _OFF_XBC + SSD_XBC
_OFF_GQ = _OFF_DT + 2 * SSD_H
_OFF_GF = _OFF_GQ + 1024
_OFF_GI = _OFF_GF + 2048
_OFF_GG = _OFF_GI + 1024
_OFF_MQ = _OFF_GG + 1024
_OFF_MK = _OFF_MQ + 1024
_OFF_MV = _OFF_MK + 1024
_OFF_MI = _OFF_MV + 1024
_OFF_MF = _OFF_MI + 2 * ML_H
_OFF_MO = _OFF_MF + 2 * ML_H
_OFF_GATES = _OFF_MO + 1024


def _dot(a, b):
    return jnp.dot(a, b, preferred_element_type=F32)


def _dot_nt(a, b):
    return lax.dot_general(a, b, (((1,), (1,)), ((), ())), preferred_element_type=F32)


def _dot_tn(a, b):
    return lax.dot_general(a, b, (((0,), (0,)), ((), ())), preferred_element_type=F32)


def _sigmoid(x):
    return 1.0 / (1.0 + jnp.exp(-x))


def _silu(x):
    return x * _sigmoid(x)


def _softplus(x):
    return jnp.maximum(x, 0.0) + jnp.log1p(jnp.exp(-jnp.abs(x)))


def _chunk_scan(x, chunk, reverse, op):
    n_rows = x.shape[0]
    row = lax.broadcasted_iota(jnp.int32, x.shape, 0) & (chunk - 1)
    sh = 1
    while sh < chunk:
        if reverse:
            shifted = pltpu.roll(x, n_rows - sh, axis=0)
            ok = row < chunk - sh
        else:
            shifted = pltpu.roll(x, sh, axis=0)
            ok = row >= sh
        if op == "add":
            x = x + jnp.where(ok, shifted, 0.0)
        else:
            x = jnp.maximum(x, jnp.where(ok, shifted, NEG))
        sh *= 2
    return x


def _lane_scan(x, seg, reverse, op, start=1):
    n_lanes = x.shape[1]
    pos = lax.broadcasted_iota(jnp.int32, x.shape, 1) & (seg - 1)
    sh = start
    while sh < seg:
        if reverse:
            shifted = pltpu.roll(x, n_lanes - sh, axis=1)
            ok = pos < seg - sh
        else:
            shifted = pltpu.roll(x, sh, axis=1)
            ok = pos >= sh
        if op == "add":
            x = x + jnp.where(ok, shifted, 0.0)
        else:
            x = jnp.maximum(x, jnp.where(ok, shifted, NEG))
        sh *= 2
    return x


def _tri_mask(n, reverse):
    t = lax.broadcasted_iota(jnp.int32, (n, n), 0)
    s = lax.broadcasted_iota(jnp.int32, (n, n), 1)
    return (s >= t) if reverse else (s <= t)


def _rmsnorm_kernel(h_ref, g_ref, o_ref):
    x = h_ref[...]
    ms = jnp.mean(x * x, axis=-1, keepdims=True)
    o_ref[...] = (x * lax.rsqrt(ms + EPS) * g_ref[...]).astype(o_ref.dtype)


def _rmsnorm_call(h2d, g, out_dtype, tm=1024):
    n_tok, d = h2d.shape
    tm = min(tm, n_tok)
    return pl.pallas_call(
        _rmsnorm_kernel,
        out_shape=jax.ShapeDtypeStruct((n_tok, d), out_dtype),
        grid=(n_tok // tm,),
        in_specs=[pl.BlockSpec((tm, d), lambda i: (i, 0)),
                  pl.BlockSpec((1, d), lambda i: (0, 0))],
        out_specs=pl.BlockSpec((tm, d), lambda i: (i, 0)),
        compiler_params=pltpu.CompilerParams(dimension_semantics=("parallel",),
                                             vmem_limit_bytes=VMEM_LIMIT),
        name="rmsnorm",
    )(h2d, g.reshape(1, d))


def _hg_att_safe(qc, kc, lfc, reverse):
    n = qc.shape[0]
    t = lax.broadcasted_iota(jnp.int32, (n, n), 0)
    s = lax.broadcasted_iota(jnp.int32, (n, n), 1)
    att = jnp.where(t == s, jnp.sum(qc * kc, axis=-1, keepdims=True), 0.0)
    half = 1
    while half < n:
        pre = _chunk_scan(lfc, half, False, "add")
        suf = _chunk_scan(lfc, half, True, "add")
        t_hi = (t & half) != 0
        s_hi = (s & half) != 0
        same_block = (t ^ s) < 2 * half
        if reverse:
            qe = qc * jnp.exp(suf)
            ke = kc * jnp.exp(pre - lfc)
            pair = jnp.logical_and(same_block, jnp.logical_and(jnp.logical_not(t_hi), s_hi))
        else:
            qe = qc * jnp.exp(pre)
            ke = kc * jnp.exp(suf - lfc)
            pair = jnp.logical_and(same_block, jnp.logical_and(t_hi, jnp.logical_not(s_hi)))
        att = att + jnp.where(pair, _dot_nt(qe.astype(BF16), ke.astype(BF16)), 0.0)
        half *= 2
    return att


def _hgrn2_kernel(u_ref, w_ref, lb_ref, ng_ref, o_ref,
                  q_s, v_s, k_s, lf_s, b_s, qe_s, ke_s, qi_s, kd_s, dec_s, upd_s, st_s, acc_s,
                  *, seq, chunk):
    n_chunks = seq // chunk
    proj = _dot(u_ref[...], w_ref[...])
    q_s[...] = _silu(proj[:, 0:HEAD])
    v_s[...] = proj[:, 3 * HEAD:4 * HEAD].astype(BF16)
    o_ref[...] = proj[:, 4 * HEAD:5 * HEAD]

    for d in range(2):
        f_raw = proj[:, (1 + d) * HEAD:(2 + d) * HEAD]
        lb = lb_ref[d:d + 1, :]
        sig_f = _sigmoid(f_raw)
        f = lb + (1.0 - lb) * sig_f
        k_s[d] = (1.0 - lb) * (1.0 - sig_f)
        log_f = jnp.log(jnp.maximum(f, TINY))
        lf_s[d] = log_f
        b = _chunk_scan(log_f, chunk, d == 1, "add")
        b_s[d] = b

    def ref_span(d):
        first, mid, last = (b_s[d, pl.ds(r, n_chunks, stride=chunk), :] for r in (0, chunk // 2, chunk - 1))
        return jnp.maximum(jnp.abs(first - mid), jnp.abs(mid - last))

    mild_decay = jnp.max(jnp.maximum(ref_span(0), ref_span(1))) <= HG_SAFE_DECAY

    for d in range(2):
        reverse = d == 1
        mask = _tri_mask(chunk, reverse)
        end_row = 0 if reverse else chunk - 1

        def chunk_rows(c):
            r0 = pl.multiple_of(c * chunk, chunk)
            return r0, pl.ds(r0, chunk)

        def scale_body(c, carry, d=d, end_row=end_row):
            r0, rows = chunk_rows(c)
            qc = q_s[rows, :]
            kc = k_s[d, rows, :]
            bc = b_s[d, rows, :]
            ref = b_s[d, pl.ds(r0 + chunk // 2, 1), :]
            bend = b_s[d, pl.ds(r0 + end_row, 1), :]
            qe_s[rows, :] = (qc * jnp.exp(bc - ref)).astype(BF16)
            ke_s[rows, :] = (kc * jnp.exp(ref - bc)).astype(BF16)
            qi_s[rows, :] = (qc * jnp.exp(bc)).astype(BF16)
            kd_s[rows, :] = (kc * jnp.exp(bend - bc)).astype(BF16)
            dec_s[c] = jnp.broadcast_to(jnp.exp(bend), (HEAD, HEAD)).T
            return carry

        lax.fori_loop(0, n_chunks, scale_body, 0, unroll=HG_UNROLL)

        for c in range(n_chunks):
            rows = slice(c * chunk, (c + 1) * chunk)
            upd_s[c] = _dot_tn(kd_s[rows, :], v_s[rows, :])

        def rec_body(ci, st, reverse=reverse):
            c = (n_chunks - 1 - ci) if reverse else ci
            st_s[c] = st.astype(BF16)
            return st * dec_s[c] + upd_s[c]

        lax.fori_loop(0, n_chunks, rec_body, jnp.zeros((HEAD, HEAD), F32))

        def emit(c, rows, att, d=d):
            o = _dot(jnp.concatenate([qi_s[rows, :], att.astype(BF16)], axis=1),
                     jnp.concatenate([st_s[c], v_s[rows, :]], axis=0))
            if d == 0:
                acc_s[rows, :] = o
            else:
                acc_s[rows, :] = acc_s[rows, :] + o

        @pl.when(mild_decay)
        def _(mask=mask):
            for g0 in range(0, n_chunks, HG_UNROLL):
                group = [(c, slice(c * chunk, (c + 1) * chunk)) for c in range(g0, min(g0 + HG_UNROLL, n_chunks))]
                atts = [jnp.where(mask, _dot_nt(qe_s[rows, :], ke_s[rows, :]), 0.0) for _, rows in group]
                for (c, rows), att in zip(group, atts):
                    emit(c, rows, att)

        @pl.when(jnp.logical_not(mild_decay))
        def _(d=d, reverse=reverse):
            def safe_body(c, carry):
                _, rows = chunk_rows(c)
                emit(c, rows, _hg_att_safe(q_s[rows, :], k_s[d, rows, :], lf_s[d, rows, :], reverse))
                return carry

            lax.fori_loop(0, n_chunks, safe_body, 0)

    o = acc_s[...]
    o = o * lax.rsqrt(jnp.mean(o * o, axis=-1, keepdims=True) + EPS) * ng_ref[...]
    o_ref[...] = o * _silu(o_ref[...])


def _hgrn2_call(u, w, lb, ng, *, chunk=HG_CHUNK):
    n_b, seq, d = u.shape
    kern = functools.partial(_hgrn2_kernel, seq=seq, chunk=chunk)
    return pl.pallas_call(
        kern,
        out_shape=jax.ShapeDtypeStruct((n_b, seq, HG_H * HEAD), F32),
        grid=(n_b, HG_H),
        in_specs=[pl.BlockSpec((None, seq, d), lambda b, h: (b, 0, 0)),
                  pl.BlockSpec((None, d, 5 * HEAD), lambda b, h: (h, 0, 0)),
                  pl.BlockSpec((None, 2, HEAD), lambda b, h: (h, 0, 0)),
                  pl.BlockSpec((None, 1, HEAD), lambda b, h: (h, 0, 0))],
        out_specs=pl.BlockSpec((None, seq, HEAD), lambda b, h: (b, 0, h)),
        scratch_shapes=[pltpu.VMEM((seq, HEAD), F32),
                        pltpu.VMEM((seq, HEAD), BF16),
                        pltpu.VMEM((2, seq, HEAD), F32),
                        pltpu.VMEM((2, seq, HEAD), F32),
                        pltpu.VMEM((2, seq, HEAD), F32),
                        pltpu.VMEM((seq, HEAD), BF16),
                        pltpu.VMEM((seq, HEAD), BF16),
                        pltpu.VMEM((seq, HEAD), BF16),
                        pltpu.VMEM((seq, HEAD), BF16),
                        pltpu.VMEM((seq // chunk, HEAD, HEAD), F32),
                        pltpu.VMEM((seq // chunk, HEAD, HEAD), F32),
                        pltpu.VMEM((seq // chunk, HEAD, HEAD), BF16),
                        pltpu.VMEM((seq, HEAD), F32)],
        compiler_params=pltpu.CompilerParams(dimension_semantics=("parallel", "arbitrary"),
                                             vmem_limit_bytes=VMEM_LIMIT),
        name="hgrn2_branch",
    )(u, w, lb, ng)


def _log_sigmoid(x):
    return jnp.minimum(x, 0.0) - jnp.log1p(jnp.exp(-jnp.abs(x)))


def _mlstm_kernel(u_ref, w_ref, bias_ref, ng_ref, o_ref,
                  qb_s, kb_s, va_s, tm_s, qw_s, kw_s, mx_s, en_s, gt_s, dec_s, upd_s, ct_s,
                  acc_s, *, seq, chunk):
    n_chunks = seq // chunk
    proj = _dot(u_ref[...], w_ref[...])
    q = proj[:, 0:HEAD]
    k = proj[:, HEAD:2 * HEAD] * (HEAD ** -0.5)
    qb_s[...] = q.astype(BF16)
    kb_s[...] = k.astype(BF16)
    va_s[:, 0:HEAD] = proj[:, 2 * HEAD:3 * HEAD].astype(BF16)
    va_s[:, HEAD:2 * HEAD] = jnp.ones((seq, HEAD), BF16)
    o_ref[...] = proj[:, 3 * HEAD:4 * HEAD]

    x = (proj[:, 4 * HEAD:5 * HEAD] + bias_ref[...]).T[0:SUBLANES, :]
    fwd_row = lax.broadcasted_iota(jnp.int32, (SUBLANES, seq), 0) == 0
    pos = lax.broadcasted_iota(jnp.int32, (SUBLANES, seq), 1)

    def scans(y, op, seg, start=1):
        return _lane_scan(y, seg, False, op, start), _lane_scan(y, seg, True, op, start)

    it = x
    log_f = pltpu.roll(_log_sigmoid(x), SUBLANES - 2, axis=0)
    pre, suf = scans(log_f, "add", chunk)
    bt = jnp.where(fwd_row, pre, suf)
    bt_end = pre + suf - log_f
    g = it - bt
    g_pre, g_suf = scans(g, "max", chunk)
    cm = jnp.where(fwd_row, g_pre, g_suf)
    cm_end = jnp.maximum(g_pre, g_suf)

    p_pre, p_suf = scans(bt_end, "add", seq, chunk)
    p_sum = jnp.where(fwd_row, p_pre, p_suf)
    x_pre, x_suf = scans(cm_end - (p_sum - bt_end), "max", seq, chunk)
    m_end = p_sum + jnp.where(fwd_row, x_pre, x_suf)
    m_prev = jnp.where(fwd_row,
                       jnp.where(pos < chunk, NEG, pltpu.roll(m_end, chunk, axis=1)),
                       jnp.where(pos >= seq - chunk, NEG, pltpu.roll(m_end, seq - chunk, axis=1)))
    mx = jnp.maximum(m_prev, cm)
    w_inter = jnp.exp(m_prev - mx)
    wk = jnp.exp(bt_end - bt + it - m_end)
    en = jnp.exp(-(bt + mx))
    decay = jnp.exp(bt_end + m_prev - m_end)

    for c in range(n_chunks):
        gt_s[c] = g[:, c * chunk:(c + 1) * chunk]
    row = lax.broadcasted_iota(jnp.int32, (SUBLANES, seq), 0)
    packed = jnp.where(row < 2, w_inter,
                       jnp.where(row < 4, pltpu.roll(wk, 2, axis=0),
                                 jnp.where(row < 6, pltpu.roll(mx, 4, axis=0), pltpu.roll(en, 6, axis=0))))
    tm_s[...] = jnp.concatenate(
        [packed, decay, jnp.zeros((LANES - 2 * SUBLANES, seq), F32)], axis=0).T
    for d in range(2):
        def lanes(col, ref_rows=slice(None)):
            block = tm_s[ref_rows, :]
            return jnp.broadcast_to(block[:, col:col + 1], (block.shape[0], LANES))
        qw_s[d] = (q * lanes(d)).astype(BF16)
        kw_s[d] = (k * lanes(2 + d)).astype(BF16)
        mx_s[d] = lanes(4 + d)
        en_s[d] = lanes(6 + d)
        dec_s[d] = lanes(8 + d, pl.ds(0, n_chunks, stride=chunk))

    for d in range(2):
        reverse = d == 1
        mask = _tri_mask(chunk, reverse)

        def chunk_rows(c):
            return pl.ds(pl.multiple_of(c * chunk, chunk), chunk)

        for c in range(n_chunks):
            rows = slice(c * chunk, (c + 1) * chunk)
            upd_s[c] = _dot_tn(kw_s[d, rows, :], va_s[rows, :])

        def rec_body(ci, ct, reverse=reverse, d=d):
            c = (n_chunks - 1 - ci) if reverse else ci
            ct_s[c] = ct.astype(BF16)
            dec = dec_s[d, pl.ds(c, 1), :]
            return ct * jnp.concatenate([dec, dec], axis=1) + upd_s[c]

        lax.fori_loop(0, n_chunks, rec_body, jnp.zeros((HEAD, 2 * HEAD), F32))

        def scores(c, rows):
            p = jnp.exp(jnp.where(mask, gt_s[c][d:d + 1, :] - mx_s[d, rows, :], NEG))
            return (_dot_nt(qb_s[rows, :], kb_s[rows, :]) * p).astype(BF16)

        def emit(c, rows, sc):
            tot = _dot(jnp.concatenate([qw_s[d, rows, :], sc], axis=1),
                       jnp.concatenate([ct_s[c], va_s[rows, :]], axis=0))
            hval = tot[:, 0:HEAD] / jnp.maximum(jnp.abs(tot[:, HEAD:2 * HEAD]), en_s[d, rows, :])
            if d == 0:
                acc_s[rows, :] = hval
            else:
                acc_s[rows, :] = acc_s[rows, :] + hval

        for g0 in range(0, n_chunks, ML_UNROLL):
            group = [(c, slice(c * chunk, (c + 1) * chunk)) for c in range(g0, min(g0 + ML_UNROLL, n_chunks))]
            staged = [scores(c, rows) for c, rows in group]
            for (c, rows), sc in zip(group, staged):
                emit(c, rows, sc)

    hsum = acc_s[...]
    hc = hsum - jnp.mean(hsum, axis=-1, keepdims=True)
    hn = hc * lax.rsqrt(jnp.mean(hc * hc, axis=-1, keepdims=True) + EPS) * ng_ref[...]
    o_ref[...] = hn * _sigmoid(o_ref[...])


def _mlstm_call(u, w, bias, ng, *, chunk=ML_CHUNK):
    n_b, seq, d = u.shape
    n_chunks = seq // chunk
    kern = functools.partial(_mlstm_kernel, seq=seq, chunk=chunk)
    return pl.pallas_call(
        kern,
        out_shape=jax.ShapeDtypeStruct((n_b, seq, ML_H * HEAD), F32),
        grid=(n_b, ML_H),
        in_specs=[pl.BlockSpec((None, seq, d), lambda b, h: (b, 0, 0)),
                  pl.BlockSpec((None, d, 5 * HEAD), lambda b, h: (h, 0, 0)),
                  pl.BlockSpec((None, 1, HEAD), lambda b, h: (h, 0, 0)),
                  pl.BlockSpec((None, 1, HEAD), lambda b, h: (h, 0, 0))],
        out_specs=pl.BlockSpec((None, seq, HEAD), lambda b, h: (b, 0, h)),
        scratch_shapes=[pltpu.VMEM((seq, HEAD), BF16),
                        pltpu.VMEM((seq, HEAD), BF16),
                        pltpu.VMEM((seq, 2 * HEAD), BF16),
                        pltpu.VMEM((seq, LANES), F32),
                        pltpu.VMEM((2, seq, HEAD), BF16),
                        pltpu.VMEM((2, seq, HEAD), BF16),
                        pltpu.VMEM((2, seq, LANES), F32),
                        pltpu.VMEM((2, seq, LANES), F32),
                        pltpu.VMEM((n_chunks, SUBLANES, chunk), F32),
                        pltpu.VMEM((2, n_chunks, LANES), F32),
                        pltpu.VMEM((n_chunks, HEAD, 2 * HEAD), F32),
                        pltpu.VMEM((n_chunks, HEAD, 2 * HEAD), BF16),
                        pltpu.VMEM((seq, HEAD), F32)],
        compiler_params=pltpu.CompilerParams(dimension_semantics=("parallel", "arbitrary"),
                                             vmem_limit_bytes=VMEM_LIMIT),
        name="mlstm_branch",
    )(u, w, bias, ng)


_SSD_W_Z = 0
_SSD_W_XBC = SSD_GW
_SSD_W_DT = SSD_GW + SSD_GW + 2 * SSD_N
_SSD_W_END = _SSD_W_DT + LANES
_SSD_XBC_G = SSD_GW + 2 * SSD_N
_SSD_PAD = SUBLANES
_SSD_PAIRS = SSD_HG // 2


def _ssd_kernel(u_ref, w_ref, cw_ref, cb_ref, dtb_ref, alog_ref, dsk_ref, ng_ref, o_ref,
                pad_s, xs_s, b_s, c_s, a_s, at_s, dtt_s, cbm_s, ea_s, xw_s, dec_s, cur_s, st_s,
                *, seq, chunk, rblk):
    n_chunks = seq // chunk
    n_rblk = seq // rblk

    zero_rows = jnp.zeros((_SSD_PAD, _SSD_XBC_G), F32)
    pad_s[0:_SSD_PAD, :] = zero_rows
    pad_s[_SSD_PAD + seq:2 * _SSD_PAD + seq, :] = zero_rows
    for i in range(n_rblk):
        r0 = i * rblk
        pad_s[_SSD_PAD + r0:_SSD_PAD + r0 + rblk, :] = _dot(
            u_ref[r0:r0 + rblk, :], w_ref[:, _SSD_W_XBC:_SSD_W_DT])
    for i in range(n_rblk):
        r0 = i * rblk
        acc = cb_ref[...]
        for j in range(SSD_CONV):
            off = _SSD_PAD + r0 + j - SSD_CONV // 2
            acc = acc + pad_s[off:off + rblk, :] * cw_ref[j:j + 1, :]
        xbc = _silu(acc)
        xs_s[r0:r0 + rblk, :] = xbc[:, 0:SSD_GW]
        b_s[r0:r0 + rblk, :] = xbc[:, SSD_GW:SSD_GW + SSD_N].astype(BF16)
        c_s[r0:r0 + rblk, :] = xbc[:, SSD_GW + SSD_N:SSD_GW + 2 * SSD_N].astype(BF16)

    n_hd = 2 * SSD_HG
    dt_tok = _softplus(_dot(u_ref[...], w_ref[:, _SSD_W_DT:_SSD_W_END]) + dtb_ref[...])
    dt = dt_tok.T[0:n_hd, :]
    log_a = (dt_tok * (-jnp.exp(alog_ref[...]))).T[0:n_hd, :]
    fwd_row = lax.broadcasted_iota(jnp.int32, (n_hd, seq), 0) < SSD_HG
    pre = _lane_scan(log_a, chunk, False, "add")
    suf = _lane_scan(log_a, chunk, True, "add")
    acum = jnp.where(fwd_row, pre, suf)
    a_end = pre + suf - log_a
    for c in range(n_chunks):
        lanes_c = slice(c * chunk, (c + 1) * chunk)
        at_s[c] = acum[:, lanes_c]
        dtt_s[c] = dt[:, lanes_c]
        r0 = c * chunk
        cbm_s[r0:r0 + chunk, :] = _dot_nt(c_s[r0:r0 + chunk, :], b_s[r0:r0 + chunk, :])
    a_s[...] = jnp.concatenate(
        [acum, jnp.exp(acum), dt * jnp.exp(a_end - acum), jnp.exp(a_end),
         jnp.zeros((LANES - 4 * n_hd, seq), F32)], axis=0).T
    lane_lo = lax.broadcasted_iota(jnp.int32, (chunk, LANES), 1) < SSD_HEAD_DIM

    def split3(x):
        hi = x.astype(BF16)
        r1 = x - hi.astype(F32)
        mid = r1.astype(BF16)
        return [hi, mid, (r1 - mid.astype(F32)).astype(BF16)]

    def spread(first_lane, pieces):
        one = (lax.broadcasted_iota(jnp.int32, (LANES, SSD_GW), 1) // SSD_HEAD_DIM
               == lax.broadcasted_iota(jnp.int32, (LANES, SSD_GW), 0) - first_lane).astype(BF16)
        return jnp.concatenate([one] * pieces, axis=0)

    for d in range(2):
        reverse = d == 1
        mask = _tri_mask(chunk, reverse)
        spread_ea = spread(n_hd + d * SSD_HG, 1)
        spread_wt = spread(2 * n_hd + d * SSD_HG, 2)
        dec_s[...] = _dot(jnp.concatenate(split3(a_s[pl.ds(0, n_chunks, stride=chunk), :]), axis=1),
                          spread(3 * n_hd + d * SSD_HG, 3))
        for i in range(n_rblk):
            rows = slice(i * rblk, (i + 1) * rblk)
            hi, mid, _ = split3(a_s[rows, :])
            ea_s[rows, :] = _dot(hi, spread_ea).astype(BF16)
            xw_s[rows, :] = (xs_s[rows, :] * _dot(jnp.concatenate([hi, mid], axis=1), spread_wt)).astype(BF16)

        def chunk_rows(c):
            return pl.ds(pl.multiple_of(c * chunk, chunk), chunk)

        for c in range(n_chunks):
            rows = slice(c * chunk, (c + 1) * chunk)
            pad_s[rows, 0:SSD_GW] = _dot_tn(b_s[rows, :], xw_s[rows, :])

        cur_s[...] = jnp.zeros_like(cur_s)

        def rec_body(ci, carry, reverse=reverse):
            c = (n_chunks - 1 - ci) if reverse else ci
            cur = cur_s[...]
            st_s[c] = cur.astype(BF16)
            cur_s[...] = cur * dec_s[pl.ds(c, 1), :] + pad_s[chunk_rows(c), 0:SSD_GW]
            return carry

        lax.fori_loop(0, n_chunks, rec_body, 0)

        def out_body(c, carry, mask=mask, d=d):
            rows = chunk_rows(c)
            blk = a_s[rows, :]
            blk_t = at_s[c]
            dtt = dtt_s[c]
            cbm = cbm_s[rows, :]
            y_inter = _dot(c_s[rows, :], st_s[c])
            for p in range(_SSD_PAIRS):
                cols = slice(p * LANES, (p + 1) * LANES)
                xs_pair = xs_s[rows, cols]
                m_tiles, x_tiles = [], []
                for hh in range(2):
                    col = d * SSD_HG + 2 * p + hh
                    seg = jnp.exp(jnp.where(mask, blk[:, col:col + 1] - blk_t[col:col + 1, :], NEG))
                    m_tiles.append((cbm * seg * dtt[col:col + 1, :]).astype(BF16))
                    keep = lane_lo if hh == 0 else jnp.logical_not(lane_lo)
                    x_tiles.append(jnp.where(keep, xs_pair, 0.0).astype(BF16))
                y = (y_inter[:, cols] * ea_s[rows, cols].astype(F32)
                     + _dot(jnp.concatenate(m_tiles, axis=1), jnp.concatenate(x_tiles, axis=0)))
                if d == 0:
                    o_ref[rows, cols] = y
                else:
                    o_ref[rows, cols] = o_ref[rows, cols] + y
            return carry

        lax.fori_loop(0, n_chunks, out_body, 0, unroll=SSD_UNROLL)

    for i in range(n_rblk):
        r0 = i * rblk
        rows = slice(r0, r0 + rblk)
        z = _dot(u_ref[rows, :], w_ref[:, _SSD_W_Z:_SSD_W_XBC])
        y = (o_ref[rows, :] + xs_s[rows, :] * dsk_ref[...]) * _silu(z)
        o_ref[rows, :] = y * lax.rsqrt(jnp.mean(y * y, axis=-1, keepdims=True) + EPS) * ng_ref[...]


def _ssd_call(u, w, cw, cb, dtb, alog, dsk, ng, *, chunk=SSD_CHUNK):
    n_b, seq, d = u.shape
    rblk = min(seq, 512)
    kern = functools.partial(_ssd_kernel, seq=seq, chunk=chunk, rblk=rblk)
    vec = lambda n: pl.BlockSpec((None, 1, n), lambda b, g: (g, 0, 0))
    return pl.pallas_call(
        kern,
        out_shape=jax.ShapeDtypeStruct((n_b, seq, SSD_G * SSD_GW), F32),
        grid=(n_b, SSD_G),
        in_specs=[pl.BlockSpec((None, seq, d), lambda b, g: (b, 0, 0)),
                  pl.BlockSpec((None, d, _SSD_W_END), lambda b, g: (g, 0, 0)),
                  pl.BlockSpec((None, SSD_CONV, _SSD_XBC_G), lambda b, g: (g, 0, 0)),
                  vec(_SSD_XBC_G), vec(LANES), vec(LANES), vec(SSD_GW), vec(SSD_GW)],
        out_specs=pl.BlockSpec((None, seq, SSD_GW), lambda b, g: (b, 0, g)),
        scratch_shapes=[pltpu.VMEM((seq + 2 * _SSD_PAD, _SSD_XBC_G), F32),
                        pltpu.VMEM((seq, SSD_GW), F32),
                        pltpu.VMEM((seq, SSD_N), BF16),
                        pltpu.VMEM((seq, SSD_N), BF16),
                        pltpu.VMEM((seq, LANES), F32),
                        pltpu.VMEM((seq // chunk, 2 * SSD_HG, chunk), F32),
                        pltpu.VMEM((seq // chunk, 2 * SSD_HG, chunk), F32),
                        pltpu.VMEM((seq, chunk), F32),
                        pltpu.VMEM((seq, SSD_GW), BF16),
                        pltpu.VMEM((seq, SSD_GW), BF16),
                        pltpu.VMEM((seq // chunk, SSD_GW), F32),
                        pltpu.VMEM((SSD_N, SSD_GW), F32),
                        pltpu.VMEM((seq // chunk, SSD_N, SSD_GW), BF16)],
        compiler_params=pltpu.CompilerParams(dimension_semantics=("parallel", "arbitrary"),
                                             vmem_limit_bytes=VMEM_LIMIT),
        name="ssd_branch",
    )(u, w, cw, cb, dtb, alog, dsk, ng)


def _merge_kernel(u_ref, h_ref, y0_ref, y1_ref, y2_ref, wg_ref, wb_ref, wo_ref, o_ref):
    u = u_ref[...]
    merged = None
    for r, y_ref in enumerate((y0_ref, y1_ref, y2_ref)):
        term = _sigmoid(_dot(u, wg_ref[r])) * _dot(y_ref[...].astype(BF16), wb_ref[r])
        merged = term if merged is None else merged + term
    o_ref[...] = h_ref[...] + _dot(merged.astype(BF16), wo_ref[...])


def _merge_call(u2d, h2d, y0, y1, y2, wg, wb, wo, tm=512):
    n_tok, d = h2d.shape
    tm = min(tm, n_tok)
    tok = lambda: pl.BlockSpec((tm, d), lambda i: (i, 0))
    held = lambda shape: pl.BlockSpec(shape, lambda i: tuple(0 for _ in shape), pipeline_mode=pl.Buffered(1))
    return pl.pallas_call(
        _merge_kernel,
        out_shape=jax.ShapeDtypeStruct((n_tok, d), F32),
        grid=(n_tok // tm,),
        in_specs=[tok(), tok(), tok(), tok(), tok(),
                  held((3, d, d)), held((3, d, d)), held((d, d))],
        out_specs=tok(),
        compiler_params=pltpu.CompilerParams(dimension_semantics=("parallel",),
                                             vmem_limit_bytes=VMEM_LIMIT),
        name="merge",
    )(u2d, h2d, y0, y1, y2, wg, wb, wo)


_FFN_CW = 256
_FFN_NJ = D_FF // _FFN_CW
_FFN_HALO = SUBLANES

def _ffn_kernel(h_ref, hp_ref, hn_ref, p_ref, g_ref, wa_ref, wv_ref, cwa_ref, cwv_ref,
                cba_ref, cbv_ref, wd_ref, wple_ref, wpg_ref, fg_ref, o_ref, u_s,
                *, tm, n_tiles, final_norm):
    i = pl.program_id(1)
    j = pl.program_id(2)

    def norm(x):
        return (x * lax.rsqrt(jnp.mean(x * x, axis=-1, keepdims=True) + EPS) * g_ref[...]).astype(BF16)

    @pl.when(j == 0)
    def _():
        up = jnp.where(i > 0, 1.0, 0.0)
        dn = jnp.where(i < n_tiles - 1, 1.0, 0.0)
        u_s[0:_FFN_HALO, :] = norm(hp_ref[...] * up)
        u_s[_FFN_HALO:_FFN_HALO + tm, :] = norm(h_ref[...])
        u_s[_FFN_HALO + tm:2 * _FFN_HALO + tm, :] = norm(hn_ref[...] * dn)
        o_ref[...] = h_ref[...]

    def conv(w_ref, cw_ref, cb_ref):
        up = _dot(u_s[...], w_ref[...])
        acc = cb_ref[...]
        for t in range(FFN_CONV):
            off = _FFN_HALO + t - FFN_CONV // 2
            acc = acc + up[off:off + tm, :] * cw_ref[t:t + 1, :]
        return acc

    act = (_silu(conv(wa_ref, cwa_ref, cba_ref)) * conv(wv_ref, cwv_ref, cbv_ref)).astype(BF16)
    o_ref[...] += _dot(act, wd_ref[...])

    @pl.when(j == _FFN_NJ - 1)
    def _():
        h2 = o_ref[...]
        ple = _dot(p_ref[...].astype(BF16), wple_ref[...])
        h3 = h2 + ple * _sigmoid(_dot(h2.astype(BF16), wpg_ref[...]))
        if final_norm:
            h3 = h3 * lax.rsqrt(jnp.mean(h3 * h3, axis=-1, keepdims=True) + EPS) * fg_ref[...]
        o_ref[...] = h3


def _ffn_call(h, p, g, w_up, cw, cb, w_down, w_ple, w_pg, fg, *, final_norm, tm=1024):
    n_b, seq, d = h.shape
    tm = min(tm, seq)
    n_tiles = seq // tm
    hb = tm // _FFN_HALO
    n_hb = seq // _FFN_HALO
    kern = functools.partial(_ffn_kernel, tm=tm, n_tiles=n_tiles, final_norm=final_norm)
    cst = lambda shape: pl.BlockSpec(shape, lambda b, i, j: tuple(0 for _ in shape))
    return pl.pallas_call(
        kern,
        out_shape=jax.ShapeDtypeStruct((n_b, seq, d), F32),
        grid=(n_b, n_tiles, _FFN_NJ),
        in_specs=[pl.BlockSpec((None, tm, d), lambda b, i, j: (b, i, 0)),
                  pl.BlockSpec((None, _FFN_HALO, d), lambda b, i, j: (b, jnp.maximum(i * hb - 1, 0), 0)),
                  pl.BlockSpec((None, _FFN_HALO, d), lambda b, i, j: (b, jnp.minimum((i + 1) * hb, n_hb - 1), 0)),
                  pl.BlockSpec((None, tm, PLE_DIM), lambda b, i, j: (b, i, 0)),
                  cst((1, d)),
                  pl.BlockSpec((d, _FFN_CW), lambda b, i, j: (0, j)),
                  pl.BlockSpec((d, _FFN_CW), lambda b, i, j: (0, _FFN_NJ + j)),
                  pl.BlockSpec((FFN_CONV, _FFN_CW), lambda b, i, j: (0, j)),
                  pl.BlockSpec((FFN_CONV, _FFN_CW), lambda b, i, j: (0, _FFN_NJ + j)),
                  pl.BlockSpec((1, _FFN_CW), lambda b, i, j: (0, j)),
                  pl.BlockSpec((1, _FFN_CW), lambda b, i, j: (0, _FFN_NJ + j)),
                  pl.BlockSpec((_FFN_CW, d), lambda b, i, j: (j, 0)),
                  cst((PLE_DIM, d)), cst((d, d)), cst((1, d))],
        out_specs=pl.BlockSpec((None, tm, d), lambda b, i, j: (b, i, 0)),
        scratch_shapes=[pltpu.VMEM((tm + 2 * _FFN_HALO, d), BF16)],
        compiler_params=pltpu.CompilerParams(
            dimension_semantics=("parallel", "parallel", "arbitrary"),
            vmem_limit_bytes=VMEM_LIMIT),
        name="convffn_ple",
    )(h, h, h, p, g.reshape(1, d), w_up, w_up, cw, cw, cb.reshape(1, -1), cb.reshape(1, -1),
      w_down, w_ple, w_pg, fg.reshape(1, d))


def _cols(w, start, width):
    return lax.slice_in_dim(w, start, start + width, axis=1)


def _pad_cols(w, width):
    return jnp.pad(w, ((0, 0), (0, width - w.shape[1])))


def _ssd_params(w_in, conv_w, conv_b, dt_bias, a_log, d_skip, norm_g):
    ws, cws, cbs, dtbs, alogs = [], [], [], [], []
    for g in range(SSD_G):
        dt_cols = jnp.concatenate(
            [_cols(w_in, _OFF_DT + dd * SSD_H + g * SSD_HG, SSD_HG) for dd in range(2)], axis=1)
        ws.append(jnp.concatenate([
            _cols(w_in, _OFF_Z + g * SSD_GW, SSD_GW),
            _cols(w_in, _OFF_XBC + g * SSD_GW, SSD_GW),
            _cols(w_in, _OFF_XBC + D_MODEL + g * SSD_N, SSD_N),
            _cols(w_in, _OFF_XBC + D_MODEL + SSD_G * SSD_N + g * SSD_N, SSD_N),
            _pad_cols(dt_cols, LANES)], axis=1))
        pick = lambda a: jnp.concatenate([
            _cols(a, g * SSD_GW, SSD_GW),
            _cols(a, D_MODEL + g * SSD_N, SSD_N),
            _cols(a, D_MODEL + SSD_G * SSD_N + g * SSD_N, SSD_N)], axis=1)
        cws.append(pick(conv_w))
        cbs.append(pick(conv_b.reshape(1, -1)))
        head_row = lambda a: _pad_cols(
            jnp.concatenate([a[dd, g * SSD_HG:(g + 1) * SSD_HG] for dd in range(2)]).reshape(1, -1), LANES)
        dtbs.append(head_row(dt_bias))
        alogs.append(head_row(a_log))
    dsk = jnp.repeat(d_skip, SSD_HEAD_DIM).reshape(SSD_G, 1, SSD_GW)
    return (jnp.stack(ws).astype(BF16), jnp.stack(cws), jnp.stack(cbs), jnp.stack(dtbs),
            jnp.stack(alogs), dsk, norm_g.reshape(SSD_G, 1, SSD_GW))


def _hgrn2_params(w_in, lb, norm_g):
    ws = [jnp.concatenate([
        _cols(w_in, _OFF_GQ + h * HEAD, HEAD),
        _cols(w_in, _OFF_GF + h * HEAD, HEAD),
        _cols(w_in, _OFF_GF + D_MODEL + h * HEAD, HEAD),
        _cols(w_in, _OFF_GI + h * HEAD, HEAD),
        _cols(w_in, _OFF_GG + h * HEAD, HEAD)], axis=1) for h in range(HG_H)]
    lbs = lb.reshape(2, HG_H, HEAD).swapaxes(0, 1)
    return jnp.stack(ws).astype(BF16), lbs, norm_g.reshape(HG_H, 1, HEAD)


def _mlstm_params(w_in, i_bias, f_bias, norm_g):
    ws, biases = [], []
    for h in range(ML_H):
        gate_cols = jnp.concatenate([
            _cols(w_in, _OFF_MI + h, 1), _cols(w_in, _OFF_MI + ML_H + h, 1),
            _cols(w_in, _OFF_MF + h, 1), _cols(w_in, _OFF_MF + ML_H + h, 1)], axis=1)
        ws.append(jnp.concatenate([
            _cols(w_in, _OFF_MQ + h * HEAD, HEAD),
            _cols(w_in, _OFF_MK + h * HEAD, HEAD),
            _cols(w_in, _OFF_MV + h * HEAD, HEAD),
            _cols(w_in, _OFF_MO + h * HEAD, HEAD),
            _pad_cols(gate_cols, LANES)], axis=1))
        biases.append(_pad_cols(
            jnp.stack([i_bias[0, h], i_bias[1, h], f_bias[0, h], f_bias[1, h]]).reshape(1, 4), LANES))
    return jnp.stack(ws).astype(BF16), jnp.stack(biases), norm_g.reshape(ML_H, 1, HEAD)


def kernel(x, p, norm_mix_g, w_in, ssd_conv_w, ssd_conv_b, ssd_dt_bias, ssd_a_log, ssd_d, ssd_norm_g, hg_lb_raw, hg_norm_g, ml_i_bias, ml_f_bias, ml_norm_g, w_br_ssd, w_br_hg, w_br_ml, w_out, norm_ffn_g, w_up, ffn_conv_w, ffn_conv_b, w_down, w_ple, w_ple_gate, final_norm_g):
    n_b, seq, d = x.shape
    depth = w_in.shape[0]
    lb_soft = jax.nn.softmax(hg_lb_raw.astype(F32), axis=0)
    hg_lb = jnp.cumsum(lb_soft, axis=0) - lb_soft[0:1]
    h = x
    for l in range(depth):
        u2d = _rmsnorm_call(h.reshape(n_b * seq, d), norm_mix_g[l], BF16)
        u = u2d.reshape(n_b, seq, d)
        y_ssd = _ssd_call(u, *_ssd_params(w_in[l], ssd_conv_w[l], ssd_conv_b[l], ssd_dt_bias[l],
                                          ssd_a_log[l], ssd_d[l], ssd_norm_g[l]))
        y_hg = _hgrn2_call(u, *_hgrn2_params(w_in[l], hg_lb[l], hg_norm_g[l]))
        y_ml = _mlstm_call(u, *_mlstm_params(w_in[l], ml_i_bias[l], ml_f_bias[l], ml_norm_g[l]))
        wg = jnp.stack([_cols(w_in[l], _OFF_GATES + r * d, d) for r in range(3)]).astype(BF16)
        wb = jnp.stack([w_br_ssd[l], w_br_hg[l], w_br_ml[l]]).astype(BF16)
        tok = lambda a: a.reshape(n_b * seq, d)
        h = _merge_call(u2d, tok(h), tok(y_ssd), tok(y_hg), tok(y_ml), wg, wb,
                        w_out[l].astype(BF16)).reshape(n_b, seq, d)
        h = _ffn_call(h, p[l], norm_ffn_g[l], w_up[l].astype(BF16), ffn_conv_w[l], ffn_conv_b[l],
                      w_down[l].astype(BF16), w_ple[l].astype(BF16), w_ple_gate[l].astype(BF16),
                      final_norm_g, final_norm=(l == depth - 1))
    return h
```

```python
import functools

import jax
import jax.numpy as jnp
from jax import lax
from jax.experimental import pallas as pl
from jax.experimental.pallas import tpu as pltpu

F32 = jnp.float32
BF16 = jnp.bfloat16

D_MODEL = 1024
PLE_DIM = 256
EPS = 1e-6
NEG = -1e30
TINY = 1e-30

SSD_HEAD_DIM = 64
SSD_H = 16
SSD_G = 2
SSD_HG = 8
SSD_N = 128
SSD_CONV = 5
SSD_GW = SSD_HG * SSD_HEAD_DIM
SSD_XBC = D_MODEL + 2 * SSD_G * SSD_N
HG_H = 8
ML_H = 8
HEAD = 128
D_FF = 2816
FFN_CONV = 3

LANES = 128
SUBLANES = 8
VMEM_LIMIT = 56 * 1024 * 1024

SSD_CHUNK = 128
HG_CHUNK = 64
ML_CHUNK = 128
SSD_UNROLL = 2
HG_UNROLL = 16
HG_GROUP = 16
ML_GROUP = 8
HG_SAFE_DECAY = 60.0

_OFF_Z = 0
_OFF_XBC = 1024
_OFF_DT = _OFF_XBC + SSD_XBC
_OFF_GQ = _OFF_DT + 2 * SSD_H
_OFF_GF = _OFF_GQ + 1024
_OFF_GI = _OFF_GF + 2048
_OFF_GG = _OFF_GI + 1024
_OFF_MQ = _OFF_GG + 1024
_OFF_MK = _OFF_MQ + 1024
_OFF_MV = _OFF_MK + 1024
_OFF_MI = _OFF_MV + 1024
_OFF_MF = _OFF_MI + 2 * ML_H
_OFF_MO = _OFF_MF + 2 * ML_H
_OFF_GATES = _OFF_MO + 1024


def _dot(a, b):
    return jnp.dot(a, b, preferred_element_type=F32)


def _dot_nt(a, b):
    return lax.dot_general(a, b, (((1,), (1,)), ((), ())), preferred_element_type=F32)


def _dot_tn(a, b):
    return lax.dot_general(a, b, (((0,), (0,)), ((), ())), preferred_element_type=F32)


def _sigmoid(x):
    return 1.0 / (1.0 + jnp.exp(-x))


def _silu(x):
    return x * _sigmoid(x)


def _softplus(x):
    return jnp.maximum(x, 0.0) + jnp.log1p(jnp.exp(-jnp.abs(x)))


def _chunk_scan(x, chunk, reverse, op):
    n_rows = x.shape[0]
    row = lax.broadcasted_iota(jnp.int32, x.shape, 0) & (chunk - 1)
    sh = 1
    while sh < chunk:
        if reverse:
            shifted = pltpu.roll(x, n_rows - sh, axis=0)
            ok = row < chunk - sh
        else:
            shifted = pltpu.roll(x, sh, axis=0)
            ok = row >= sh
        if op == "add":
            x = x + jnp.where(ok, shifted, 0.0)
        else:
            x = jnp.maximum(x, jnp.where(ok, shifted, NEG))
        sh *= 2
    return x


def _lane_scan(x, seg, reverse, op, start=1):
    n_lanes = x.shape[1]
    pos = lax.broadcasted_iota(jnp.int32, x.shape, 1) & (seg - 1)
    sh = start
    while sh < seg:
        if reverse:
            shifted = pltpu.roll(x, n_lanes - sh, axis=1)
            ok = pos < seg - sh
        else:
            shifted = pltpu.roll(x, sh, axis=1)
            ok = pos >= sh
        if op == "add":
            x = x + jnp.where(ok, shifted, 0.0)
        else:
            x = jnp.maximum(x, jnp.where(ok, shifted, NEG))
        sh *= 2
    return x


def _tri_mask(n, reverse):
    t = lax.broadcasted_iota(jnp.int32, (n, n), 0)
    s = lax.broadcasted_iota(jnp.int32, (n, n), 1)
    return (s >= t) if reverse else (s <= t)


def _rmsnorm_kernel(h_ref, g_ref, o_ref):
    x = h_ref[...]
    ms = jnp.mean(x * x, axis=-1, keepdims=True)
    o_ref[...] = (x * lax.rsqrt(ms + EPS) * g_ref[...]).astype(o_ref.dtype)


def _rmsnorm_call(h2d, g, out_dtype, tm=1024):
    n_tok, d = h2d.shape
    tm = min(tm, n_tok)
    return pl.pallas_call(
        _rmsnorm_kernel,
        out_shape=jax.ShapeDtypeStruct((n_tok, d), out_dtype),
        grid=(n_tok // tm,),
        in_specs=[pl.BlockSpec((tm, d), lambda i: (i, 0)),
                  pl.BlockSpec((1, d), lambda i: (0, 0))],
        out_specs=pl.BlockSpec((tm, d), lambda i: (i, 0)),
        compiler_params=pltpu.CompilerParams(dimension_semantics=("parallel",),
                                             vmem_limit_bytes=VMEM_LIMIT),
        name="rmsnorm",
    )(h2d, g.reshape(1, d))


def _hg_att_safe(qc, kc, lfc, reverse):
    n = qc.shape[0]
    t = lax.broadcasted_iota(jnp.int32, (n, n), 0)
    s = lax.broadcasted_iota(jnp.int32, (n, n), 1)
    att = jnp.where(t == s, jnp.sum(qc * kc, axis=-1, keepdims=True), 0.0)
    half = 1
    while half < n:
        pre = _chunk_scan(lfc, half, False, "add")
        suf = _chunk_scan(lfc, half, True, "add")
        t_hi = (t & half) != 0
        s_hi = (s & half) != 0
        same_block = (t ^ s) < 2 * half
        if reverse:
            qe = qc * jnp.exp(suf)
            ke = kc * jnp.exp(pre - lfc)
            pair = jnp.logical_and(same_block, jnp.logical_and(jnp.logical_not(t_hi), s_hi))
        else:
            qe = qc * jnp.exp(pre)
            ke = kc * jnp.exp(suf - lfc)
            pair = jnp.logical_and(same_block, jnp.logical_and(t_hi, jnp.logical_not(s_hi)))
        att = att + jnp.where(pair, _dot_nt(qe.astype(BF16), ke.astype(BF16)), 0.0)
        half *= 2
    return att


def _hgrn2_kernel(u_ref, w_ref, lb_ref, ng_ref, o_ref,
                  q_s, v_s, k_s, lf_s, b_s, qe_s, ke_s, qi_s, kd_s, dec_s, upd_s, st_s, acc_s,
                  *, seq, chunk):
    n_chunks = seq // chunk
    proj = _dot(u_ref[...], w_ref[...])
    q_s[...] = _silu(proj[:, 0:HEAD])
    v_s[...] = proj[:, 3 * HEAD:4 * HEAD].astype(BF16)
    o_ref[...] = proj[:, 4 * HEAD:5 * HEAD]

    for d in range(2):
        f_raw = proj[:, (1 + d) * HEAD:(2 + d) * HEAD]
        lb = lb_ref[d:d + 1, :]
        sig_f = _sigmoid(f_raw)
        f = lb + (1.0 - lb) * sig_f
        k_s[d] = (1.0 - lb) * (1.0 - sig_f)
        log_f = jnp.log(jnp.maximum(f, TINY))
        lf_s[d] = log_f
        b = _chunk_scan(log_f, chunk, d == 1, "add")
        b_s[d] = b

    def ref_span(d):
        first, mid, last = (b_s[d, pl.ds(r, n_chunks, stride=chunk), :] for r in (0, chunk // 2, chunk - 1))
        return jnp.maximum(jnp.abs(first - mid), jnp.abs(mid - last))

    mild_decay = jnp.max(jnp.maximum(ref_span(0), ref_span(1))) <= HG_SAFE_DECAY

    for d in range(2):
        reverse = d == 1
        mask = _tri_mask(chunk, reverse)
        end_row = 0 if reverse else chunk - 1

        def chunk_rows(c):
            r0 = pl.multiple_of(c * chunk, chunk)
            return r0, pl.ds(r0, chunk)

        def scale_body(c, carry, d=d, end_row=end_row):
            r0, rows = chunk_rows(c)
            qc = q_s[rows, :]
            kc = k_s[d, rows, :]
            bc = b_s[d, rows, :]
            ref = b_s[d, pl.ds(r0 + chunk // 2, 1), :]
            bend = b_s[d, pl.ds(r0 + end_row, 1), :]
            qe_s[rows, :] = (qc * jnp.exp(bc - ref)).astype(BF16)
            ke_s[rows, :] = (kc * jnp.exp(ref - bc)).astype(BF16)
            qi_s[rows, :] = (qc * jnp.exp(bc)).astype(BF16)
            kd_s[rows, :] = (kc * jnp.exp(bend - bc)).astype(BF16)
            dec_s[c] = jnp.broadcast_to(jnp.exp(bend), (HEAD, HEAD)).T
            return carry

        lax.fori_loop(0, n_chunks, scale_body, 0, unroll=HG_UNROLL)

        for c in range(n_chunks):
            rows = slice(c * chunk, (c + 1) * chunk)
            upd_s[c] = _dot_tn(kd_s[rows, :], v_s[rows, :])

        def rec_body(ci, st, reverse=reverse):
            c = (n_chunks - 1 - ci) if reverse else ci
            st_s[c] = st.astype(BF16)
            return st * dec_s[c] + upd_s[c]

        lax.fori_loop(0, n_chunks, rec_body, jnp.zeros((HEAD, HEAD), F32))

        def emit(c, rows, att, d=d):
            o = _dot(jnp.concatenate([qi_s[rows, :], att.astype(BF16)], axis=1),
                     jnp.concatenate([st_s[c], v_s[rows, :]], axis=0))
            if d == 0:
                acc_s[rows, :] = o
            else:
                acc_s[rows, :] = acc_s[rows, :] + o

        @pl.when(mild_decay)
        def _(mask=mask):
            for g0 in range(0, n_chunks, HG_GROUP):
                group = [(c, slice(c * chunk, (c + 1) * chunk)) for c in range(g0, min(g0 + HG_GROUP, n_chunks))]
                atts = [jnp.where(mask, _dot_nt(qe_s[rows, :], ke_s[rows, :]), 0.0) for _, rows in group]
                for (c, rows), att in zip(group, atts):
                    emit(c, rows, att)

        @pl.when(jnp.logical_not(mild_decay))
        def _(d=d, reverse=reverse):
            def safe_body(c, carry):
                _, rows = chunk_rows(c)
                emit(c, rows, _hg_att_safe(q_s[rows, :], k_s[d, rows, :], lf_s[d, rows, :], reverse))
                return carry

            lax.fori_loop(0, n_chunks, safe_body, 0)

    o = acc_s[...]
    o = o * lax.rsqrt(jnp.mean(o * o, axis=-1, keepdims=True) + EPS) * ng_ref[...]
    o_ref[...] = o * _silu(o_ref[...])


def _hgrn2_call(u, w, lb, ng, *, chunk=HG_CHUNK):
    n_b, seq, d = u.shape
    kern = functools.partial(_hgrn2_kernel, seq=seq, chunk=chunk)
    return pl.pallas_call(
        kern,
        out_shape=jax.ShapeDtypeStruct((n_b, seq, HG_H * HEAD), F32),
        grid=(n_b, HG_H),
        in_specs=[pl.BlockSpec((None, seq, d), lambda b, h: (b, 0, 0)),
                  pl.BlockSpec((None, d, 5 * HEAD), lambda b, h: (h, 0, 0)),
                  pl.BlockSpec((None, 2, HEAD), lambda b, h: (h, 0, 0)),
                  pl.BlockSpec((None, 1, HEAD), lambda b, h: (h, 0, 0))],
        out_specs=pl.BlockSpec((None, seq, HEAD), lambda b, h: (b, 0, h)),
        scratch_shapes=[pltpu.VMEM((seq, HEAD), F32),
                        pltpu.VMEM((seq, HEAD), BF16),
                        pltpu.VMEM((2, seq, HEAD), F32),
                        pltpu.VMEM((2, seq, HEAD), F32),
                        pltpu.VMEM((2, seq, HEAD), F32),
                        pltpu.VMEM((seq, HEAD), BF16),
                        pltpu.VMEM((seq, HEAD), BF16),
                        pltpu.VMEM((seq, HEAD), BF16),
                        pltpu.VMEM((seq, HEAD), BF16),
                        pltpu.VMEM((seq // chunk, HEAD, HEAD), F32),
                        pltpu.VMEM((seq // chunk, HEAD, HEAD), F32),
                        pltpu.VMEM((seq // chunk, HEAD, HEAD), BF16),
                        pltpu.VMEM((seq, HEAD), F32)],
        compiler_params=pltpu.CompilerParams(dimension_semantics=("parallel", "arbitrary"),
                                             vmem_limit_bytes=VMEM_LIMIT),
        name="hgrn2_branch",
    )(u, w, lb, ng)


def _log_sigmoid(x):
    return jnp.minimum(x, 0.0) - jnp.log1p(jnp.exp(-jnp.abs(x)))


def _mlstm_kernel(u_ref, w_ref, bias_ref, ng_ref, o_ref,
                  qb_s, kb_s, va_s, tm_s, qw_s, kw_s, mx_s, en_s, gt_s, dec_s, upd_s, ct_s,
                  acc_s, *, seq, chunk):
    n_chunks = seq // chunk
    proj = _dot(u_ref[...], w_ref[...])
    q = proj[:, 0:HEAD]
    k = proj[:, HEAD:2 * HEAD] * (HEAD ** -0.5)
    qb_s[...] = q.astype(BF16)
    kb_s[...] = k.astype(BF16)
    va_s[:, 0:HEAD] = proj[:, 2 * HEAD:3 * HEAD].astype(BF16)
    va_s[:, HEAD:2 * HEAD] = jnp.ones((seq, HEAD), BF16)
    o_ref[...] = proj[:, 3 * HEAD:4 * HEAD]

    x = (proj[:, 4 * HEAD:5 * HEAD] + bias_ref[...]).T[0:SUBLANES, :]
    fwd_row = lax.broadcasted_iota(jnp.int32, (SUBLANES, seq), 0) == 0
    pos = lax.broadcasted_iota(jnp.int32, (SUBLANES, seq), 1)

    def scans(y, op, seg, start=1):
        return _lane_scan(y, seg, False, op, start), _lane_scan(y, seg, True, op, start)

    it = x
    log_f = pltpu.roll(_log_sigmoid(x), SUBLANES - 2, axis=0)
    pre, suf = scans(log_f, "add", chunk)
    bt = jnp.where(fwd_row, pre, suf)
    bt_end = pre + suf - log_f
    g = it - bt
    g_pre, g_suf = scans(g, "max", chunk)
    cm = jnp.where(fwd_row, g_pre, g_suf)
    cm_end = jnp.maximum(g_pre, g_suf)

    p_pre, p_suf = scans(bt_end, "add", seq, chunk)
    p_sum = jnp.where(fwd_row, p_pre, p_suf)
    x_pre, x_suf = scans(cm_end - (p_sum - bt_end), "max", seq, chunk)
    m_end = p_sum + jnp.where(fwd_row, x_pre, x_suf)
    m_prev = jnp.where(fwd_row,
                       jnp.where(pos < chunk, NEG, pltpu.roll(m_end, chunk, axis=1)),
                       jnp.where(pos >= seq - chunk, NEG, pltpu.roll(m_end, seq - chunk, axis=1)))
    mx = jnp.maximum(m_prev, cm)
    w_inter = jnp.exp(m_prev - mx)
    wk = jnp.exp(bt_end - bt + it - m_end)
    en = jnp.exp(-(bt + mx))
    decay = jnp.exp(bt_end + m_prev - m_end)

    for c in range(n_chunks):
        gt_s[c] = g[:, c * chunk:(c + 1) * chunk]
    row = lax.broadcasted_iota(jnp.int32, (SUBLANES, seq), 0)
    packed = jnp.where(row < 2, w_inter,
                       jnp.where(row < 4, pltpu.roll(wk, 2, axis=0),
                                 jnp.where(row < 6, pltpu.roll(mx, 4, axis=0), pltpu.roll(en, 6, axis=0))))
    tm_s[...] = jnp.concatenate(
        [packed, decay, jnp.zeros((LANES - 2 * SUBLANES, seq), F32)], axis=0).T
    for d in range(2):
        def lanes(col, ref_rows=slice(None)):
            block = tm_s[ref_rows, :]
            return jnp.broadcast_to(block[:, col:col + 1], (block.shape[0], LANES))
        qw_s[d] = (q * lanes(d)).astype(BF16)
        kw_s[d] = (k * lanes(2 + d)).astype(BF16)
        mx_s[d] = lanes(4 + d)
        en_s[d] = lanes(6 + d)
        dec_s[d] = lanes(8 + d, pl.ds(0, n_chunks, stride=chunk))

    for d in range(2):
        reverse = d == 1
        mask = _tri_mask(chunk, reverse)

        def chunk_rows(c):
            return pl.ds(pl.multiple_of(c * chunk, chunk), chunk)

        for c in range(n_chunks):
            rows = slice(c * chunk, (c + 1) * chunk)
            upd_s[c] = _dot_tn(kw_s[d, rows, :], va_s[rows, :])

        def rec_body(ci, ct, reverse=reverse, d=d):
            c = (n_chunks - 1 - ci) if reverse else ci
            ct_s[c] = ct.astype(BF16)
            dec = dec_s[d, pl.ds(c, 1), :]
            return ct * jnp.concatenate([dec, dec], axis=1) + upd_s[c]

        lax.fori_loop(0, n_chunks, rec_body, jnp.zeros((HEAD, 2 * HEAD), F32))

        def scores(c, rows):
            p = jnp.exp(jnp.where(mask, gt_s[c][d:d + 1, :] - mx_s[d, rows, :], NEG))
            return (_dot_nt(qb_s[rows, :], kb_s[rows, :]) * p).astype(BF16)

        def emit(c, rows, sc):
            tot = _dot(jnp.concatenate([qw_s[d, rows, :], sc], axis=1),
                       jnp.concatenate([ct_s[c], va_s[rows, :]], axis=0))
            hval = tot[:, 0:HEAD] / jnp.maximum(jnp.abs(tot[:, HEAD:2 * HEAD]), en_s[d, rows, :])
            if d == 0:
                acc_s[rows, :] = hval
            else:
                acc_s[rows, :] = acc_s[rows, :] + hval

        for g0 in range(0, n_chunks, ML_GROUP):
            group = [(c, slice(c * chunk, (c + 1) * chunk)) for c in range(g0, min(g0 + ML_GROUP, n_chunks))]
            staged = [scores(c, rows) for c, rows in group]
            for (c, rows), sc in zip(group, staged):
                emit(c, rows, sc)

    hsum = acc_s[...]
    hc = hsum - jnp.mean(hsum, axis=-1, keepdims=True)
    hn = hc * lax.rsqrt(jnp.mean(hc * hc, axis=-1, keepdims=True) + EPS) * ng_ref[...]
    o_ref[...] = hn * _sigmoid(o_ref[...])


def _mlstm_call(u, w, bias, ng, *, chunk=ML_CHUNK):
    n_b, seq, d = u.shape
    n_chunks = seq // chunk
    kern = functools.partial(_mlstm_kernel, seq=seq, chunk=chunk)
    return pl.pallas_call(
        kern,
        out_shape=jax.ShapeDtypeStruct((n_b, seq, ML_H * HEAD), F32),
        grid=(n_b, ML_H),
        in_specs=[pl.BlockSpec((None, seq, d), lambda b, h: (b, 0, 0)),
                  pl.BlockSpec((None, d, 5 * HEAD), lambda b, h: (h, 0, 0)),
                  pl.BlockSpec((None, 1, HEAD), lambda b, h: (h, 0, 0)),
                  pl.BlockSpec((None, 1, HEAD), lambda b, h: (h, 0, 0))],
        out_specs=pl.BlockSpec((None, seq, HEAD), lambda b, h: (b, 0, h)),
        scratch_shapes=[pltpu.VMEM((seq, HEAD), BF16),
                        pltpu.VMEM((seq, HEAD), BF16),
                        pltpu.VMEM((seq, 2 * HEAD), BF16),
                        pltpu.VMEM((seq, LANES), F32),
                        pltpu.VMEM((2, seq, HEAD), BF16),
                        pltpu.VMEM((2, seq, HEAD), BF16),
                        pltpu.VMEM((2, seq, LANES), F32),
                        pltpu.VMEM((2, seq, LANES), F32),
                        pltpu.VMEM((n_chunks, SUBLANES, chunk), F32),
                        pltpu.VMEM((2, n_chunks, LANES), F32),
                        pltpu.VMEM((n_chunks, HEAD, 2 * HEAD), F32),
                        pltpu.VMEM((n_chunks, HEAD, 2 * HEAD), BF16),
                        pltpu.VMEM((seq, HEAD), F32)],
        compiler_params=pltpu.CompilerParams(dimension_semantics=("parallel", "arbitrary"),
                                             vmem_limit_bytes=VMEM_LIMIT),
        name="mlstm_branch",
    )(u, w, bias, ng)


_SSD_W_Z = 0
_SSD_W_XBC = SSD_GW
_SSD_W_DT = SSD_GW + SSD_GW + 2 * SSD_N
_SSD_W_END = _SSD_W_DT + LANES
_SSD_XBC_G = SSD_GW + 2 * SSD_N
_SSD_PAD = SUBLANES
_SSD_PAIRS = SSD_HG // 2


def _ssd_kernel(u_ref, w_ref, cw_ref, cb_ref, dtb_ref, alog_ref, dsk_ref, ng_ref, o_ref,
                pad_s, xs_s, b_s, c_s, a_s, at_s, dtt_s, cbm_s, ea_s, xw_s, dec_s, cur_s, st_s,
                *, seq, chunk, rblk):
    n_chunks = seq // chunk
    n_rblk = seq // rblk

    zero_rows = jnp.zeros((_SSD_PAD, _SSD_XBC_G), F32)
    pad_s[0:_SSD_PAD, :] = zero_rows
    pad_s[_SSD_PAD + seq:2 * _SSD_PAD + seq, :] = zero_rows
    for i in range(n_rblk):
        r0 = i * rblk
        pad_s[_SSD_PAD + r0:_SSD_PAD + r0 + rblk, :] = _dot(
            u_ref[r0:r0 + rblk, :], w_ref[:, _SSD_W_XBC:_SSD_W_DT])
    for i in range(n_rblk):
        r0 = i * rblk
        acc = cb_ref[...]
        slab = pad_s[r0:r0 + rblk + 2 * _SSD_PAD, :]
        for j in range(SSD_CONV):
            tap = slab if j == SSD_CONV // 2 else pltpu.roll(
                slab, (SSD_CONV // 2 - j) % (rblk + 2 * _SSD_PAD), axis=0)
            acc = acc + tap[_SSD_PAD:_SSD_PAD + rblk, :] * cw_ref[j:j + 1, :]
        xbc = _silu(acc)
        xs_s[r0:r0 + rblk, :] = xbc[:, 0:SSD_GW]
        b_s[r0:r0 + rblk, :] = xbc[:, SSD_GW:SSD_GW + SSD_N].astype(BF16)
        c_s[r0:r0 + rblk, :] = xbc[:, SSD_GW + SSD_N:SSD_GW + 2 * SSD_N].astype(BF16)

    n_hd = 2 * SSD_HG
    dt_tok = _softplus(_dot(u_ref[...], w_ref[:, _SSD_W_DT:_SSD_W_END]) + dtb_ref[...])
    dt = dt_tok.T[0:n_hd, :]
    log_a = (dt_tok * (-jnp.exp(alog_ref[...]))).T[0:n_hd, :]
    fwd_row = lax.broadcasted_iota(jnp.int32, (n_hd, seq), 0) < SSD_HG
    pre = _lane_scan(log_a, chunk, False, "add")
    suf = _lane_scan(log_a, chunk, True, "add")
    acum = jnp.where(fwd_row, pre, suf)
    a_end = pre + suf - log_a
    for c in range(n_chunks):
        lanes_c = slice(c * chunk, (c + 1) * chunk)
        at_s[c] = acum[:, lanes_c]
        dtt_s[c] = dt[:, lanes_c]
        r0 = c * chunk
        cbm_s[r0:r0 + chunk, :] = _dot_nt(c_s[r0:r0 + chunk, :], b_s[r0:r0 + chunk, :])
    a_s[...] = jnp.concatenate(
        [acum, jnp.exp(acum), dt * jnp.exp(a_end - acum), jnp.exp(a_end),
         jnp.zeros((LANES - 4 * n_hd, seq), F32)], axis=0).T
    lane_lo = lax.broadcasted_iota(jnp.int32, (chunk, LANES), 1) < SSD_HEAD_DIM

    def split3(x):
        hi = x.astype(BF16)
        r1 = x - hi.astype(F32)
        mid = r1.astype(BF16)
        return [hi, mid, (r1 - mid.astype(F32)).astype(BF16)]

    def spread(first_lane, pieces):
        one = (lax.broadcasted_iota(jnp.int32, (LANES, SSD_GW), 1) // SSD_HEAD_DIM
               == lax.broadcasted_iota(jnp.int32, (LANES, SSD_GW), 0) - first_lane).astype(BF16)
        return jnp.concatenate([one] * pieces, axis=0)

    for d in range(2):
        reverse = d == 1
        mask = _tri_mask(chunk, reverse)
        spread_ea = spread(n_hd + d * SSD_HG, 1)
        spread_wt = spread(2 * n_hd + d * SSD_HG, 2)
        dec_s[...] = _dot(jnp.concatenate(split3(a_s[pl.ds(0, n_chunks, stride=chunk), :]), axis=1),
                          spread(3 * n_hd + d * SSD_HG, 3))
        for i in range(n_rblk):
            rows = slice(i * rblk, (i + 1) * rblk)
            hi, mid, _ = split3(a_s[rows, :])
            ea_s[rows, :] = _dot(hi, spread_ea).astype(BF16)
            xw_s[rows, :] = (xs_s[rows, :] * _dot(jnp.concatenate([hi, mid], axis=1), spread_wt)).astype(BF16)

        def chunk_rows(c):
            return pl.ds(pl.multiple_of(c * chunk, chunk), chunk)

        for c in range(n_chunks):
            rows = slice(c * chunk, (c + 1) * chunk)
            pad_s[rows, 0:SSD_GW] = _dot_tn(b_s[rows, :], xw_s[rows, :])

        cur_s[...] = jnp.zeros_like(cur_s)

        def rec_body(ci, carry, reverse=reverse):
            c = (n_chunks - 1 - ci) if reverse else ci
            cur = cur_s[...]
            st_s[c] = cur.astype(BF16)
            cur_s[...] = cur * dec_s[pl.ds(c, 1), :] + pad_s[chunk_rows(c), 0:SSD_GW]
            return carry

        lax.fori_loop(0, n_chunks, rec_body, 0)

        def out_body(c, carry, mask=mask, d=d):
            rows = chunk_rows(c)
            blk = a_s[rows, :]
            blk_t = at_s[c]
            dtt = dtt_s[c]
            cbm = cbm_s[rows, :]
            y_inter = _dot(c_s[rows, :], st_s[c])
            for p in range(_SSD_PAIRS):
                cols = slice(p * LANES, (p + 1) * LANES)
                xs_pair = xs_s[rows, cols]
                m_tiles, x_tiles = [], []
                for hh in range(2):
                    col = d * SSD_HG + 2 * p + hh
                    seg = jnp.exp(jnp.where(mask, blk[:, col:col + 1] - blk_t[col:col + 1, :], NEG))
                    m_tiles.append((cbm * seg * dtt[col:col + 1, :]).astype(BF16))
                    keep = lane_lo if hh == 0 else jnp.logical_not(lane_lo)
                    x_tiles.append(jnp.where(keep, xs_pair, 0.0).astype(BF16))
                y = (y_inter[:, cols] * ea_s[rows, cols].astype(F32)
                     + _dot(jnp.concatenate(m_tiles, axis=1), jnp.concatenate(x_tiles, axis=0)))
                if d == 0:
                    o_ref[rows, cols] = y
                else:
                    o_ref[rows, cols] = o_ref[rows, cols] + y
            return carry

        lax.fori_loop(0, n_chunks, out_body, 0, unroll=SSD_UNROLL)

    for i in range(n_rblk):
        r0 = i * rblk
        rows = slice(r0, r0 + rblk)
        z = _dot(u_ref[rows, :], w_ref[:, _SSD_W_Z:_SSD_W_XBC])
        y = (o_ref[rows, :] + xs_s[rows, :] * dsk_ref[...]) * _silu(z)
        o_ref[rows, :] = y * lax.rsqrt(jnp.mean(y * y, axis=-1, keepdims=True) + EPS) * ng_ref[...]


def _ssd_call(u, w, cw, cb, dtb, alog, dsk, ng, *, chunk=SSD_CHUNK):
    n_b, seq, d = u.shape
    rblk = min(seq, 512)
    kern = functools.partial(_ssd_kernel, seq=seq, chunk=chunk, rblk=rblk)
    vec = lambda n: pl.BlockSpec((None, 1, n), lambda b, g: (g, 0, 0))
    return pl.pallas_call(
        kern,
        out_shape=jax.ShapeDtypeStruct((n_b, seq, SSD_G * SSD_GW), F32),
        grid=(n_b, SSD_G),
        in_specs=[pl.BlockSpec((None, seq, d), lambda b, g: (b, 0, 0)),
                  pl.BlockSpec((None, d, _SSD_W_END), lambda b, g: (g, 0, 0)),
                  pl.BlockSpec((None, SSD_CONV, _SSD_XBC_G), lambda b, g: (g, 0, 0)),
                  vec(_SSD_XBC_G), vec(LANES), vec(LANES), vec(SSD_GW), vec(SSD_GW)],
        out_specs=pl.BlockSpec((None, seq, SSD_GW), lambda b, g: (b, 0, g)),
        scratch_shapes=[pltpu.VMEM((seq + 2 * _SSD_PAD, _SSD_XBC_G), F32),
                        pltpu.VMEM((seq, SSD_GW), F32),
                        pltpu.VMEM((seq, SSD_N), BF16),
                        pltpu.VMEM((seq, SSD_N), BF16),
                        pltpu.VMEM((seq, LANES), F32),
                        pltpu.VMEM((seq // chunk, 2 * SSD_HG, chunk), F32),
                        pltpu.VMEM((seq // chunk, 2 * SSD_HG, chunk), F32),
                        pltpu.VMEM((seq, chunk), F32),
                        pltpu.VMEM((seq, SSD_GW), BF16),
                        pltpu.VMEM((seq, SSD_GW), BF16),
                        pltpu.VMEM((seq // chunk, SSD_GW), F32),
                        pltpu.VMEM((SSD_N, SSD_GW), F32),
                        pltpu.VMEM((seq // chunk, SSD_N, SSD_GW), BF16)],
        compiler_params=pltpu.CompilerParams(dimension_semantics=("parallel", "arbitrary"),
                                             vmem_limit_bytes=VMEM_LIMIT),
        name="ssd_branch",
    )(u, w, cw, cb, dtb, alog, dsk, ng)


def _merge_kernel(u_ref, h_ref, y0_ref, y1_ref, y2_ref, wg_ref, wb_ref, wo_ref, o_ref):
    u = u_ref[...]
    merged = None
    for r, y_ref in enumerate((y0_ref, y1_ref, y2_ref)):
        term = _sigmoid(_dot(u, wg_ref[r])) * _dot(y_ref[...].astype(BF16), wb_ref[r])
        merged = term if merged is None else merged + term
    o_ref[...] = h_ref[...] + _dot(merged.astype(BF16), wo_ref[...])


def _merge_call(u2d, h2d, y0, y1, y2, wg, wb, wo, tm=512):
    n_tok, d = h2d.shape
    tm = min(tm, n_tok)
    tok = lambda: pl.BlockSpec((tm, d), lambda i: (i, 0))
    held = lambda shape: pl.BlockSpec(shape, lambda i: tuple(0 for _ in shape), pipeline_mode=pl.Buffered(1))
    return pl.pallas_call(
        _merge_kernel,
        out_shape=jax.ShapeDtypeStruct((n_tok, d), F32),
        grid=(n_tok // tm,),
        in_specs=[tok(), tok(), tok(), tok(), tok(),
                  held((3, d, d)), held((3, d, d)), held((d, d))],
        out_specs=tok(),
        compiler_params=pltpu.CompilerParams(dimension_semantics=("parallel",),
                                             vmem_limit_bytes=VMEM_LIMIT),
        name="merge",
    )(u2d, h2d, y0, y1, y2, wg, wb, wo)


_FFN_CW = 256
_FFN_NJ = D_FF // _FFN_CW
_FFN_HALO = SUBLANES

def _ffn_kernel(h_ref, hp_ref, hn_ref, p_ref, g_ref, wa_ref, wv_ref, cwa_ref, cwv_ref,
                cba_ref, cbv_ref, wd_ref, wple_ref, wpg_ref, fg_ref, o_ref, u_s,
                *, tm, n_tiles, final_norm):
    i = pl.program_id(1)
    j = pl.program_id(2)

    def norm(x):
        return (x * lax.rsqrt(jnp.mean(x * x, axis=-1, keepdims=True) + EPS) * g_ref[...]).astype(BF16)

    @pl.when(j == 0)
    def _():
        up = jnp.where(i > 0, 1.0, 0.0)
        dn = jnp.where(i < n_tiles - 1, 1.0, 0.0)
        u_s[0:_FFN_HALO, :] = norm(hp_ref[...] * up)
        u_s[_FFN_HALO:_FFN_HALO + tm, :] = norm(h_ref[...])
        u_s[_FFN_HALO + tm:2 * _FFN_HALO + tm, :] = norm(hn_ref[...] * dn)
        o_ref[...] = h_ref[...]

    def conv(w_ref, cw_ref, cb_ref):
        up = _dot(u_s[...], w_ref[...])
        n_rows = tm + 2 * _FFN_HALO
        acc = cb_ref[...]
        for t in range(FFN_CONV):
            tap = up if t == FFN_CONV // 2 else pltpu.roll(up, (FFN_CONV // 2 - t) % n_rows, axis=0)
            acc = acc + tap[_FFN_HALO:_FFN_HALO + tm, :] * cw_ref[t:t + 1, :]
        return acc

    act = (_silu(conv(wa_ref, cwa_ref, cba_ref)) * conv(wv_ref, cwv_ref, cbv_ref)).astype(BF16)
    o_ref[...] += _dot(act, wd_ref[...])

    @pl.when(j == _FFN_NJ - 1)
    def _():
        h2 = o_ref[...]
        ple = _dot(p_ref[...].astype(BF16), wple_ref[...])
        h3 = h2 + ple * _sigmoid(_dot(h2.astype(BF16), wpg_ref[...]))
        if final_norm:
            h3 = h3 * lax.rsqrt(jnp.mean(h3 * h3, axis=-1, keepdims=True) + EPS) * fg_ref[...]
        o_ref[...] = h3


def _ffn_call(h, p, g, w_up, cw, cb, w_down, w_ple, w_pg, fg, *, final_norm, tm=1024):
    n_b, seq, d = h.shape
    tm = min(tm, seq)
    n_tiles = seq // tm
    hb = tm // _FFN_HALO
    n_hb = seq // _FFN_HALO
    kern = functools.partial(_ffn_kernel, tm=tm, n_tiles=n_tiles, final_norm=final_norm)
    cst = lambda shape: pl.BlockSpec(shape, lambda b, i, j: tuple(0 for _ in shape))
    return pl.pallas_call(
        kern,
        out_shape=jax.ShapeDtypeStruct((n_b, seq, d), F32),
        grid=(n_b, n_tiles, _FFN_NJ),
        in_specs=[pl.BlockSpec((None, tm, d), lambda b, i, j: (b, i, 0)),
                  pl.BlockSpec((None, _FFN_HALO, d), lambda b, i, j: (b, jnp.maximum(i * hb - 1, 0), 0)),
                  pl.BlockSpec((None, _FFN_HALO, d), lambda b, i, j: (b, jnp.minimum((i + 1) * hb, n_hb - 1), 0)),
                  pl.BlockSpec((None, tm, PLE_DIM), lambda b, i, j: (b, i, 0)),
                  cst((1, d)),
                  pl.BlockSpec((d, _FFN_CW), lambda b, i, j: (0, j)),
                  pl.BlockSpec((d, _FFN_CW), lambda b, i, j: (0, _FFN_NJ + j)),
                  pl.BlockSpec((FFN_CONV, _FFN_CW), lambda b, i, j: (0, j)),
                  pl.BlockSpec((FFN_CONV, _FFN_CW), lambda b, i, j: (0, _FFN_NJ + j)),
                  pl.BlockSpec((1, _FFN_CW), lambda b, i, j: (0, j)),
                  pl.BlockSpec((1, _FFN_CW), lambda b, i, j: (0, _FFN_NJ + j)),
                  pl.BlockSpec((_FFN_CW, d), lambda b, i, j: (j, 0)),
                  cst((PLE_DIM, d)), cst((d, d)), cst((1, d))],
        out_specs=pl.BlockSpec((None, tm, d), lambda b, i, j: (b, i, 0)),
        scratch_shapes=[pltpu.VMEM((tm + 2 * _FFN_HALO, d), BF16)],
        compiler_params=pltpu.CompilerParams(
            dimension_semantics=("parallel", "parallel", "arbitrary"),
            vmem_limit_bytes=VMEM_LIMIT),
        name="convffn_ple",
    )(h, h, h, p, g.reshape(1, d), w_up, w_up, cw, cw, cb.reshape(1, -1), cb.reshape(1, -1),
      w_down, w_ple, w_pg, fg.reshape(1, d))


def _cols(w, start, width):
    return lax.slice_in_dim(w, start, start + width, axis=1)


def _pad_cols(w, width):
    return jnp.pad(w, ((0, 0), (0, width - w.shape[1])))


def _ssd_params(w_in, conv_w, conv_b, dt_bias, a_log, d_skip, norm_g):
    ws, cws, cbs, dtbs, alogs = [], [], [], [], []
    for g in range(SSD_G):
        dt_cols = jnp.concatenate(
            [_cols(w_in, _OFF_DT + dd * SSD_H + g * SSD_HG, SSD_HG) for dd in range(2)], axis=1)
        ws.append(jnp.concatenate([
            _cols(w_in, _OFF_Z + g * SSD_GW, SSD_GW),
            _cols(w_in, _OFF_XBC + g * SSD_GW, SSD_GW),
            _cols(w_in, _OFF_XBC + D_MODEL + g * SSD_N, SSD_N),
            _cols(w_in, _OFF_XBC + D_MODEL + SSD_G * SSD_N + g * SSD_N, SSD_N),
            _pad_cols(dt_cols, LANES)], axis=1))
        pick = lambda a: jnp.concatenate([
            _cols(a, g * SSD_GW, SSD_GW),
            _cols(a, D_MODEL + g * SSD_N, SSD_N),
            _cols(a, D_MODEL + SSD_G * SSD_N + g * SSD_N, SSD_N)], axis=1)
        cws.append(pick(conv_w))
        cbs.append(pick(conv_b.reshape(1, -1)))
        head_row = lambda a: _pad_cols(
            jnp.concatenate([a[dd, g * SSD_HG:(g + 1) * SSD_HG] for dd in range(2)]).reshape(1, -1), LANES)
        dtbs.append(head_row(dt_bias))
        alogs.append(head_row(a_log))
    dsk = jnp.repeat(d_skip, SSD_HEAD_DIM).reshape(SSD_G, 1, SSD_GW)
    return (jnp.stack(ws).astype(BF16), jnp.stack(cws), jnp.stack(cbs), jnp.stack(dtbs),
            jnp.stack(alogs), dsk, norm_g.reshape(SSD_G, 1, SSD_GW))


def _hgrn2_params(w_in, lb, norm_g):
    ws = [jnp.concatenate([
        _cols(w_in, _OFF_GQ + h * HEAD, HEAD),
        _cols(w_in, _OFF_GF + h * HEAD, HEAD),
        _cols(w_in, _OFF_GF + D_MODEL + h * HEAD, HEAD),
        _cols(w_in, _OFF_GI + h * HEAD, HEAD),
        _cols(w_in, _OFF_GG + h * HEAD, HEAD)], axis=1) for h in range(HG_H)]
    lbs = lb.reshape(2, HG_H, HEAD).swapaxes(0, 1)
    return jnp.stack(ws).astype(BF16), lbs, norm_g.reshape(HG_H, 1, HEAD)


def _mlstm_params(w_in, i_bias, f_bias, norm_g):
    ws, biases = [], []
    for h in range(ML_H):
        gate_cols = jnp.concatenate([
            _cols(w_in, _OFF_MI + h, 1), _cols(w_in, _OFF_MI + ML_H + h, 1),
            _cols(w_in, _OFF_MF + h, 1), _cols(w_in, _OFF_MF + ML_H + h, 1)], axis=1)
        ws.append(jnp.concatenate([
            _cols(w_in, _OFF_MQ + h * HEAD, HEAD),
            _cols(w_in, _OFF_MK + h * HEAD, HEAD),
            _cols(w_in, _OFF_MV + h * HEAD, HEAD),
            _cols(w_in, _OFF_MO + h * HEAD, HEAD),
            _pad_cols(gate_cols, LANES)], axis=1))
        biases.append(_pad_cols(
            jnp.stack([i_bias[0, h], i_bias[1, h], f_bias[0, h], f_bias[1, h]]).reshape(1, 4), LANES))
    return jnp.stack(ws).astype(BF16), jnp.stack(biases), norm_g.reshape(ML_H, 1, HEAD)


def kernel(x, p, norm_mix_g, w_in, ssd_conv_w, ssd_conv_b, ssd_dt_bias, ssd_a_log, ssd_d, ssd_norm_g, hg_lb_raw, hg_norm_g, ml_i_bias, ml_f_bias, ml_norm_g, w_br_ssd, w_br_hg, w_br_ml, w_out, norm_ffn_g, w_up, ffn_conv_w, ffn_conv_b, w_down, w_ple, w_ple_gate, final_norm_g):
    n_b, seq, d = x.shape
    depth = w_in.shape[0]
    lb_soft = jax.nn.softmax(hg_lb_raw.astype(F32), axis=0)
    hg_lb = jnp.cumsum(lb_soft, axis=0) - lb_soft[0:1]
    h = x
    for l in range(depth):
        u2d = _rmsnorm_call(h.reshape(n_b * seq, d), norm_mix_g[l], BF16)
        u = u2d.reshape(n_b, seq, d)
        y_ssd = _ssd_call(u, *_ssd_params(w_in[l], ssd_conv_w[l], ssd_conv_b[l], ssd_dt_bias[l],
                                          ssd_a_log[l], ssd_d[l], ssd_norm_g[l]))
        y_hg = _hgrn2_call(u, *_hgrn2_params(w_in[l], hg_lb[l], hg_norm_g[l]))
        y_ml = _mlstm_call(u, *_mlstm_params(w_in[l], ml_i_bias[l], ml_f_bias[l], ml_norm_g[l]))
        wg = jnp.stack([_cols(w_in[l], _OFF_GATES + r * d, d) for r in range(3)]).astype(BF16)
        wb = jnp.stack([w_br_ssd[l], w_br_hg[l], w_br_ml[l]]).astype(BF16)
        tok = lambda a: a.reshape(n_b * seq, d)
        h = _merge_call(u2d, tok(h), tok(y_ssd), tok(y_hg), tok(y_ml), wg, wb,
                        w_out[l].astype(BF16)).reshape(n_b, seq, d)
        h = _ffn_call(h, p[l], norm_ffn_g[l], w_up[l].astype(BF16), ffn_conv_w[l], ffn_conv_b[l],
                      w_down[l].astype(BF16), w_ple[l].astype(BF16), w_ple_gate[l].astype(BF16),
                      final_norm_g, final_norm=(l == depth - 1))
    return h
```

```python
import functools

import jax
import jax.numpy as jnp
from jax import lax
from jax.experimental import pallas as pl
from jax.experimental.pallas import tpu as pltpu

F32 = jnp.float32
BF16 = jnp.bfloat16

D_MODEL = 1024
PLE_DIM = 256
EPS = 1e-6
NEG = -1e30
TINY = 1e-30

SSD_HEAD_DIM = 64
SSD_H = 16
SSD_G = 2
SSD_HG = 8
SSD_N = 128
SSD_CONV = 5
SSD_GW = SSD_HG * SSD_HEAD_DIM
SSD_XBC = D_MODEL + 2 * SSD_G * SSD_N
HG_H = 8
ML_H = 8
HEAD = 128
D_FF = 2816
FFN_CONV = 3

LANES = 128
SUBLANES = 8
VMEM_LIMIT = 56 * 1024 * 1024

SSD_CHUNK = 128
HG_CHUNK = 64
ML_CHUNK = 128
SSD_UNROLL = 8
HG_UNROLL = 16
HG_GROUP = 16
ML_GROUP = 8
HG_SAFE_DECAY = 60.0

_OFF_Z = 0
_OFF_XBC = 1024
_OFF_DT = _OFF_XBC + SSD_XBC
_OFF_GQ = _OFF_DT + 2 * SSD_H
_OFF_GF = _OFF_GQ + 1024
_OFF_GI = _OFF_GF + 2048
_OFF_GG = _OFF_GI + 1024
_OFF_MQ = _OFF_GG + 1024
_OFF_MK = _OFF_MQ + 1024
_OFF_MV = _OFF_MK + 1024
_OFF_MI = _OFF_MV + 1024
_OFF_MF = _OFF_MI + 2 * ML_H
_OFF_MO = _OFF_MF + 2 * ML_H
_OFF_GATES = _OFF_MO + 1024


def _dot(a, b):
    return jnp.dot(a, b, preferred_element_type=F32)


def _dot_nt(a, b):
    return lax.dot_general(a, b, (((1,), (1,)), ((), ())), preferred_element_type=F32)


def _dot_tn(a, b):
    return lax.dot_general(a, b, (((0,), (0,)), ((), ())), preferred_element_type=F32)


def _sigmoid(x):
    return 1.0 / (1.0 + jnp.exp(-x))


def _silu(x):
    return x * _sigmoid(x)


def _softplus(x):
    return jnp.maximum(x, 0.0) + jnp.log1p(jnp.exp(-jnp.abs(x)))


def _chunk_scan(x, chunk, reverse, op):
    n_rows = x.shape[0]
    row = lax.broadcasted_iota(jnp.int32, x.shape, 0) & (chunk - 1)
    sh = 1
    while sh < chunk:
        if reverse:
            shifted = pltpu.roll(x, n_rows - sh, axis=0)
            ok = row < chunk - sh
        else:
            shifted = pltpu.roll(x, sh, axis=0)
            ok = row >= sh
        if op == "add":
            x = x + jnp.where(ok, shifted, 0.0)
        else:
            x = jnp.maximum(x, jnp.where(ok, shifted, NEG))
        sh *= 2
    return x


def _lane_scan(x, seg, reverse, op, start=1):
    n_lanes = x.shape[1]
    pos = lax.broadcasted_iota(jnp.int32, x.shape, 1) & (seg - 1)
    sh = start
    while sh < seg:
        if reverse:
            shifted = pltpu.roll(x, n_lanes - sh, axis=1)
            ok = pos < seg - sh
        else:
            shifted = pltpu.roll(x, sh, axis=1)
            ok = pos >= sh
        if op == "add":
            x = x + jnp.where(ok, shifted, 0.0)
        else:
            x = jnp.maximum(x, jnp.where(ok, shifted, NEG))
        sh *= 2
    return x


def _tri_mask(n, reverse):
    t = lax.broadcasted_iota(jnp.int32, (n, n), 0)
    s = lax.broadcasted_iota(jnp.int32, (n, n), 1)
    return (s >= t) if reverse else (s <= t)


def _rmsnorm_kernel(h_ref, g_ref, o_ref):
    x = h_ref[...]
    ms = jnp.mean(x * x, axis=-1, keepdims=True)
    o_ref[...] = (x * lax.rsqrt(ms + EPS) * g_ref[...]).astype(o_ref.dtype)


def _rmsnorm_call(h2d, g, out_dtype, tm=1024):
    n_tok, d = h2d.shape
    tm = min(tm, n_tok)
    return pl.pallas_call(
        _rmsnorm_kernel,
        out_shape=jax.ShapeDtypeStruct((n_tok, d), out_dtype),
        grid=(n_tok // tm,),
        in_specs=[pl.BlockSpec((tm, d), lambda i: (i, 0)),
                  pl.BlockSpec((1, d), lambda i: (0, 0))],
        out_specs=pl.BlockSpec((tm, d), lambda i: (i, 0)),
        compiler_params=pltpu.CompilerParams(dimension_semantics=("parallel",),
                                             vmem_limit_bytes=VMEM_LIMIT),
        name="rmsnorm",
    )(h2d, g.reshape(1, d))


def _hg_att_safe(qc, kc, lfc, reverse):
    n = qc.shape[0]
    t = lax.broadcasted_iota(jnp.int32, (n, n), 0)
    s = lax.broadcasted_iota(jnp.int32, (n, n), 1)
    att = jnp.where(t == s, jnp.sum(qc * kc, axis=-1, keepdims=True), 0.0)
    half = 1
    while half < n:
        pre = _chunk_scan(lfc, half, False, "add")
        suf = _chunk_scan(lfc, half, True, "add")
        t_hi = (t & half) != 0
        s_hi = (s & half) != 0
        same_block = (t ^ s) < 2 * half
        if reverse:
            qe = qc * jnp.exp(suf)
            ke = kc * jnp.exp(pre - lfc)
            pair = jnp.logical_and(same_block, jnp.logical_and(jnp.logical_not(t_hi), s_hi))
        else:
            qe = qc * jnp.exp(pre)
            ke = kc * jnp.exp(suf - lfc)
            pair = jnp.logical_and(same_block, jnp.logical_and(t_hi, jnp.logical_not(s_hi)))
        att = att + jnp.where(pair, _dot_nt(qe.astype(BF16), ke.astype(BF16)), 0.0)
        half *= 2
    return att


def _hgrn2_kernel(u_ref, w_ref, lb_ref, ng_ref, o_ref,
                  q_s, v_s, k_s, lf_s, b_s, qe_s, ke_s, qi_s, kd_s, dec_s, upd_s, st_s, acc_s,
                  *, seq, chunk):
    n_chunks = seq // chunk
    proj = _dot(u_ref[...], w_ref[...])
    q_s[...] = _silu(proj[:, 0:HEAD])
    v_s[...] = proj[:, 3 * HEAD:4 * HEAD].astype(BF16)
    o_ref[...] = proj[:, 4 * HEAD:5 * HEAD]

    for d in range(2):
        f_raw = proj[:, (1 + d) * HEAD:(2 + d) * HEAD]
        lb = lb_ref[d:d + 1, :]
        sig_f = _sigmoid(f_raw)
        f = lb + (1.0 - lb) * sig_f
        k_s[d] = (1.0 - lb) * (1.0 - sig_f)
        log_f = jnp.log(jnp.maximum(f, TINY))
        lf_s[d] = log_f
        b = _chunk_scan(log_f, chunk, d == 1, "add")
        b_s[d] = b

    def ref_span(d):
        first, mid, last = (b_s[d, pl.ds(r, n_chunks, stride=chunk), :] for r in (0, chunk // 2, chunk - 1))
        return jnp.maximum(jnp.abs(first - mid), jnp.abs(mid - last))

    mild_decay = jnp.max(jnp.maximum(ref_span(0), ref_span(1))) <= HG_SAFE_DECAY

    for d in range(2):
        reverse = d == 1
        mask = _tri_mask(chunk, reverse)
        end_row = 0 if reverse else chunk - 1

        def chunk_rows(c):
            r0 = pl.multiple_of(c * chunk, chunk)
            return r0, pl.ds(r0, chunk)

        def scale_body(c, carry, d=d, end_row=end_row):
            r0, rows = chunk_rows(c)
            qc = q_s[rows, :]
            kc = k_s[d, rows, :]
            bc = b_s[d, rows, :]
            ref = b_s[d, pl.ds(r0 + chunk // 2, 1), :]
            bend = b_s[d, pl.ds(r0 + end_row, 1), :]
            qe_s[rows, :] = (qc * jnp.exp(bc - ref)).astype(BF16)
            ke_s[rows, :] = (kc * jnp.exp(ref - bc)).astype(BF16)
            qi_s[rows, :] = (qc * jnp.exp(bc)).astype(BF16)
            kd_s[rows, :] = (kc * jnp.exp(bend - bc)).astype(BF16)
            dec_s[c] = jnp.broadcast_to(jnp.exp(bend), (HEAD, HEAD)).T
            return carry

        lax.fori_loop(0, n_chunks, scale_body, 0, unroll=HG_UNROLL)

        for c in range(n_chunks):
            rows = slice(c * chunk, (c + 1) * chunk)
            upd_s[c] = _dot_tn(kd_s[rows, :], v_s[rows, :])

        def rec_body(ci, st, reverse=reverse):
            c = (n_chunks - 1 - ci) if reverse else ci
            st_s[c] = st.astype(BF16)
            return st * dec_s[c] + upd_s[c]

        lax.fori_loop(0, n_chunks, rec_body, jnp.zeros((HEAD, HEAD), F32))

        def emit(c, rows, att, d=d):
            o = _dot(jnp.concatenate([qi_s[rows, :], att.astype(BF16)], axis=1),
                     jnp.concatenate([st_s[c], v_s[rows, :]], axis=0))
            if d == 0:
                acc_s[rows, :] = o
            else:
                acc_s[rows, :] = acc_s[rows, :] + o

        @pl.when(mild_decay)
        def _(mask=mask):
            for g0 in range(0, n_chunks, HG_GROUP):
                group = [(c, slice(c * chunk, (c + 1) * chunk)) for c in range(g0, min(g0 + HG_GROUP, n_chunks))]
                atts = [jnp.where(mask, _dot_nt(qe_s[rows, :], ke_s[rows, :]), 0.0) for _, rows in group]
                for (c, rows), att in zip(group, atts):
                    emit(c, rows, att)

        @pl.when(jnp.logical_not(mild_decay))
        def _(d=d, reverse=reverse):
            def safe_body(c, carry):
                _, rows = chunk_rows(c)
                emit(c, rows, _hg_att_safe(q_s[rows, :], k_s[d, rows, :], lf_s[d, rows, :], reverse))
                return carry

            lax.fori_loop(0, n_chunks, safe_body, 0)

    o = acc_s[...]
    o = o * lax.rsqrt(jnp.mean(o * o, axis=-1, keepdims=True) + EPS) * ng_ref[...]
    o_ref[...] = o * _silu(o_ref[...])


def _hgrn2_call(u, w, lb, ng, *, chunk=HG_CHUNK):
    n_b, seq, d = u.shape
    kern = functools.partial(_hgrn2_kernel, seq=seq, chunk=chunk)
    return pl.pallas_call(
        kern,
        out_shape=jax.ShapeDtypeStruct((n_b, seq, HG_H * HEAD), F32),
        grid=(n_b, HG_H),
        in_specs=[pl.BlockSpec((None, seq, d), lambda b, h: (b, 0, 0)),
                  pl.BlockSpec((None, d, 5 * HEAD), lambda b, h: (h, 0, 0)),
                  pl.BlockSpec((None, 2, HEAD), lambda b, h: (h, 0, 0)),
                  pl.BlockSpec((None, 1, HEAD), lambda b, h: (h, 0, 0))],
        out_specs=pl.BlockSpec((None, seq, HEAD), lambda b, h: (b, 0, h)),
        scratch_shapes=[pltpu.VMEM((seq, HEAD), F32),
                        pltpu.VMEM((seq, HEAD), BF16),
                        pltpu.VMEM((2, seq, HEAD), F32),
                        pltpu.VMEM((2, seq, HEAD), F32),
                        pltpu.VMEM((2, seq, HEAD), F32),
                        pltpu.VMEM((seq, HEAD), BF16),
                        pltpu.VMEM((seq, HEAD), BF16),
                        pltpu.VMEM((seq, HEAD), BF16),
                        pltpu.VMEM((seq, HEAD), BF16),
                        pltpu.VMEM((seq // chunk, HEAD, HEAD), F32),
                        pltpu.VMEM((seq // chunk, HEAD, HEAD), F32),
                        pltpu.VMEM((seq // chunk, HEAD, HEAD), BF16),
                        pltpu.VMEM((seq, HEAD), F32)],
        compiler_params=pltpu.CompilerParams(dimension_semantics=("parallel", "arbitrary"),
                                             vmem_limit_bytes=VMEM_LIMIT),
        name="hgrn2_branch",
    )(u, w, lb, ng)


def _log_sigmoid(x):
    return jnp.minimum(x, 0.0) - jnp.log1p(jnp.exp(-jnp.abs(x)))


def _mlstm_kernel(u_ref, w_ref, bias_ref, ng_ref, o_ref,
                  qb_s, kb_s, va_s, tm_s, qw_s, kw_s, mx_s, en_s, gt_s, dec_s, upd_s, ct_s,
                  acc_s, *, seq, chunk):
    n_chunks = seq // chunk
    proj = _dot(u_ref[...], w_ref[...])
    q = proj[:, 0:HEAD]
    k = proj[:, HEAD:2 * HEAD] * (HEAD ** -0.5)
    qb_s[...] = q.astype(BF16)
    kb_s[...] = k.astype(BF16)
    va_s[:, 0:HEAD] = proj[:, 2 * HEAD:3 * HEAD].astype(BF16)
    va_s[:, HEAD:2 * HEAD] = jnp.ones((seq, HEAD), BF16)
    o_ref[...] = proj[:, 3 * HEAD:4 * HEAD]

    x = (proj[:, 4 * HEAD:5 * HEAD] + bias_ref[...]).T[0:SUBLANES, :]
    fwd_row = lax.broadcasted_iota(jnp.int32, (SUBLANES, seq), 0) == 0
    pos = lax.broadcasted_iota(jnp.int32, (SUBLANES, seq), 1)

    def scans(y, op, seg, start=1):
        return _lane_scan(y, seg, False, op, start), _lane_scan(y, seg, True, op, start)

    it = x
    log_f = pltpu.roll(_log_sigmoid(x), SUBLANES - 2, axis=0)
    pre, suf = scans(log_f, "add", chunk)
    bt = jnp.where(fwd_row, pre, suf)
    bt_end = pre + suf - log_f
    g = it - bt
    g_pre, g_suf = scans(g, "max", chunk)
    cm = jnp.where(fwd_row, g_pre, g_suf)
    cm_end = jnp.maximum(g_pre, g_suf)

    p_pre, p_suf = scans(bt_end, "add", seq, chunk)
    p_sum = jnp.where(fwd_row, p_pre, p_suf)
    x_pre, x_suf = scans(cm_end - (p_sum - bt_end), "max", seq, chunk)
    m_end = p_sum + jnp.where(fwd_row, x_pre, x_suf)
    m_prev = jnp.where(fwd_row,
                       jnp.where(pos < chunk, NEG, pltpu.roll(m_end, chunk, axis=1)),
                       jnp.where(pos >= seq - chunk, NEG, pltpu.roll(m_end, seq - chunk, axis=1)))
    mx = jnp.maximum(m_prev, cm)
    w_inter = jnp.exp(m_prev - mx)
    wk = jnp.exp(bt_end - bt + it - m_end)
    en = jnp.exp(-(bt + mx))
    decay = jnp.exp(bt_end + m_prev - m_end)

    for c in range(n_chunks):
        gt_s[c] = g[:, c * chunk:(c + 1) * chunk]
    row = lax.broadcasted_iota(jnp.int32, (SUBLANES, seq), 0)
    packed = jnp.where(row < 2, w_inter,
                       jnp.where(row < 4, pltpu.roll(wk, 2, axis=0),
                                 jnp.where(row < 6, pltpu.roll(mx, 4, axis=0), pltpu.roll(en, 6, axis=0))))
    tm_s[...] = jnp.concatenate(
        [packed, decay, jnp.zeros((LANES - 2 * SUBLANES, seq), F32)], axis=0).T
    for d in range(2):
        def lanes(col, ref_rows=slice(None)):
            block = tm_s[ref_rows, :]
            return jnp.broadcast_to(block[:, col:col + 1], (block.shape[0], LANES))
        qw_s[d] = (q * lanes(d)).astype(BF16)
        kw_s[d] = (k * lanes(2 + d)).astype(BF16)
        mx_s[d] = lanes(4 + d)
        en_s[d] = lanes(6 + d)
        dec_s[d] = lanes(8 + d, pl.ds(0, n_chunks, stride=chunk))

    for d in range(2):
        reverse = d == 1
        mask = _tri_mask(chunk, reverse)

        def chunk_rows(c):
            return pl.ds(pl.multiple_of(c * chunk, chunk), chunk)

        for c in range(n_chunks):
            rows = slice(c * chunk, (c + 1) * chunk)
            upd_s[c] = _dot_tn(kw_s[d, rows, :], va_s[rows, :])

        def rec_body(ci, ct, reverse=reverse, d=d):
            c = (n_chunks - 1 - ci) if reverse else ci
            ct_s[c] = ct.astype(BF16)
            dec = dec_s[d, pl.ds(c, 1), :]
            return ct * jnp.concatenate([dec, dec], axis=1) + upd_s[c]

        lax.fori_loop(0, n_chunks, rec_body, jnp.zeros((HEAD, 2 * HEAD), F32))

        def scores(c, rows):
            p = jnp.exp(jnp.where(mask, gt_s[c][d:d + 1, :] - mx_s[d, rows, :], NEG))
            return (_dot_nt(qb_s[rows, :], kb_s[rows, :]) * p).astype(BF16)

        def emit(c, rows, sc):
            tot = _dot(jnp.concatenate([qw_s[d, rows, :], sc], axis=1),
                       jnp.concatenate([ct_s[c], va_s[rows, :]], axis=0))
            hval = tot[:, 0:HEAD] / jnp.maximum(jnp.abs(tot[:, HEAD:2 * HEAD]), en_s[d, rows, :])
            if d == 0:
                acc_s[rows, :] = hval
            else:
                acc_s[rows, :] = acc_s[rows, :] + hval

        for g0 in range(0, n_chunks, ML_GROUP):
            group = [(c, slice(c * chunk, (c + 1) * chunk)) for c in range(g0, min(g0 + ML_GROUP, n_chunks))]
            staged = [scores(c, rows) for c, rows in group]
            for (c, rows), sc in zip(group, staged):
                emit(c, rows, sc)

    hsum = acc_s[...]
    hc = hsum - jnp.mean(hsum, axis=-1, keepdims=True)
    hn = hc * lax.rsqrt(jnp.mean(hc * hc, axis=-1, keepdims=True) + EPS) * ng_ref[...]
    o_ref[...] = hn * _sigmoid(o_ref[...])


def _mlstm_call(u, w, bias, ng, *, chunk=ML_CHUNK):
    n_b, seq, d = u.shape
    n_chunks = seq // chunk
    kern = functools.partial(_mlstm_kernel, seq=seq, chunk=chunk)
    return pl.pallas_call(
        kern,
        out_shape=jax.ShapeDtypeStruct((n_b, seq, ML_H * HEAD), F32),
        grid=(n_b, ML_H),
        in_specs=[pl.BlockSpec((None, seq, d), lambda b, h: (b, 0, 0)),
                  pl.BlockSpec((None, d, 5 * HEAD), lambda b, h: (h, 0, 0)),
                  pl.BlockSpec((None, 1, HEAD), lambda b, h: (h, 0, 0)),
                  pl.BlockSpec((None, 1, HEAD), lambda b, h: (h, 0, 0))],
        out_specs=pl.BlockSpec((None, seq, HEAD), lambda b, h: (b, 0, h)),
        scratch_shapes=[pltpu.VMEM((seq, HEAD), BF16),
                        pltpu.VMEM((seq, HEAD), BF16),
                        pltpu.VMEM((seq, 2 * HEAD), BF16),
                        pltpu.VMEM((seq, LANES), F32),
                        pltpu.VMEM((2, seq, HEAD), BF16),
                        pltpu.VMEM((2, seq, HEAD), BF16),
                        pltpu.VMEM((2, seq, LANES), F32),
                        pltpu.VMEM((2, seq, LANES), F32),
                        pltpu.VMEM((n_chunks, SUBLANES, chunk), F32),
                        pltpu.VMEM((2, n_chunks, LANES), F32),
                        pltpu.VMEM((n_chunks, HEAD, 2 * HEAD), F32),
                        pltpu.VMEM((n_chunks, HEAD, 2 * HEAD), BF16),
                        pltpu.VMEM((seq, HEAD), F32)],
        compiler_params=pltpu.CompilerParams(dimension_semantics=("parallel", "arbitrary"),
                                             vmem_limit_bytes=VMEM_LIMIT),
        name="mlstm_branch",
    )(u, w, bias, ng)


_SSD_W_Z = 0
_SSD_W_XBC = SSD_GW
_SSD_W_DT = SSD_GW + SSD_GW + 2 * SSD_N
_SSD_W_END = _SSD_W_DT + LANES
_SSD_XBC_G = SSD_GW + 2 * SSD_N
_SSD_PAD = SUBLANES
_SSD_PAIRS = SSD_HG // 2


def _ssd_kernel(u_ref, w_ref, cw_ref, cb_ref, dtb_ref, alog_ref, dsk_ref, ng_ref, o_ref,
                pad_s, xs_s, b_s, c_s, a_s, at_s, dtt_s, cbm_s, ea_s, xw_s, dec_s, cur_s, st_s,
                *, seq, chunk, rblk):
    n_chunks = seq // chunk
    n_rblk = seq // rblk

    zero_rows = jnp.zeros((_SSD_PAD, _SSD_XBC_G), F32)
    pad_s[0:_SSD_PAD, :] = zero_rows
    pad_s[_SSD_PAD + seq:2 * _SSD_PAD + seq, :] = zero_rows
    for i in range(n_rblk):
        r0 = i * rblk
        pad_s[_SSD_PAD + r0:_SSD_PAD + r0 + rblk, :] = _dot(
            u_ref[r0:r0 + rblk, :], w_ref[:, _SSD_W_XBC:_SSD_W_DT])
    for i in range(n_rblk):
        r0 = i * rblk
        acc = cb_ref[...]
        slab = pad_s[r0:r0 + rblk + 2 * _SSD_PAD, :]
        for j in range(SSD_CONV):
            tap = slab if j == SSD_CONV // 2 else pltpu.roll(
                slab, (SSD_CONV // 2 - j) % (rblk + 2 * _SSD_PAD), axis=0)
            acc = acc + tap[_SSD_PAD:_SSD_PAD + rblk, :] * cw_ref[j:j + 1, :]
        xbc = _silu(acc)
        xs_s[r0:r0 + rblk, :] = xbc[:, 0:SSD_GW]
        b_s[r0:r0 + rblk, :] = xbc[:, SSD_GW:SSD_GW + SSD_N].astype(BF16)
        c_s[r0:r0 + rblk, :] = xbc[:, SSD_GW + SSD_N:SSD_GW + 2 * SSD_N].astype(BF16)

    n_hd = 2 * SSD_HG
    dt_tok = _softplus(_dot(u_ref[...], w_ref[:, _SSD_W_DT:_SSD_W_END]) + dtb_ref[...])
    dt = dt_tok.T[0:n_hd, :]
    log_a = (dt_tok * (-jnp.exp(alog_ref[...]))).T[0:n_hd, :]
    fwd_row = lax.broadcasted_iota(jnp.int32, (n_hd, seq), 0) < SSD_HG
    pre = _lane_scan(log_a, chunk, False, "add")
    suf = _lane_scan(log_a, chunk, True, "add")
    acum = jnp.where(fwd_row, pre, suf)
    a_end = pre + suf - log_a
    for c in range(n_chunks):
        lanes_c = slice(c * chunk, (c + 1) * chunk)
        at_s[c] = acum[:, lanes_c]
        dtt_s[c] = dt[:, lanes_c]
        r0 = c * chunk
        cbm_s[r0:r0 + chunk, :] = _dot_nt(c_s[r0:r0 + chunk, :], b_s[r0:r0 + chunk, :])
    a_s[...] = jnp.concatenate(
        [acum, jnp.exp(acum), dt * jnp.exp(a_end - acum), jnp.exp(a_end),
         jnp.zeros((LANES - 4 * n_hd, seq), F32)], axis=0).T
    lane_lo = lax.broadcasted_iota(jnp.int32, (chunk, LANES), 1) < SSD_HEAD_DIM

    def split3(x):
        hi = x.astype(BF16)
        r1 = x - hi.astype(F32)
        mid = r1.astype(BF16)
        return [hi, mid, (r1 - mid.astype(F32)).astype(BF16)]

    def spread(first_lane, pieces):
        one = (lax.broadcasted_iota(jnp.int32, (LANES, SSD_GW), 1) // SSD_HEAD_DIM
               == lax.broadcasted_iota(jnp.int32, (LANES, SSD_GW), 0) - first_lane).astype(BF16)
        return jnp.concatenate([one] * pieces, axis=0)

    for d in range(2):
        reverse = d == 1
        mask = _tri_mask(chunk, reverse)
        spread_ea = spread(n_hd + d * SSD_HG, 1)
        spread_wt = spread(2 * n_hd + d * SSD_HG, 2)
        dec_s[...] = _dot(jnp.concatenate(split3(a_s[pl.ds(0, n_chunks, stride=chunk), :]), axis=1),
                          spread(3 * n_hd + d * SSD_HG, 3))
        for i in range(n_rblk):
            rows = slice(i * rblk, (i + 1) * rblk)
            hi, mid, _ = split3(a_s[rows, :])
            ea_s[rows, :] = _dot(hi, spread_ea).astype(BF16)
            xw_s[rows, :] = (xs_s[rows, :] * _dot(jnp.concatenate([hi, mid], axis=1), spread_wt)).astype(BF16)

        def chunk_rows(c):
            return pl.ds(pl.multiple_of(c * chunk, chunk), chunk)

        for c in range(n_chunks):
            rows = slice(c * chunk, (c + 1) * chunk)
            pad_s[rows, 0:SSD_GW] = _dot_tn(b_s[rows, :], xw_s[rows, :])

        cur_s[...] = jnp.zeros_like(cur_s)

        def rec_body(ci, carry, reverse=reverse):
            c = (n_chunks - 1 - ci) if reverse else ci
            cur = cur_s[...]
            st_s[c] = cur.astype(BF16)
            cur_s[...] = cur * dec_s[pl.ds(c, 1), :] + pad_s[chunk_rows(c), 0:SSD_GW]
            return carry

        lax.fori_loop(0, n_chunks, rec_body, 0)

        def out_body(c, carry, mask=mask, d=d):
            rows = chunk_rows(c)
            blk = a_s[rows, :]
            blk_t = at_s[c]
            dtt = dtt_s[c]
            cbm = cbm_s[rows, :]
            y_inter = _dot(c_s[rows, :], st_s[c])
            for p in range(_SSD_PAIRS):
                cols = slice(p * LANES, (p + 1) * LANES)
                xs_pair = xs_s[rows, cols]
                m_tiles, x_tiles = [], []
                for hh in range(2):
                    col = d * SSD_HG + 2 * p + hh
                    seg = jnp.exp(jnp.where(mask, blk[:, col:col + 1] - blk_t[col:col + 1, :], NEG))
                    m_tiles.append((cbm * seg * dtt[col:col + 1, :]).astype(BF16))
                    keep = lane_lo if hh == 0 else jnp.logical_not(lane_lo)
                    x_tiles.append(jnp.where(keep, xs_pair, 0.0).astype(BF16))
                y = (y_inter[:, cols] * ea_s[rows, cols].astype(F32)
                     + _dot(jnp.concatenate(m_tiles, axis=1), jnp.concatenate(x_tiles, axis=0)))
                if d == 0:
                    o_ref[rows, cols] = y
                else:
                    o_ref[rows, cols] = o_ref[rows, cols] + y
            return carry

        lax.fori_loop(0, n_chunks, out_body, 0, unroll=SSD_UNROLL)

    for i in range(n_rblk):
        r0 = i * rblk
        rows = slice(r0, r0 + rblk)
        z = _dot(u_ref[rows, :], w_ref[:, _SSD_W_Z:_SSD_W_XBC])
        y = (o_ref[rows, :] + xs_s[rows, :] * dsk_ref[...]) * _silu(z)
        o_ref[rows, :] = y * lax.rsqrt(jnp.mean(y * y, axis=-1, keepdims=True) + EPS) * ng_ref[...]


def _ssd_call(u, w, cw, cb, dtb, alog, dsk, ng, *, chunk=SSD_CHUNK):
    n_b, seq, d = u.shape
    rblk = min(seq, 512)
    kern = functools.partial(_ssd_kernel, seq=seq, chunk=chunk, rblk=rblk)
    vec = lambda n: pl.BlockSpec((None, 1, n), lambda b, g: (g, 0, 0))
    return pl.pallas_call(
        kern,
        out_shape=jax.ShapeDtypeStruct((n_b, seq, SSD_G * SSD_GW), F32),
        grid=(n_b, SSD_G),
        in_specs=[pl.BlockSpec((None, seq, d), lambda b, g: (b, 0, 0)),
                  pl.BlockSpec((None, d, _SSD_W_END), lambda b, g: (g, 0, 0)),
                  pl.BlockSpec((None, SSD_CONV, _SSD_XBC_G), lambda b, g: (g, 0, 0)),
                  vec(_SSD_XBC_G), vec(LANES), vec(LANES), vec(SSD_GW), vec(SSD_GW)],
        out_specs=pl.BlockSpec((None, seq, SSD_GW), lambda b, g: (b, 0, g)),
        scratch_shapes=[pltpu.VMEM((seq + 2 * _SSD_PAD, _SSD_XBC_G), F32),
                        pltpu.VMEM((seq, SSD_GW), F32),
                        pltpu.VMEM((seq, SSD_N), BF16),
                        pltpu.VMEM((seq, SSD_N), BF16),
                        pltpu.VMEM((seq, LANES), F32),
                        pltpu.VMEM((seq // chunk, 2 * SSD_HG, chunk), F32),
                        pltpu.VMEM((seq // chunk, 2 * SSD_HG, chunk), F32),
                        pltpu.VMEM((seq, chunk), F32),
                        pltpu.VMEM((seq, SSD_GW), BF16),
                        pltpu.VMEM((seq, SSD_GW), BF16),
                        pltpu.VMEM((seq // chunk, SSD_GW), F32),
                        pltpu.VMEM((SSD_N, SSD_GW), F32),
                        pltpu.VMEM((seq // chunk, SSD_N, SSD_GW), BF16)],
        compiler_params=pltpu.CompilerParams(dimension_semantics=("parallel", "arbitrary"),
                                             vmem_limit_bytes=VMEM_LIMIT),
        name="ssd_branch",
    )(u, w, cw, cb, dtb, alog, dsk, ng)


def _merge_kernel(u_ref, h_ref, y0_ref, y1_ref, y2_ref, wg_ref, wb_ref, wo_ref, o_ref):
    u = u_ref[...]
    merged = None
    for r, y_ref in enumerate((y0_ref, y1_ref, y2_ref)):
        term = _sigmoid(_dot(u, wg_ref[r])) * _dot(y_ref[...].astype(BF16), wb_ref[r])
        merged = term if merged is None else merged + term
    o_ref[...] = h_ref[...] + _dot(merged.astype(BF16), wo_ref[...])


def _merge_call(u2d, h2d, y0, y1, y2, wg, wb, wo, tm=512):
    n_tok, d = h2d.shape
    tm = min(tm, n_tok)
    tok = lambda: pl.BlockSpec((tm, d), lambda i: (i, 0))
    held = lambda shape: pl.BlockSpec(shape, lambda i: tuple(0 for _ in shape), pipeline_mode=pl.Buffered(1))
    return pl.pallas_call(
        _merge_kernel,
        out_shape=jax.ShapeDtypeStruct((n_tok, d), F32),
        grid=(n_tok // tm,),
        in_specs=[tok(), tok(), tok(), tok(), tok(),
                  held((3, d, d)), held((3, d, d)), held((d, d))],
        out_specs=tok(),
        compiler_params=pltpu.CompilerParams(dimension_semantics=("parallel",),
                                             vmem_limit_bytes=VMEM_LIMIT),
        name="merge",
    )(u2d, h2d, y0, y1, y2, wg, wb, wo)


_FFN_CW = 256
_FFN_NJ = D_FF // _FFN_CW
_FFN_HALO = SUBLANES

def _ffn_kernel(h_ref, hp_ref, hn_ref, p_ref, g_ref, wa_ref, wv_ref, cwa_ref, cwv_ref,
                cba_ref, cbv_ref, wd_ref, wple_ref, wpg_ref, fg_ref, o_ref, u_s,
                *, tm, n_tiles, final_norm):
    i = pl.program_id(1)
    j = pl.program_id(2)

    def norm(x):
        return (x * lax.rsqrt(jnp.mean(x * x, axis=-1, keepdims=True) + EPS) * g_ref[...]).astype(BF16)

    @pl.when(j == 0)
    def _():
        up = jnp.where(i > 0, 1.0, 0.0)
        dn = jnp.where(i < n_tiles - 1, 1.0, 0.0)
        u_s[0:_FFN_HALO, :] = norm(hp_ref[...] * up)
        u_s[_FFN_HALO:_FFN_HALO + tm, :] = norm(h_ref[...])
        u_s[_FFN_HALO + tm:2 * _FFN_HALO + tm, :] = norm(hn_ref[...] * dn)
        o_ref[...] = h_ref[...]

    def conv(w_ref, cw_ref, cb_ref):
        up = _dot(u_s[...], w_ref[...])
        n_rows = tm + 2 * _FFN_HALO
        acc = cb_ref[...]
        for t in range(FFN_CONV):
            tap = up if t == FFN_CONV // 2 else pltpu.roll(up, (FFN_CONV // 2 - t) % n_rows, axis=0)
            acc = acc + tap[_FFN_HALO:_FFN_HALO + tm, :] * cw_ref[t:t + 1, :]
        return acc

    act = (_silu(conv(wa_ref, cwa_ref, cba_ref)) * conv(wv_ref, cwv_ref, cbv_ref)).astype(BF16)
    o_ref[...] += _dot(act, wd_ref[...])

    @pl.when(j == _FFN_NJ - 1)
    def _():
        h2 = o_ref[...]
        ple = _dot(p_ref[...].astype(BF16), wple_ref[...])
        h3 = h2 + ple * _sigmoid(_dot(h2.astype(BF16), wpg_ref[...]))
        if final_norm:
            h3 = h3 * lax.rsqrt(jnp.mean(h3 * h3, axis=-1, keepdims=True) + EPS) * fg_ref[...]
        o_ref[...] = h3


def _ffn_call(h, p, g, w_up, cw, cb, w_down, w_ple, w_pg, fg, *, final_norm, tm=1024):
    n_b, seq, d = h.shape
    tm = min(tm, seq)
    n_tiles = seq // tm
    hb = tm // _FFN_HALO
    n_hb = seq // _FFN_HALO
    kern = functools.partial(_ffn_kernel, tm=tm, n_tiles=n_tiles, final_norm=final_norm)
    cst = lambda shape: pl.BlockSpec(shape, lambda b, i, j: tuple(0 for _ in shape))
    return pl.pallas_call(
        kern,
        out_shape=jax.ShapeDtypeStruct((n_b, seq, d), F32),
        grid=(n_b, n_tiles, _FFN_NJ),
        in_specs=[pl.BlockSpec((None, tm, d), lambda b, i, j: (b, i, 0)),
                  pl.BlockSpec((None, _FFN_HALO, d), lambda b, i, j: (b, jnp.maximum(i * hb - 1, 0), 0)),
                  pl.BlockSpec((None, _FFN_HALO, d), lambda b, i, j: (b, jnp.minimum((i + 1) * hb, n_hb - 1), 0)),
                  pl.BlockSpec((None, tm, PLE_DIM), lambda b, i, j: (b, i, 0)),
                  cst((1, d)),
                  pl.BlockSpec((d, _FFN_CW), lambda b, i, j: (0, j)),
                  pl.BlockSpec((d, _FFN_CW), lambda b, i, j: (0, _FFN_NJ + j)),
                  pl.BlockSpec((FFN_CONV, _FFN_CW), lambda b, i, j: (0, j)),
                  pl.BlockSpec((FFN_CONV, _FFN_CW), lambda b, i, j: (0, _FFN_NJ + j)),
                  pl.BlockSpec((1, _FFN_CW), lambda b, i, j: (0, j)),
                  pl.BlockSpec((1, _FFN_CW), lambda b, i, j: (0, _FFN_NJ + j)),
                  pl.BlockSpec((_FFN_CW, d), lambda b, i, j: (j, 0)),
                  cst((PLE_DIM, d)), cst((d, d)), cst((1, d))],
        out_specs=pl.BlockSpec((None, tm, d), lambda b, i, j: (b, i, 0)),
        scratch_shapes=[pltpu.VMEM((tm + 2 * _FFN_HALO, d), BF16)],
        compiler_params=pltpu.CompilerParams(
            dimension_semantics=("parallel", "parallel", "arbitrary"),
            vmem_limit_bytes=VMEM_LIMIT),
        name="convffn_ple",
    )(h, h, h, p, g.reshape(1, d), w_up, w_up, cw, cw, cb.reshape(1, -1), cb.reshape(1, -1),
      w_down, w_ple, w_pg, fg.reshape(1, d))


def _cols(w, start, width):
    return lax.slice_in_dim(w, start, start + width, axis=1)


def _pad_cols(w, width):
    return jnp.pad(w, ((0, 0), (0, width - w.shape[1])))


def _ssd_params(w_in, conv_w, conv_b, dt_bias, a_log, d_skip, norm_g):
    ws, cws, cbs, dtbs, alogs = [], [], [], [], []
    for g in range(SSD_G):
        dt_cols = jnp.concatenate(
            [_cols(w_in, _OFF_DT + dd * SSD_H + g * SSD_HG, SSD_HG) for dd in range(2)], axis=1)
        ws.append(jnp.concatenate([
            _cols(w_in, _OFF_Z + g * SSD_GW, SSD_GW),
            _cols(w_in, _OFF_XBC + g * SSD_GW, SSD_GW),
            _cols(w_in, _OFF_XBC + D_MODEL + g * SSD_N, SSD_N),
            _cols(w_in, _OFF_XBC + D_MODEL + SSD_G * SSD_N + g * SSD_N, SSD_N),
            _pad_cols(dt_cols, LANES)], axis=1))
        pick = lambda a: jnp.concatenate([
            _cols(a, g * SSD_GW, SSD_GW),
            _cols(a, D_MODEL + g * SSD_N, SSD_N),
            _cols(a, D_MODEL + SSD_G * SSD_N + g * SSD_N, SSD_N)], axis=1)
        cws.append(pick(conv_w))
        cbs.append(pick(conv_b.reshape(1, -1)))
        head_row = lambda a: _pad_cols(
            jnp.concatenate([a[dd, g * SSD_HG:(g + 1) * SSD_HG] for dd in range(2)]).reshape(1, -1), LANES)
        dtbs.append(head_row(dt_bias))
        alogs.append(head_row(a_log))
    dsk = jnp.repeat(d_skip, SSD_HEAD_DIM).reshape(SSD_G, 1, SSD_GW)
    return (jnp.stack(ws).astype(BF16), jnp.stack(cws), jnp.stack(cbs), jnp.stack(dtbs),
            jnp.stack(alogs), dsk, norm_g.reshape(SSD_G, 1, SSD_GW))


def _hgrn2_params(w_in, lb, norm_g):
    ws = [jnp.concatenate([
        _cols(w_in, _OFF_GQ + h * HEAD, HEAD),
        _cols(w_in, _OFF_GF + h * HEAD, HEAD),
        _cols(w_in, _OFF_GF + D_MODEL + h * HEAD, HEAD),
        _cols(w_in, _OFF_GI + h * HEAD, HEAD),
        _cols(w_in, _OFF_GG + h * HEAD, HEAD)], axis=1) for h in range(HG_H)]
    lbs = lb.reshape(2, HG_H, HEAD).swapaxes(0, 1)
    return jnp.stack(ws).astype(BF16), lbs, norm_g.reshape(HG_H, 1, HEAD)


def _mlstm_params(w_in, i_bias, f_bias, norm_g):
    ws, biases = [], []
    for h in range(ML_H):
        gate_cols = jnp.concatenate([
            _cols(w_in, _OFF_MI + h, 1), _cols(w_in, _OFF_MI + ML_H + h, 1),
            _cols(w_in, _OFF_MF + h, 1), _cols(w_in, _OFF_MF + ML_H + h, 1)], axis=1)
        ws.append(jnp.concatenate([
            _cols(w_in, _OFF_MQ + h * HEAD, HEAD),
            _cols(w_in, _OFF_MK + h * HEAD, HEAD),
            _cols(w_in, _OFF_MV + h * HEAD, HEAD),
            _cols(w_in, _OFF_MO + h * HEAD, HEAD),
            _pad_cols(gate_cols, LANES)], axis=1))
        biases.append(_pad_cols(
            jnp.stack([i_bias[0, h], i_bias[1, h], f_bias[0, h], f_bias[1, h]]).reshape(1, 4), LANES))
    return jnp.stack(ws).astype(BF16), jnp.stack(biases), norm_g.reshape(ML_H, 1, HEAD)


def kernel(x, p, norm_mix_g, w_in, ssd_conv_w, ssd_conv_b, ssd_dt_bias, ssd_a_log, ssd_d, ssd_norm_g, hg_lb_raw, hg_norm_g, ml_i_bias, ml_f_bias, ml_norm_g, w_br_ssd, w_br_hg, w_br_ml, w_out, norm_ffn_g, w_up, ffn_conv_w, ffn_conv_b, w_down, w_ple, w_ple_gate, final_norm_g):
    n_b, seq, d = x.shape
    depth = w_in.shape[0]
    lb_soft = jax.nn.softmax(hg_lb_raw.astype(F32), axis=0)
    hg_lb = jnp.cumsum(lb_soft, axis=0) - lb_soft[0:1]
    h = x
    for l in range(depth):
        u2d = _rmsnorm_call(h.reshape(n_b * seq, d), norm_mix_g[l], BF16)
        u = u2d.reshape(n_b, seq, d)
        y_ssd = _ssd_call(u, *_ssd_params(w_in[l], ssd_conv_w[l], ssd_conv_b[l], ssd_dt_bias[l],
                                          ssd_a_log[l], ssd_d[l], ssd_norm_g[l]))
        y_hg = _hgrn2_call(u, *_hgrn2_params(w_in[l], hg_lb[l], hg_norm_g[l]))
        y_ml = _mlstm_call(u, *_mlstm_params(w_in[l], ml_i_bias[l], ml_f_bias[l], ml_norm_g[l]))
        wg = jnp.stack([_cols(w_in[l], _OFF_GATES + r * d, d) for r in range(3)]).astype(BF16)
        wb = jnp.stack([w_br_ssd[l], w_br_hg[l], w_br_ml[l]]).astype(BF16)
        tok = lambda a: a.reshape(n_b * seq, d)
        h = _merge_call(u2d, tok(h), tok(y_ssd), tok(y_hg), tok(y_ml), wg, wb,
                        w_out[l].astype(BF16)).reshape(n_b, seq, d)
        h = _ffn_call(h, p[l], norm_ffn_g[l], w_up[l].astype(BF16), ffn_conv_w[l], ffn_conv_b[l],
                      w_down[l].astype(BF16), w_ple[l].astype(BF16), w_ple_gate[l].astype(BF16),
                      final_norm_g, final_norm=(l == depth - 1))
    return h
```

```python
import functools

import jax
import jax.numpy as jnp
from jax import lax
from jax.experimental import pallas as pl
from jax.experimental.pallas import tpu as pltpu

F32 = jnp.float32
BF16 = jnp.bfloat16

D_MODEL = 1024
PLE_DIM = 256
EPS = 1e-6
NEG = -1e30
TINY = 1e-30

SSD_HEAD_DIM = 64
SSD_H = 16
SSD_G = 2
SSD_HG = 8
SSD_N = 128
SSD_CONV = 5
SSD_GW = SSD_HG * SSD_HEAD_DIM
SSD_XBC = D_MODEL + 2 * SSD_G * SSD_N
HG_H = 8
ML_H = 8
HEAD = 128
D_FF = 2816
FFN_CONV = 3

LANES = 128
SUBLANES = 8
VMEM_LIMIT = 56 * 1024 * 1024

SSD_CHUNK = 128
HG_CHUNK = 64
ML_CHUNK = 128
SSD_UNROLL = 8
HG_UNROLL = 32
HG_GROUP = 16
ML_GROUP = 8
HG_SAFE_DECAY = 60.0

_OFF_Z = 0
_OFF_XBC = 1024
_OFF_DT = _OFF_XBC + SSD_XBC
_OFF_GQ = _OFF_DT + 2 * SSD_H
_OFF_GF = _OFF_GQ + 1024
_OFF_GI = _OFF_GF + 2048
_OFF_GG = _OFF_GI + 1024
_OFF_MQ = _OFF_GG + 1024
_OFF_MK = _OFF_MQ + 1024
_OFF_MV = _OFF_MK + 1024
_OFF_MI = _OFF_MV + 1024
_OFF_MF = _OFF_MI + 2 * ML_H
_OFF_MO = _OFF_MF + 2 * ML_H
_OFF_GATES = _OFF_MO + 1024


def _dot(a, b):
    return jnp.dot(a, b, preferred_element_type=F32)


def _dot_nt(a, b):
    return lax.dot_general(a, b, (((1,), (1,)), ((), ())), preferred_element_type=F32)


def _dot_tn(a, b):
    return lax.dot_general(a, b, (((0,), (0,)), ((), ())), preferred_element_type=F32)


def _sigmoid(x):
    return 1.0 / (1.0 + jnp.exp(-x))


def _silu(x):
    return x * _sigmoid(x)


def _softplus(x):
    return jnp.maximum(x, 0.0) + jnp.log1p(jnp.exp(-jnp.abs(x)))


def _chunk_scan(x, chunk, reverse, op):
    n_rows = x.shape[0]
    row = lax.broadcasted_iota(jnp.int32, x.shape, 0) & (chunk - 1)
    sh = 1
    while sh < chunk:
        if reverse:
            shifted = pltpu.roll(x, n_rows - sh, axis=0)
            ok = row < chunk - sh
        else:
            shifted = pltpu.roll(x, sh, axis=0)
            ok = row >= sh
        if op == "add":
            x = x + jnp.where(ok, shifted, 0.0)
        else:
            x = jnp.maximum(x, jnp.where(ok, shifted, NEG))
        sh *= 2
    return x


def _lane_scan(x, seg, reverse, op, start=1):
    n_lanes = x.shape[1]
    pos = lax.broadcasted_iota(jnp.int32, x.shape, 1) & (seg - 1)
    sh = start
    while sh < seg:
        if reverse:
            shifted = pltpu.roll(x, n_lanes - sh, axis=1)
            ok = pos < seg - sh
        else:
            shifted = pltpu.roll(x, sh, axis=1)
            ok = pos >= sh
        if op == "add":
            x = x + jnp.where(ok, shifted, 0.0)
        else:
            x = jnp.maximum(x, jnp.where(ok, shifted, NEG))
        sh *= 2
    return x


def _tri_mask(n, reverse):
    t = lax.broadcasted_iota(jnp.int32, (n, n), 0)
    s = lax.broadcasted_iota(jnp.int32, (n, n), 1)
    return (s >= t) if reverse else (s <= t)


def _rmsnorm_kernel(h_ref, g_ref, o_ref):
    x = h_ref[...]
    ms = jnp.mean(x * x, axis=-1, keepdims=True)
    o_ref[...] = (x * lax.rsqrt(ms + EPS) * g_ref[...]).astype(o_ref.dtype)


def _rmsnorm_call(h2d, g, out_dtype, tm=1024):
    n_tok, d = h2d.shape
    tm = min(tm, n_tok)
    return pl.pallas_call(
        _rmsnorm_kernel,
        out_shape=jax.ShapeDtypeStruct((n_tok, d), out_dtype),
        grid=(n_tok // tm,),
        in_specs=[pl.BlockSpec((tm, d), lambda i: (i, 0)),
                  pl.BlockSpec((1, d), lambda i: (0, 0))],
        out_specs=pl.BlockSpec((tm, d), lambda i: (i, 0)),
        compiler_params=pltpu.CompilerParams(dimension_semantics=("parallel",),
                                             vmem_limit_bytes=VMEM_LIMIT),
        name="rmsnorm",
    )(h2d, g.reshape(1, d))


def _hg_att_safe(qc, kc, lfc, reverse):
    n = qc.shape[0]
    t = lax.broadcasted_iota(jnp.int32, (n, n), 0)
    s = lax.broadcasted_iota(jnp.int32, (n, n), 1)
    att = jnp.where(t == s, jnp.sum(qc * kc, axis=-1, keepdims=True), 0.0)
    half = 1
    while half < n:
        pre = _chunk_scan(lfc, half, False, "add")
        suf = _chunk_scan(lfc, half, True, "add")
        t_hi = (t & half) != 0
        s_hi = (s & half) != 0
        same_block = (t ^ s) < 2 * half
        if reverse:
            qe = qc * jnp.exp(suf)
            ke = kc * jnp.exp(pre - lfc)
            pair = jnp.logical_and(same_block, jnp.logical_and(jnp.logical_not(t_hi), s_hi))
        else:
            qe = qc * jnp.exp(pre)
            ke = kc * jnp.exp(suf - lfc)
            pair = jnp.logical_and(same_block, jnp.logical_and(t_hi, jnp.logical_not(s_hi)))
        att = att + jnp.where(pair, _dot_nt(qe.astype(BF16), ke.astype(BF16)), 0.0)
        half *= 2
    return att


def _hgrn2_kernel(u_ref, w_ref, lb_ref, ng_ref, o_ref,
                  q_s, v_s, k_s, lf_s, b_s, qe_s, ke_s, qi_s, kd_s, dec_s, upd_s, st_s, acc_s,
                  *, seq, chunk):
    n_chunks = seq // chunk
    proj = _dot(u_ref[...], w_ref[...])
    q_s[...] = _silu(proj[:, 0:HEAD])
    v_s[...] = proj[:, 3 * HEAD:4 * HEAD].astype(BF16)
    o_ref[...] = proj[:, 4 * HEAD:5 * HEAD]

    for d in range(2):
        f_raw = proj[:, (1 + d) * HEAD:(2 + d) * HEAD]
        lb = lb_ref[d:d + 1, :]
        sig_f = _sigmoid(f_raw)
        f = lb + (1.0 - lb) * sig_f
        k_s[d] = (1.0 - lb) * (1.0 - sig_f)
        log_f = jnp.log(jnp.maximum(f, TINY))
        lf_s[d] = log_f
        b = _chunk_scan(log_f, chunk, d == 1, "add")
        b_s[d] = b

    def ref_span(d):
        first, mid, last = (b_s[d, pl.ds(r, n_chunks, stride=chunk), :] for r in (0, chunk // 2, chunk - 1))
        return jnp.maximum(jnp.abs(first - mid), jnp.abs(mid - last))

    mild_decay = jnp.max(jnp.maximum(ref_span(0), ref_span(1))) <= HG_SAFE_DECAY

    for d in range(2):
        reverse = d == 1
        mask = _tri_mask(chunk, reverse)
        end_row = 0 if reverse else chunk - 1

        def chunk_rows(c):
            r0 = pl.multiple_of(c * chunk, chunk)
            return r0, pl.ds(r0, chunk)

        def scale_body(c, carry, d=d, end_row=end_row):
            r0, rows = chunk_rows(c)
            qc = q_s[rows, :]
            kc = k_s[d, rows, :]
            bc = b_s[d, rows, :]
            ref = b_s[d, pl.ds(r0 + chunk // 2, 1), :]
            bend = b_s[d, pl.ds(r0 + end_row, 1), :]
            qe_s[rows, :] = (qc * jnp.exp(bc - ref)).astype(BF16)
            ke_s[rows, :] = (kc * jnp.exp(ref - bc)).astype(BF16)
            qi_s[rows, :] = (qc * jnp.exp(bc)).astype(BF16)
            kd_s[rows, :] = (kc * jnp.exp(bend - bc)).astype(BF16)
            dec_s[c] = jnp.broadcast_to(jnp.exp(bend), (HEAD, HEAD)).T
            return carry

        lax.fori_loop(0, n_chunks, scale_body, 0, unroll=HG_UNROLL)

        for c in range(n_chunks):
            rows = slice(c * chunk, (c + 1) * chunk)
            upd_s[c] = _dot_tn(kd_s[rows, :], v_s[rows, :])

        st = jnp.zeros((HEAD, HEAD), F32)
        for c in (range(n_chunks - 1, -1, -1) if reverse else range(n_chunks)):
            st_s[c] = st.astype(BF16)
            st = st * dec_s[c] + upd_s[c]

        def emit(c, rows, att, d=d):
            o = _dot(jnp.concatenate([qi_s[rows, :], att.astype(BF16)], axis=1),
                     jnp.concatenate([st_s[c], v_s[rows, :]], axis=0))
            if d == 0:
                acc_s[rows, :] = o
            else:
                acc_s[rows, :] = acc_s[rows, :] + o

        @pl.when(mild_decay)
        def _(mask=mask):
            for g0 in range(0, n_chunks, HG_GROUP):
                group = [(c, slice(c * chunk, (c + 1) * chunk)) for c in range(g0, min(g0 + HG_GROUP, n_chunks))]
                atts = [jnp.where(mask, _dot_nt(qe_s[rows, :], ke_s[rows, :]), 0.0) for _, rows in group]
                for (c, rows), att in zip(group, atts):
                    emit(c, rows, att)

        @pl.when(jnp.logical_not(mild_decay))
        def _(d=d, reverse=reverse):
            def safe_body(c, carry):
                _, rows = chunk_rows(c)
                emit(c, rows, _hg_att_safe(q_s[rows, :], k_s[d, rows, :], lf_s[d, rows, :], reverse))
                return carry

            lax.fori_loop(0, n_chunks, safe_body, 0)

    o = acc_s[...]
    o = o * lax.rsqrt(jnp.mean(o * o, axis=-1, keepdims=True) + EPS) * ng_ref[...]
    o_ref[...] = o * _silu(o_ref[...])


def _hgrn2_call(u, w, lb, ng, *, chunk=HG_CHUNK):
    n_b, seq, d = u.shape
    kern = functools.partial(_hgrn2_kernel, seq=seq, chunk=chunk)
    return pl.pallas_call(
        kern,
        out_shape=jax.ShapeDtypeStruct((n_b, seq, HG_H * HEAD), F32),
        grid=(n_b, HG_H),
        in_specs=[pl.BlockSpec((None, seq, d), lambda b, h: (b, 0, 0)),
                  pl.BlockSpec((None, d, 5 * HEAD), lambda b, h: (h, 0, 0)),
                  pl.BlockSpec((None, 2, HEAD), lambda b, h: (h, 0, 0)),
                  pl.BlockSpec((None, 1, HEAD), lambda b, h: (h, 0, 0))],
        out_specs=pl.BlockSpec((None, seq, HEAD), lambda b, h: (b, 0, h)),
        scratch_shapes=[pltpu.VMEM((seq, HEAD), F32),
                        pltpu.VMEM((seq, HEAD), BF16),
                        pltpu.VMEM((2, seq, HEAD), F32),
                        pltpu.VMEM((2, seq, HEAD), F32),
                        pltpu.VMEM((2, seq, HEAD), F32),
                        pltpu.VMEM((seq, HEAD), BF16),
                        pltpu.VMEM((seq, HEAD), BF16),
                        pltpu.VMEM((seq, HEAD), BF16),
                        pltpu.VMEM((seq, HEAD), BF16),
                        pltpu.VMEM((seq // chunk, HEAD, HEAD), F32),
                        pltpu.VMEM((seq // chunk, HEAD, HEAD), F32),
                        pltpu.VMEM((seq // chunk, HEAD, HEAD), BF16),
                        pltpu.VMEM((seq, HEAD), F32)],
        compiler_params=pltpu.CompilerParams(dimension_semantics=("parallel", "arbitrary"),
                                             vmem_limit_bytes=VMEM_LIMIT),
        name="hgrn2_branch",
    )(u, w, lb, ng)


def _log_sigmoid(x):
    return jnp.minimum(x, 0.0) - jnp.log1p(jnp.exp(-jnp.abs(x)))


def _mlstm_kernel(u_ref, w_ref, bias_ref, ng_ref, o_ref,
                  qb_s, kb_s, va_s, tm_s, qw_s, kw_s, mx_s, en_s, gt_s, dec_s, upd_s, ct_s,
                  acc_s, *, seq, chunk):
    n_chunks = seq // chunk
    proj = _dot(u_ref[...], w_ref[...])
    q = proj[:, 0:HEAD]
    k = proj[:, HEAD:2 * HEAD] * (HEAD ** -0.5)
    qb_s[...] = q.astype(BF16)
    kb_s[...] = k.astype(BF16)
    va_s[:, 0:HEAD] = proj[:, 2 * HEAD:3 * HEAD].astype(BF16)
    va_s[:, HEAD:2 * HEAD] = jnp.ones((seq, HEAD), BF16)
    o_ref[...] = proj[:, 3 * HEAD:4 * HEAD]

    x = (proj[:, 4 * HEAD:5 * HEAD] + bias_ref[...]).T[0:SUBLANES, :]
    fwd_row = lax.broadcasted_iota(jnp.int32, (SUBLANES, seq), 0) == 0
    pos = lax.broadcasted_iota(jnp.int32, (SUBLANES, seq), 1)

    def scans(y, op, seg, start=1):
        return _lane_scan(y, seg, False, op, start), _lane_scan(y, seg, True, op, start)

    it = x
    log_f = pltpu.roll(_log_sigmoid(x), SUBLANES - 2, axis=0)
    pre, suf = scans(log_f, "add", chunk)
    bt = jnp.where(fwd_row, pre, suf)
    bt_end = pre + suf - log_f
    g = it - bt
    g_pre, g_suf = scans(g, "max", chunk)
    cm = jnp.where(fwd_row, g_pre, g_suf)
    cm_end = jnp.maximum(g_pre, g_suf)

    p_pre, p_suf = scans(bt_end, "add", seq, chunk)
    p_sum = jnp.where(fwd_row, p_pre, p_suf)
    x_pre, x_suf = scans(cm_end - (p_sum - bt_end), "max", seq, chunk)
    m_end = p_sum + jnp.where(fwd_row, x_pre, x_suf)
    m_prev = jnp.where(fwd_row,
                       jnp.where(pos < chunk, NEG, pltpu.roll(m_end, chunk, axis=1)),
                       jnp.where(pos >= seq - chunk, NEG, pltpu.roll(m_end, seq - chunk, axis=1)))
    mx = jnp.maximum(m_prev, cm)
    w_inter = jnp.exp(m_prev - mx)
    wk = jnp.exp(bt_end - bt + it - m_end)
    en = jnp.exp(-(bt + mx))
    decay = jnp.exp(bt_end + m_prev - m_end)

    for c in range(n_chunks):
        gt_s[c] = g[:, c * chunk:(c + 1) * chunk]
    row = lax.broadcasted_iota(jnp.int32, (SUBLANES, seq), 0)
    packed = jnp.where(row < 2, w_inter,
                       jnp.where(row < 4, pltpu.roll(wk, 2, axis=0),
                                 jnp.where(row < 6, pltpu.roll(mx, 4, axis=0), pltpu.roll(en, 6, axis=0))))
    tm_s[...] = jnp.concatenate(
        [packed, decay, jnp.zeros((LANES - 2 * SUBLANES, seq), F32)], axis=0).T
    for d in range(2):
        def lanes(col, ref_rows=slice(None)):
            block = tm_s[ref_rows, :]
            return jnp.broadcast_to(block[:, col:col + 1], (block.shape[0], LANES))
        qw_s[d] = (q * lanes(d)).astype(BF16)
        kw_s[d] = (k * lanes(2 + d)).astype(BF16)
        mx_s[d] = lanes(4 + d)
        en_s[d] = lanes(6 + d)
        dec_s[d] = lanes(8 + d, pl.ds(0, n_chunks, stride=chunk))

    for d in range(2):
        reverse = d == 1
        mask = _tri_mask(chunk, reverse)

        def chunk_rows(c):
            return pl.ds(pl.multiple_of(c * chunk, chunk), chunk)

        for c in range(n_chunks):
            rows = slice(c * chunk, (c + 1) * chunk)
            upd_s[c] = _dot_tn(kw_s[d, rows, :], va_s[rows, :])

        ct = jnp.zeros((HEAD, 2 * HEAD), F32)
        for c in (range(n_chunks - 1, -1, -1) if reverse else range(n_chunks)):
            ct_s[c] = ct.astype(BF16)
            dec = dec_s[d, c:c + 1, :]
            ct = ct * jnp.concatenate([dec, dec], axis=1) + upd_s[c]

        def scores(c, rows):
            p = jnp.exp(jnp.where(mask, gt_s[c][d:d + 1, :] - mx_s[d, rows, :], NEG))
            return (_dot_nt(qb_s[rows, :], kb_s[rows, :]) * p).astype(BF16)

        def emit(c, rows, sc):
            tot = _dot(jnp.concatenate([qw_s[d, rows, :], sc], axis=1),
                       jnp.concatenate([ct_s[c], va_s[rows, :]], axis=0))
            hval = tot[:, 0:HEAD] / jnp.maximum(jnp.abs(tot[:, HEAD:2 * HEAD]), en_s[d, rows, :])
            if d == 0:
                acc_s[rows, :] = hval
            else:
                acc_s[rows, :] = acc_s[rows, :] + hval

        for g0 in range(0, n_chunks, ML_GROUP):
            group = [(c, slice(c * chunk, (c + 1) * chunk)) for c in range(g0, min(g0 + ML_GROUP, n_chunks))]
            staged = [scores(c, rows) for c, rows in group]
            for (c, rows), sc in zip(group, staged):
                emit(c, rows, sc)

    hsum = acc_s[...]
    hc = hsum - jnp.mean(hsum, axis=-1, keepdims=True)
    hn = hc * lax.rsqrt(jnp.mean(hc * hc, axis=-1, keepdims=True) + EPS) * ng_ref[...]
    o_ref[...] = hn * _sigmoid(o_ref[...])


def _mlstm_call(u, w, bias, ng, *, chunk=ML_CHUNK):
    n_b, seq, d = u.shape
    n_chunks = seq // chunk
    kern = functools.partial(_mlstm_kernel, seq=seq, chunk=chunk)
    return pl.pallas_call(
        kern,
        out_shape=jax.ShapeDtypeStruct((n_b, seq, ML_H * HEAD), F32),
        grid=(n_b, ML_H),
        in_specs=[pl.BlockSpec((None, seq, d), lambda b, h: (b, 0, 0)),
                  pl.BlockSpec((None, d, 5 * HEAD), lambda b, h: (h, 0, 0)),
                  pl.BlockSpec((None, 1, HEAD), lambda b, h: (h, 0, 0)),
                  pl.BlockSpec((None, 1, HEAD), lambda b, h: (h, 0, 0))],
        out_specs=pl.BlockSpec((None, seq, HEAD), lambda b, h: (b, 0, h)),
        scratch_shapes=[pltpu.VMEM((seq, HEAD), BF16),
                        pltpu.VMEM((seq, HEAD), BF16),
                        pltpu.VMEM((seq, 2 * HEAD), BF16),
                        pltpu.VMEM((seq, LANES), F32),
                        pltpu.VMEM((2, seq, HEAD), BF16),
                        pltpu.VMEM((2, seq, HEAD), BF16),
                        pltpu.VMEM((2, seq, LANES), F32),
                        pltpu.VMEM((2, seq, LANES), F32),
                        pltpu.VMEM((n_chunks, SUBLANES, chunk), F32),
                        pltpu.VMEM((2, n_chunks, LANES), F32),
                        pltpu.VMEM((n_chunks, HEAD, 2 * HEAD), F32),
                        pltpu.VMEM((n_chunks, HEAD, 2 * HEAD), BF16),
                        pltpu.VMEM((seq, HEAD), F32)],
        compiler_params=pltpu.CompilerParams(dimension_semantics=("parallel", "arbitrary"),
                                             vmem_limit_bytes=VMEM_LIMIT),
        name="mlstm_branch",
    )(u, w, bias, ng)


_SSD_W_Z = 0
_SSD_W_XBC = SSD_GW
_SSD_W_DT = SSD_GW + SSD_GW + 2 * SSD_N
_SSD_W_END = _SSD_W_DT + LANES
_SSD_XBC_G = SSD_GW + 2 * SSD_N
_SSD_PAD = SUBLANES
_SSD_PAIRS = SSD_HG // 2


def _ssd_kernel(u_ref, w_ref, cw_ref, cb_ref, dtb_ref, alog_ref, dsk_ref, ng_ref, o_ref,
                pad_s, xs_s, b_s, c_s, a_s, at_s, dtt_s, cbm_s, ea_s, xw_s, dec_s, cur_s, st_s,
                *, seq, chunk, rblk):
    n_chunks = seq // chunk
    n_rblk = seq // rblk

    zero_rows = jnp.zeros((_SSD_PAD, _SSD_XBC_G), F32)
    pad_s[0:_SSD_PAD, :] = zero_rows
    pad_s[_SSD_PAD + seq:2 * _SSD_PAD + seq, :] = zero_rows
    for i in range(n_rblk):
        r0 = i * rblk
        pad_s[_SSD_PAD + r0:_SSD_PAD + r0 + rblk, :] = _dot(
            u_ref[r0:r0 + rblk, :], w_ref[:, _SSD_W_XBC:_SSD_W_DT])
    for i in range(n_rblk):
        r0 = i * rblk
        acc = cb_ref[...]
        slab = pad_s[r0:r0 + rblk + 2 * _SSD_PAD, :]
        for j in range(SSD_CONV):
            tap = slab if j == SSD_CONV // 2 else pltpu.roll(
                slab, (SSD_CONV // 2 - j) % (rblk + 2 * _SSD_PAD), axis=0)
            acc = acc + tap[_SSD_PAD:_SSD_PAD + rblk, :] * cw_ref[j:j + 1, :]
        xbc = _silu(acc)
        xs_s[r0:r0 + rblk, :] = xbc[:, 0:SSD_GW]
        b_s[r0:r0 + rblk, :] = xbc[:, SSD_GW:SSD_GW + SSD_N].astype(BF16)
        c_s[r0:r0 + rblk, :] = xbc[:, SSD_GW + SSD_N:SSD_GW + 2 * SSD_N].astype(BF16)

    n_hd = 2 * SSD_HG
    dt_tok = _softplus(_dot(u_ref[...], w_ref[:, _SSD_W_DT:_SSD_W_END]) + dtb_ref[...])
    dt = dt_tok.T[0:n_hd, :]
    log_a = (dt_tok * (-jnp.exp(alog_ref[...]))).T[0:n_hd, :]
    fwd_row = lax.broadcasted_iota(jnp.int32, (n_hd, seq), 0) < SSD_HG
    pre = _lane_scan(log_a, chunk, False, "add")
    suf = _lane_scan(log_a, chunk, True, "add")
    acum = jnp.where(fwd_row, pre, suf)
    a_end = pre + suf - log_a
    for c in range(n_chunks):
        lanes_c = slice(c * chunk, (c + 1) * chunk)
        at_s[c] = acum[:, lanes_c]
        dtt_s[c] = dt[:, lanes_c]
        r0 = c * chunk
        cbm_s[r0:r0 + chunk, :] = _dot_nt(c_s[r0:r0 + chunk, :], b_s[r0:r0 + chunk, :])
    a_s[...] = jnp.concatenate(
        [acum, jnp.exp(acum), dt * jnp.exp(a_end - acum), jnp.exp(a_end),
         jnp.zeros((LANES - 4 * n_hd, seq), F32)], axis=0).T
    lane_lo = lax.broadcasted_iota(jnp.int32, (chunk, LANES), 1) < SSD_HEAD_DIM

    def split3(x):
        hi = x.astype(BF16)
        r1 = x - hi.astype(F32)
        mid = r1.astype(BF16)
        return [hi, mid, (r1 - mid.astype(F32)).astype(BF16)]

    def spread(first_lane, pieces):
        one = (lax.broadcasted_iota(jnp.int32, (LANES, SSD_GW), 1) // SSD_HEAD_DIM
               == lax.broadcasted_iota(jnp.int32, (LANES, SSD_GW), 0) - first_lane).astype(BF16)
        return jnp.concatenate([one] * pieces, axis=0)

    for d in range(2):
        reverse = d == 1
        mask = _tri_mask(chunk, reverse)
        spread_ea = spread(n_hd + d * SSD_HG, 1)
        spread_wt = spread(2 * n_hd + d * SSD_HG, 2)
        dec_s[...] = _dot(jnp.concatenate(split3(a_s[pl.ds(0, n_chunks, stride=chunk), :]), axis=1),
                          spread(3 * n_hd + d * SSD_HG, 3))
        for i in range(n_rblk):
            rows = slice(i * rblk, (i + 1) * rblk)
            hi, mid, _ = split3(a_s[rows, :])
            ea_s[rows, :] = _dot(hi, spread_ea).astype(BF16)
            xw_s[rows, :] = (xs_s[rows, :] * _dot(jnp.concatenate([hi, mid], axis=1), spread_wt)).astype(BF16)

        def chunk_rows(c):
            return pl.ds(pl.multiple_of(c * chunk, chunk), chunk)

        for c in range(n_chunks):
            rows = slice(c * chunk, (c + 1) * chunk)
            pad_s[rows, 0:SSD_GW] = _dot_tn(b_s[rows, :], xw_s[rows, :])

        cur_s[...] = jnp.zeros_like(cur_s)

        for c in (range(n_chunks - 1, -1, -1) if reverse else range(n_chunks)):
            cur = cur_s[...]
            st_s[c] = cur.astype(BF16)
            cur_s[...] = cur * dec_s[c:c + 1, :] + pad_s[c * chunk:(c + 1) * chunk, 0:SSD_GW]

        def out_body(c, carry, mask=mask, d=d):
            rows = chunk_rows(c)
            blk = a_s[rows, :]
            blk_t = at_s[c]
            dtt = dtt_s[c]
            cbm = cbm_s[rows, :]
            y_inter = _dot(c_s[rows, :], st_s[c])
            for p in range(_SSD_PAIRS):
                cols = slice(p * LANES, (p + 1) * LANES)
                xs_pair = xs_s[rows, cols]
                m_tiles, x_tiles = [], []
                for hh in range(2):
                    col = d * SSD_HG + 2 * p + hh
                    seg = jnp.exp(jnp.where(mask, blk[:, col:col + 1] - blk_t[col:col + 1, :], NEG))
                    m_tiles.append((cbm * seg * dtt[col:col + 1, :]).astype(BF16))
                    keep = lane_lo if hh == 0 else jnp.logical_not(lane_lo)
                    x_tiles.append(jnp.where(keep, xs_pair, 0.0).astype(BF16))
                y = (y_inter[:, cols] * ea_s[rows, cols].astype(F32)
                     + _dot(jnp.concatenate(m_tiles, axis=1), jnp.concatenate(x_tiles, axis=0)))
                if d == 0:
                    o_ref[rows, cols] = y
                else:
                    o_ref[rows, cols] = o_ref[rows, cols] + y
            return carry

        lax.fori_loop(0, n_chunks, out_body, 0, unroll=SSD_UNROLL)

    for i in range(n_rblk):
        r0 = i * rblk
        rows = slice(r0, r0 + rblk)
        z = _dot(u_ref[rows, :], w_ref[:, _SSD_W_Z:_SSD_W_XBC])
        y = (o_ref[rows, :] + xs_s[rows, :] * dsk_ref[...]) * _silu(z)
        o_ref[rows, :] = y * lax.rsqrt(jnp.mean(y * y, axis=-1, keepdims=True) + EPS) * ng_ref[...]


def _ssd_call(u, w, cw, cb, dtb, alog, dsk, ng, *, chunk=SSD_CHUNK):
    n_b, seq, d = u.shape
    rblk = min(seq, 512)
    kern = functools.partial(_ssd_kernel, seq=seq, chunk=chunk, rblk=rblk)
    vec = lambda n: pl.BlockSpec((None, 1, n), lambda b, g: (g, 0, 0))
    return pl.pallas_call(
        kern,
        out_shape=jax.ShapeDtypeStruct((n_b, seq, SSD_G * SSD_GW), F32),
        grid=(n_b, SSD_G),
        in_specs=[pl.BlockSpec((None, seq, d), lambda b, g: (b, 0, 0)),
                  pl.BlockSpec((None, d, _SSD_W_END), lambda b, g: (g, 0, 0)),
                  pl.BlockSpec((None, SSD_CONV, _SSD_XBC_G), lambda b, g: (g, 0, 0)),
                  vec(_SSD_XBC_G), vec(LANES), vec(LANES), vec(SSD_GW), vec(SSD_GW)],
        out_specs=pl.BlockSpec((None, seq, SSD_GW), lambda b, g: (b, 0, g)),
        scratch_shapes=[pltpu.VMEM((seq + 2 * _SSD_PAD, _SSD_XBC_G), F32),
                        pltpu.VMEM((seq, SSD_GW), F32),
                        pltpu.VMEM((seq, SSD_N), BF16),
                        pltpu.VMEM((seq, SSD_N), BF16),
                        pltpu.VMEM((seq, LANES), F32),
                        pltpu.VMEM((seq // chunk, 2 * SSD_HG, chunk), F32),
                        pltpu.VMEM((seq // chunk, 2 * SSD_HG, chunk), F32),
                        pltpu.VMEM((seq, chunk), F32),
                        pltpu.VMEM((seq, SSD_GW), BF16),
                        pltpu.VMEM((seq, SSD_GW), BF16),
                        pltpu.VMEM((seq // chunk, SSD_GW), F32),
                        pltpu.VMEM((SSD_N, SSD_GW), F32),
                        pltpu.VMEM((seq // chunk, SSD_N, SSD_GW), BF16)],
        compiler_params=pltpu.CompilerParams(dimension_semantics=("parallel", "arbitrary"),
                                             vmem_limit_bytes=VMEM_LIMIT),
        name="ssd_branch",
    )(u, w, cw, cb, dtb, alog, dsk, ng)


def _merge_kernel(u_ref, h_ref, y0_ref, y1_ref, y2_ref, wg_ref, wb_ref, wo_ref, o_ref):
    u = u_ref[...]
    merged = None
    for r, y_ref in enumerate((y0_ref, y1_ref, y2_ref)):
        term = _sigmoid(_dot(u, wg_ref[r])) * _dot(y_ref[...].astype(BF16), wb_ref[r])
        merged = term if merged is None else merged + term
    o_ref[...] = h_ref[...] + _dot(merged.astype(BF16), wo_ref[...])


def _merge_call(u2d, h2d, y0, y1, y2, wg, wb, wo, tm=512):
    n_tok, d = h2d.shape
    tm = min(tm, n_tok)
    tok = lambda: pl.BlockSpec((tm, d), lambda i: (i, 0))
    held = lambda shape: pl.BlockSpec(shape, lambda i: tuple(0 for _ in shape), pipeline_mode=pl.Buffered(1))
    return pl.pallas_call(
        _merge_kernel,
        out_shape=jax.ShapeDtypeStruct((n_tok, d), F32),
        grid=(n_tok // tm,),
        in_specs=[tok(), tok(), tok(), tok(), tok(),
                  held((3, d, d)), held((3, d, d)), held((d, d))],
        out_specs=tok(),
        compiler_params=pltpu.CompilerParams(dimension_semantics=("parallel",),
                                             vmem_limit_bytes=VMEM_LIMIT),
        name="merge",
    )(u2d, h2d, y0, y1, y2, wg, wb, wo)


_FFN_CW = 256
_FFN_NJ = D_FF // _FFN_CW
_FFN_HALO = SUBLANES

def _ffn_kernel(h_ref, hp_ref, hn_ref, p_ref, g_ref, wa_ref, wv_ref, cwa_ref, cwv_ref,
                cba_ref, cbv_ref, wd_ref, wple_ref, wpg_ref, fg_ref, o_ref, u_s,
                *, tm, n_tiles, final_norm):
    i = pl.program_id(1)
    j = pl.program_id(2)

    def norm(x):
        return (x * lax.rsqrt(jnp.mean(x * x, axis=-1, keepdims=True) + EPS) * g_ref[...]).astype(BF16)

    @pl.when(j == 0)
    def _():
        up = jnp.where(i > 0, 1.0, 0.0)
        dn = jnp.where(i < n_tiles - 1, 1.0, 0.0)
        u_s[0:_FFN_HALO, :] = norm(hp_ref[...] * up)
        u_s[_FFN_HALO:_FFN_HALO + tm, :] = norm(h_ref[...])
        u_s[_FFN_HALO + tm:2 * _FFN_HALO + tm, :] = norm(hn_ref[...] * dn)
        o_ref[...] = h_ref[...]

    def conv(w_ref, cw_ref, cb_ref):
        up = _dot(u_s[...], w_ref[...])
        n_rows = tm + 2 * _FFN_HALO
        acc = cb_ref[...]
        for t in range(FFN_CONV):
            tap = up if t == FFN_CONV // 2 else pltpu.roll(up, (FFN_CONV // 2 - t) % n_rows, axis=0)
            acc = acc + tap[_FFN_HALO:_FFN_HALO + tm, :] * cw_ref[t:t + 1, :]
        return acc

    act = (_silu(conv(wa_ref, cwa_ref, cba_ref)) * conv(wv_ref, cwv_ref, cbv_ref)).astype(BF16)
    o_ref[...] += _dot(act, wd_ref[...])

    @pl.when(j == _FFN_NJ - 1)
    def _():
        h2 = o_ref[...]
        ple = _dot(p_ref[...].astype(BF16), wple_ref[...])
        h3 = h2 + ple * _sigmoid(_dot(h2.astype(BF16), wpg_ref[...]))
        if final_norm:
            h3 = h3 * lax.rsqrt(jnp.mean(h3 * h3, axis=-1, keepdims=True) + EPS) * fg_ref[...]
        o_ref[...] = h3


def _ffn_call(h, p, g, w_up, cw, cb, w_down, w_ple, w_pg, fg, *, final_norm, tm=1024):
    n_b, seq, d = h.shape
    tm = min(tm, seq)
    n_tiles = seq // tm
    hb = tm // _FFN_HALO
    n_hb = seq // _FFN_HALO
    kern = functools.partial(_ffn_kernel, tm=tm, n_tiles=n_tiles, final_norm=final_norm)
    cst = lambda shape: pl.BlockSpec(shape, lambda b, i, j: tuple(0 for _ in shape))
    return pl.pallas_call(
        kern,
        out_shape=jax.ShapeDtypeStruct((n_b, seq, d), F32),
        grid=(n_b, n_tiles, _FFN_NJ),
        in_specs=[pl.BlockSpec((None, tm, d), lambda b, i, j: (b, i, 0)),
                  pl.BlockSpec((None, _FFN_HALO, d), lambda b, i, j: (b, jnp.maximum(i * hb - 1, 0), 0)),
                  pl.BlockSpec((None, _FFN_HALO, d), lambda b, i, j: (b, jnp.minimum((i + 1) * hb, n_hb - 1), 0)),
                  pl.BlockSpec((None, tm, PLE_DIM), lambda b, i, j: (b, i, 0)),
                  cst((1, d)),
                  pl.BlockSpec((d, _FFN_CW), lambda b, i, j: (0, j)),
                  pl.BlockSpec((d, _FFN_CW), lambda b, i, j: (0, _FFN_NJ + j)),
                  pl.BlockSpec((FFN_CONV, _FFN_CW), lambda b, i, j: (0, j)),
                  pl.BlockSpec((FFN_CONV, _FFN_CW), lambda b, i, j: (0, _FFN_NJ + j)),
                  pl.BlockSpec((1, _FFN_CW), lambda b, i, j: (0, j)),
                  pl.BlockSpec((1, _FFN_CW), lambda b, i, j: (0, _FFN_NJ + j)),
                  pl.BlockSpec((_FFN_CW, d), lambda b, i, j: (j, 0)),
                  cst((PLE_DIM, d)), cst((d, d)), cst((1, d))],
        out_specs=pl.BlockSpec((None, tm, d), lambda b, i, j: (b, i, 0)),
        scratch_shapes=[pltpu.VMEM((tm + 2 * _FFN_HALO, d), BF16)],
        compiler_params=pltpu.CompilerParams(
            dimension_semantics=("parallel", "parallel", "arbitrary"),
            vmem_limit_bytes=VMEM_LIMIT),
        name="convffn_ple",
    )(h, h, h, p, g.reshape(1, d), w_up, w_up, cw, cw, cb.reshape(1, -1), cb.reshape(1, -1),
      w_down, w_ple, w_pg, fg.reshape(1, d))


def _cols(w, start, width):
    return lax.slice_in_dim(w, start, start + width, axis=1)


def _pad_cols(w, width):
    return jnp.pad(w, ((0, 0), (0, width - w.shape[1])))


def _ssd_params(w_in, conv_w, conv_b, dt_bias, a_log, d_skip, norm_g):
    ws, cws, cbs, dtbs, alogs = [], [], [], [], []
    for g in range(SSD_G):
        dt_cols = jnp.concatenate(
            [_cols(w_in, _OFF_DT + dd * SSD_H + g * SSD_HG, SSD_HG) for dd in range(2)], axis=1)
        ws.append(jnp.concatenate([
            _cols(w_in, _OFF_Z + g * SSD_GW, SSD_GW),
            _cols(w_in, _OFF_XBC + g * SSD_GW, SSD_GW),
            _cols(w_in, _OFF_XBC + D_MODEL + g * SSD_N, SSD_N),
            _cols(w_in, _OFF_XBC + D_MODEL + SSD_G * SSD_N + g * SSD_N, SSD_N),
            _pad_cols(dt_cols, LANES)], axis=1))
        pick = lambda a: jnp.concatenate([
            _cols(a, g * SSD_GW, SSD_GW),
            _cols(a, D_MODEL + g * SSD_N, SSD_N),
            _cols(a, D_MODEL + SSD_G * SSD_N + g * SSD_N, SSD_N)], axis=1)
        cws.append(pick(conv_w))
        cbs.append(pick(conv_b.reshape(1, -1)))
        head_row = lambda a: _pad_cols(
            jnp.concatenate([a[dd, g * SSD_HG:(g + 1) * SSD_HG] for dd in range(2)]).reshape(1, -1), LANES)
        dtbs.append(head_row(dt_bias))
        alogs.append(head_row(a_log))
    dsk = jnp.repeat(d_skip, SSD_HEAD_DIM).reshape(SSD_G, 1, SSD_GW)
    return (jnp.stack(ws).astype(BF16), jnp.stack(cws), jnp.stack(cbs), jnp.stack(dtbs),
            jnp.stack(alogs), dsk, norm_g.reshape(SSD_G, 1, SSD_GW))


def _hgrn2_params(w_in, lb, norm_g):
    ws = [jnp.concatenate([
        _cols(w_in, _OFF_GQ + h * HEAD, HEAD),
        _cols(w_in, _OFF_GF + h * HEAD, HEAD),
        _cols(w_in, _OFF_GF + D_MODEL + h * HEAD, HEAD),
        _cols(w_in, _OFF_GI + h * HEAD, HEAD),
        _cols(w_in, _OFF_GG + h * HEAD, HEAD)], axis=1) for h in range(HG_H)]
    lbs = lb.reshape(2, HG_H, HEAD).swapaxes(0, 1)
    return jnp.stack(ws).astype(BF16), lbs, norm_g.reshape(HG_H, 1, HEAD)


def _mlstm_params(w_in, i_bias, f_bias, norm_g):
    ws, biases = [], []
    for h in range(ML_H):
        gate_cols = jnp.concatenate([
            _cols(w_in, _OFF_MI + h, 1), _cols(w_in, _OFF_MI + ML_H + h, 1),
            _cols(w_in, _OFF_MF + h, 1), _cols(w_in, _OFF_MF + ML_H + h, 1)], axis=1)
        ws.append(jnp.concatenate([
            _cols(w_in, _OFF_MQ + h * HEAD, HEAD),
            _cols(w_in, _OFF_MK + h * HEAD, HEAD),
            _cols(w_in, _OFF_MV + h * HEAD, HEAD),
            _cols(w_in, _OFF_MO + h * HEAD, HEAD),
            _pad_cols(gate_cols, LANES)], axis=1))
        biases.append(_pad_cols(
            jnp.stack([i_bias[0, h], i_bias[1, h], f_bias[0, h], f_bias[1, h]]).reshape(1, 4), LANES))
    return jnp.stack(ws).astype(BF16), jnp.stack(biases), norm_g.reshape(ML_H, 1, HEAD)


def kernel(x, p, norm_mix_g, w_in, ssd_conv_w, ssd_conv_b, ssd_dt_bias, ssd_a_log, ssd_d, ssd_norm_g, hg_lb_raw, hg_norm_g, ml_i_bias, ml_f_bias, ml_norm_g, w_br_ssd, w_br_hg, w_br_ml, w_out, norm_ffn_g, w_up, ffn_conv_w, ffn_conv_b, w_down, w_ple, w_ple_gate, final_norm_g):
    n_b, seq, d = x.shape
    depth = w_in.shape[0]
    lb_soft = jax.nn.softmax(hg_lb_raw.astype(F32), axis=0)
    hg_lb = jnp.cumsum(lb_soft, axis=0) - lb_soft[0:1]
    h = x
    for l in range(depth):
        u2d = _rmsnorm_call(h.reshape(n_b * seq, d), norm_mix_g[l], BF16)
        u = u2d.reshape(n_b, seq, d)
        y_ssd = _ssd_call(u, *_ssd_params(w_in[l], ssd_conv_w[l], ssd_conv_b[l], ssd_dt_bias[l],
                                          ssd_a_log[l], ssd_d[l], ssd_norm_g[l]))
        y_hg = _hgrn2_call(u, *_hgrn2_params(w_in[l], hg_lb[l], hg_norm_g[l]))
        y_ml = _mlstm_call(u, *_mlstm_params(w_in[l], ml_i_bias[l], ml_f_bias[l], ml_norm_g[l]))
        wg = jnp.stack([_cols(w_in[l], _OFF_GATES + r * d, d) for r in range(3)]).astype(BF16)
        wb = jnp.stack([w_br_ssd[l], w_br_hg[l], w_br_ml[l]]).astype(BF16)
        tok = lambda a: a.reshape(n_b * seq, d)
        h = _merge_call(u2d, tok(h), tok(y_ssd), tok(y_hg), tok(y_ml), wg, wb,
                        w_out[l].astype(BF16)).reshape(n_b, seq, d)
        h = _ffn_call(h, p[l], norm_ffn_g[l], w_up[l].astype(BF16), ffn_conv_w[l], ffn_conv_b[l],
                      w_down[l].astype(BF16), w_ple[l].astype(BF16), w_ple_gate[l].astype(BF16),
                      final_norm_g, final_norm=(l == depth - 1))
    return h
```

```python
import functools

import jax
import jax.numpy as jnp
from jax import lax
from jax.experimental import pallas as pl
from jax.experimental.pallas import tpu as pltpu

F32 = jnp.float32
BF16 = jnp.bfloat16

D_MODEL = 1024
PLE_DIM = 256
EPS = 1e-6
NEG = -1e30
TINY = 1e-30

SSD_HEAD_DIM = 64
SSD_H = 16
SSD_G = 2
SSD_HG = 8
SSD_N = 128
SSD_CONV = 5
SSD_GW = SSD_HG * SSD_HEAD_DIM
SSD_XBC = D_MODEL + 2 * SSD_G * SSD_N
HG_H = 8
ML_H = 8
HEAD = 128
D_FF = 2816
FFN_CONV = 3

LANES = 128
SUBLANES = 8
VMEM_LIMIT = 56 * 1024 * 1024

SSD_CHUNK = 128
HG_CHUNK = 64
ML_CHUNK = 128
SSD_UNROLL = 8
HG_UNROLL = 32
HG_GROUP = 16
ML_GROUP = 8
HG_SAFE_DECAY = 60.0

_OFF_Z = 0
_OFF_XBC = 1024
_OFF_DT = _OFF_XBC + SSD_XBC
_OFF_GQ = _OFF_DT + 2 * SSD_H
_OFF_GF = _OFF_GQ + 1024
_OFF_GI = _OFF_GF + 2048
_OFF_GG = _OFF_GI + 1024
_OFF_MQ = _OFF_GG + 1024
_OFF_MK = _OFF_MQ + 1024
_OFF_MV = _OFF_MK + 1024
_OFF_MI = _OFF_MV + 1024
_OFF_MF = _OFF_MI + 2 * ML_H
_OFF_MO = _OFF_MF + 2 * ML_H
_OFF_GATES = _OFF_MO + 1024


def _dot(a, b):
    return jnp.dot(a, b, preferred_element_type=F32)


def _dot_nt(a, b):
    return lax.dot_general(a, b, (((1,), (1,)), ((), ())), preferred_element_type=F32)


def _dot_tn(a, b):
    return lax.dot_general(a, b, (((0,), (0,)), ((), ())), preferred_element_type=F32)


def _sigmoid(x):
    return 1.0 / (1.0 + jnp.exp(-x))


def _silu(x):
    return x * _sigmoid(x)


def _softplus(x):
    return jnp.maximum(x, 0.0) + jnp.log1p(jnp.exp(-jnp.abs(x)))


def _chunk_scan(x, chunk, reverse, op):
    n_rows = x.shape[0]
    row = lax.broadcasted_iota(jnp.int32, x.shape, 0) & (chunk - 1)
    sh = 1
    while sh < chunk:
        if reverse:
            shifted = pltpu.roll(x, n_rows - sh, axis=0)
            ok = row < chunk - sh
        else:
            shifted = pltpu.roll(x, sh, axis=0)
            ok = row >= sh
        if op == "add":
            x = x + jnp.where(ok, shifted, 0.0)
        else:
            x = jnp.maximum(x, jnp.where(ok, shifted, NEG))
        sh *= 2
    return x


def _lane_scan(x, seg, reverse, op, start=1):
    n_lanes = x.shape[1]
    pos = lax.broadcasted_iota(jnp.int32, x.shape, 1) & (seg - 1)
    sh = start
    while sh < seg:
        if reverse:
            shifted = pltpu.roll(x, n_lanes - sh, axis=1)
            ok = pos < seg - sh
        else:
            shifted = pltpu.roll(x, sh, axis=1)
            ok = pos >= sh
        if op == "add":
            x = x + jnp.where(ok, shifted, 0.0)
        else:
            x = jnp.maximum(x, jnp.where(ok, shifted, NEG))
        sh *= 2
    return x


def _tri_mask(n, reverse):
    t = lax.broadcasted_iota(jnp.int32, (n, n), 0)
    s = lax.broadcasted_iota(jnp.int32, (n, n), 1)
    return (s >= t) if reverse else (s <= t)


def _rmsnorm_kernel(h_ref, g_ref, o_ref):
    x = h_ref[...]
    ms = jnp.mean(x * x, axis=-1, keepdims=True)
    o_ref[...] = (x * lax.rsqrt(ms + EPS) * g_ref[...]).astype(o_ref.dtype)


def _rmsnorm_call(h2d, g, out_dtype, tm=1024):
    n_tok, d = h2d.shape
    tm = min(tm, n_tok)
    return pl.pallas_call(
        _rmsnorm_kernel,
        out_shape=jax.ShapeDtypeStruct((n_tok, d), out_dtype),
        grid=(n_tok // tm,),
        in_specs=[pl.BlockSpec((tm, d), lambda i: (i, 0)),
                  pl.BlockSpec((1, d), lambda i: (0, 0))],
        out_specs=pl.BlockSpec((tm, d), lambda i: (i, 0)),
        compiler_params=pltpu.CompilerParams(dimension_semantics=("parallel",),
                                             vmem_limit_bytes=VMEM_LIMIT),
        name="rmsnorm",
    )(h2d, g.reshape(1, d))


def _hg_att_safe(qc, kc, lfc, reverse):
    n = qc.shape[0]
    t = lax.broadcasted_iota(jnp.int32, (n, n), 0)
    s = lax.broadcasted_iota(jnp.int32, (n, n), 1)
    att = jnp.where(t == s, jnp.sum(qc * kc, axis=-1, keepdims=True), 0.0)
    half = 1
    while half < n:
        pre = _chunk_scan(lfc, half, False, "add")
        suf = _chunk_scan(lfc, half, True, "add")
        t_hi = (t & half) != 0
        s_hi = (s & half) != 0
        same_block = (t ^ s) < 2 * half
        if reverse:
            qe = qc * jnp.exp(suf)
            ke = kc * jnp.exp(pre - lfc)
            pair = jnp.logical_and(same_block, jnp.logical_and(jnp.logical_not(t_hi), s_hi))
        else:
            qe = qc * jnp.exp(pre)
            ke = kc * jnp.exp(suf - lfc)
            pair = jnp.logical_and(same_block, jnp.logical_and(t_hi, jnp.logical_not(s_hi)))
        att = att + jnp.where(pair, _dot_nt(qe.astype(BF16), ke.astype(BF16)), 0.0)
        half *= 2
    return att


def _hgrn2_kernel(u_ref, w_ref, lb_ref, ng_ref, o_ref,
                  q_s, v_s, k_s, lf_s, b_s, qe_s, ke_s, qi_s, kd_s, dec_s, upd_s, st_s, acc_s,
                  *, seq, chunk):
    n_chunks = seq // chunk
    proj = _dot(u_ref[...], w_ref[...])
    q_s[...] = _silu(proj[:, 0:HEAD])
    v_s[...] = proj[:, 3 * HEAD:4 * HEAD].astype(BF16)
    o_ref[...] = proj[:, 4 * HEAD:5 * HEAD]

    for d in range(2):
        f_raw = proj[:, (1 + d) * HEAD:(2 + d) * HEAD]
        lb = lb_ref[d:d + 1, :]
        sig_f = _sigmoid(f_raw)
        f = lb + (1.0 - lb) * sig_f
        k_s[d] = (1.0 - lb) * (1.0 - sig_f)
        log_f = jnp.log(jnp.maximum(f, TINY))
        lf_s[d] = log_f
        b = _chunk_scan(log_f, chunk, d == 1, "add")
        b_s[d] = b

    def ref_span(d):
        first, mid, last = (b_s[d, pl.ds(r, n_chunks, stride=chunk), :] for r in (0, chunk // 2, chunk - 1))
        return jnp.maximum(jnp.abs(first - mid), jnp.abs(mid - last))

    mild_decay = jnp.max(jnp.maximum(ref_span(0), ref_span(1))) <= HG_SAFE_DECAY

    def chunk_rows(c):
        r0 = pl.multiple_of(c * chunk, chunk)
        return r0, pl.ds(r0, chunk)

    for d in range(2):
        reverse = d == 1
        end_row = 0 if reverse else chunk - 1

        def scale_body(c, carry, d=d, end_row=end_row):
            r0, rows = chunk_rows(c)
            qc = q_s[rows, :]
            kc = k_s[d, rows, :]
            bc = b_s[d, rows, :]
            ref = b_s[d, pl.ds(r0 + chunk // 2, 1), :]
            bend = b_s[d, pl.ds(r0 + end_row, 1), :]
            qe_s[d, rows, :] = (qc * jnp.exp(bc - ref)).astype(BF16)
            ke_s[d, rows, :] = (kc * jnp.exp(ref - bc)).astype(BF16)
            qi_s[d, rows, :] = (qc * jnp.exp(bc)).astype(BF16)
            kd_s[rows, :] = (kc * jnp.exp(bend - bc)).astype(BF16)
            dec_s[c] = jnp.broadcast_to(jnp.exp(bend), (HEAD, HEAD)).T
            return carry

        lax.fori_loop(0, n_chunks, scale_body, 0, unroll=HG_UNROLL)

        for c in range(n_chunks):
            rows = slice(c * chunk, (c + 1) * chunk)
            upd_s[c] = _dot_tn(kd_s[rows, :], v_s[rows, :])

        st = jnp.zeros((HEAD, HEAD), F32)
        for c in (range(n_chunks - 1, -1, -1) if reverse else range(n_chunks)):
            st_s[d, c] = st.astype(BF16)
            st = st * dec_s[c] + upd_s[c]

    def emit(d, c, rows, att):
        o = _dot(jnp.concatenate([qi_s[d, rows, :], att.astype(BF16)], axis=1),
                 jnp.concatenate([st_s[d, c], v_s[rows, :]], axis=0))
        if d == 0:
            acc_s[rows, :] = o
        else:
            acc_s[rows, :] = acc_s[rows, :] + o

    @pl.when(mild_decay)
    def _():
        for d in range(2):
            mask = _tri_mask(chunk, d == 1)
            for g0 in range(0, n_chunks, HG_GROUP):
                group = [(c, slice(c * chunk, (c + 1) * chunk)) for c in range(g0, min(g0 + HG_GROUP, n_chunks))]
                atts = [jnp.where(mask, _dot_nt(qe_s[d, rows, :], ke_s[d, rows, :]), 0.0) for _, rows in group]
                for (c, rows), att in zip(group, atts):
                    emit(d, c, rows, att)

    @pl.when(jnp.logical_not(mild_decay))
    def _():
        for d in range(2):
            def safe_body(c, carry, d=d):
                _, rows = chunk_rows(c)
                emit(d, c, rows, _hg_att_safe(q_s[rows, :], k_s[d, rows, :], lf_s[d, rows, :], d == 1))
                return carry

            lax.fori_loop(0, n_chunks, safe_body, 0)

    o = acc_s[...]
    o = o * lax.rsqrt(jnp.mean(o * o, axis=-1, keepdims=True) + EPS) * ng_ref[...]
    o_ref[...] = o * _silu(o_ref[...])


def _hgrn2_call(u, w, lb, ng, *, chunk=HG_CHUNK):
    n_b, seq, d = u.shape
    kern = functools.partial(_hgrn2_kernel, seq=seq, chunk=chunk)
    return pl.pallas_call(
        kern,
        out_shape=jax.ShapeDtypeStruct((n_b, seq, HG_H * HEAD), F32),
        grid=(n_b, HG_H),
        in_specs=[pl.BlockSpec((None, seq, d), lambda b, h: (b, 0, 0)),
                  pl.BlockSpec((None, d, 5 * HEAD), lambda b, h: (h, 0, 0)),
                  pl.BlockSpec((None, 2, HEAD), lambda b, h: (h, 0, 0)),
                  pl.BlockSpec((None, 1, HEAD), lambda b, h: (h, 0, 0))],
        out_specs=pl.BlockSpec((None, seq, HEAD), lambda b, h: (b, 0, h)),
        scratch_shapes=[pltpu.VMEM((seq, HEAD), F32),
                        pltpu.VMEM((seq, HEAD), BF16),
                        pltpu.VMEM((2, seq, HEAD), F32),
                        pltpu.VMEM((2, seq, HEAD), F32),
                        pltpu.VMEM((2, seq, HEAD), F32),
                        pltpu.VMEM((2, seq, HEAD), BF16),
                        pltpu.VMEM((2, seq, HEAD), BF16),
                        pltpu.VMEM((2, seq, HEAD), BF16),
                        pltpu.VMEM((seq, HEAD), BF16),
                        pltpu.VMEM((seq // chunk, HEAD, HEAD), F32),
                        pltpu.VMEM((seq // chunk, HEAD, HEAD), F32),
                        pltpu.VMEM((2, seq // chunk, HEAD, HEAD), BF16),
                        pltpu.VMEM((seq, HEAD), F32)],
        compiler_params=pltpu.CompilerParams(dimension_semantics=("parallel", "arbitrary"),
                                             vmem_limit_bytes=VMEM_LIMIT),
        name="hgrn2_branch",
    )(u, w, lb, ng)


def _log_sigmoid(x):
    return jnp.minimum(x, 0.0) - jnp.log1p(jnp.exp(-jnp.abs(x)))


def _mlstm_kernel(u_ref, w_ref, bias_ref, ng_ref, o_ref,
                  qb_s, kb_s, va_s, tm_s, qw_s, kw_s, mx_s, en_s, gt_s, dec_s, upd_s, ct_s,
                  acc_s, *, seq, chunk):
    n_chunks = seq // chunk
    proj = _dot(u_ref[...], w_ref[...])
    q = proj[:, 0:HEAD]
    k = proj[:, HEAD:2 * HEAD] * (HEAD ** -0.5)
    qb_s[...] = q.astype(BF16)
    kb_s[...] = k.astype(BF16)
    va_s[:, 0:HEAD] = proj[:, 2 * HEAD:3 * HEAD].astype(BF16)
    va_s[:, HEAD:2 * HEAD] = jnp.ones((seq, HEAD), BF16)
    o_ref[...] = proj[:, 3 * HEAD:4 * HEAD]

    x = (proj[:, 4 * HEAD:5 * HEAD] + bias_ref[...]).T[0:SUBLANES, :]
    fwd_row = lax.broadcasted_iota(jnp.int32, (SUBLANES, seq), 0) == 0
    pos = lax.broadcasted_iota(jnp.int32, (SUBLANES, seq), 1)

    def scans(y, op, seg, start=1):
        return _lane_scan(y, seg, False, op, start), _lane_scan(y, seg, True, op, start)

    it = x
    log_f = pltpu.roll(_log_sigmoid(x), SUBLANES - 2, axis=0)
    pre, suf = scans(log_f, "add", chunk)
    bt = jnp.where(fwd_row, pre, suf)
    bt_end = pre + suf - log_f
    g = it - bt
    g_pre, g_suf = scans(g, "max", chunk)
    cm = jnp.where(fwd_row, g_pre, g_suf)
    cm_end = jnp.maximum(g_pre, g_suf)

    p_pre, p_suf = scans(bt_end, "add", seq, chunk)
    p_sum = jnp.where(fwd_row, p_pre, p_suf)
    x_pre, x_suf = scans(cm_end - (p_sum - bt_end), "max", seq, chunk)
    m_end = p_sum + jnp.where(fwd_row, x_pre, x_suf)
    m_prev = jnp.where(fwd_row,
                       jnp.where(pos < chunk, NEG, pltpu.roll(m_end, chunk, axis=1)),
                       jnp.where(pos >= seq - chunk, NEG, pltpu.roll(m_end, seq - chunk, axis=1)))
    mx = jnp.maximum(m_prev, cm)
    w_inter = jnp.exp(m_prev - mx)
    wk = jnp.exp(bt_end - bt + it - m_end)
    en = jnp.exp(-(bt + mx))
    decay = jnp.exp(bt_end + m_prev - m_end)

    for c in range(n_chunks):
        gt_s[c] = g[:, c * chunk:(c + 1) * chunk]
    row = lax.broadcasted_iota(jnp.int32, (SUBLANES, seq), 0)
    packed = jnp.where(row < 2, w_inter,
                       jnp.where(row < 4, pltpu.roll(wk, 2, axis=0),
                                 jnp.where(row < 6, pltpu.roll(mx, 4, axis=0), pltpu.roll(en, 6, axis=0))))
    tm_s[...] = jnp.concatenate(
        [packed, decay, jnp.zeros((LANES - 2 * SUBLANES, seq), F32)], axis=0).T
    for d in range(2):
        def lanes(col, ref_rows=slice(None)):
            block = tm_s[ref_rows, :]
            return jnp.broadcast_to(block[:, col:col + 1], (block.shape[0], LANES))
        qw_s[d] = (q * lanes(d)).astype(BF16)
        kw_s[d] = (k * lanes(2 + d)).astype(BF16)
        mx_s[d] = lanes(4 + d)
        en_s[d] = lanes(6 + d)
        dec_s[d] = lanes(8 + d, pl.ds(0, n_chunks, stride=chunk))

    for d in range(2):
        reverse = d == 1
        mask = _tri_mask(chunk, reverse)

        def chunk_rows(c):
            return pl.ds(pl.multiple_of(c * chunk, chunk), chunk)

        for c in range(n_chunks):
            rows = slice(c * chunk, (c + 1) * chunk)
            upd_s[c] = _dot_tn(kw_s[d, rows, :], va_s[rows, :])

        ct = jnp.zeros((HEAD, 2 * HEAD), F32)
        for c in (range(n_chunks - 1, -1, -1) if reverse else range(n_chunks)):
            ct_s[c] = ct.astype(BF16)
            dec = dec_s[d, c:c + 1, :]
            ct = ct * jnp.concatenate([dec, dec], axis=1) + upd_s[c]

        def scores(c, rows):
            p = jnp.exp(jnp.where(mask, gt_s[c][d:d + 1, :] - mx_s[d, rows, :], NEG))
            return (_dot_nt(qb_s[rows, :], kb_s[rows, :]) * p).astype(BF16)

        def emit(c, rows, sc):
            tot = _dot(jnp.concatenate([qw_s[d, rows, :], sc], axis=1),
                       jnp.concatenate([ct_s[c], va_s[rows, :]], axis=0))
            hval = tot[:, 0:HEAD] / jnp.maximum(jnp.abs(tot[:, HEAD:2 * HEAD]), en_s[d, rows, :])
            if d == 0:
                acc_s[rows, :] = hval
            else:
                acc_s[rows, :] = acc_s[rows, :] + hval

        for g0 in range(0, n_chunks, ML_GROUP):
            group = [(c, slice(c * chunk, (c + 1) * chunk)) for c in range(g0, min(g0 + ML_GROUP, n_chunks))]
            staged = [scores(c, rows) for c, rows in group]
            for (c, rows), sc in zip(group, staged):
                emit(c, rows, sc)

    hsum = acc_s[...]
    hc = hsum - jnp.mean(hsum, axis=-1, keepdims=True)
    hn = hc * lax.rsqrt(jnp.mean(hc * hc, axis=-1, keepdims=True) + EPS) * ng_ref[...]
    o_ref[...] = hn * _sigmoid(o_ref[...])


def _mlstm_call(u, w, bias, ng, *, chunk=ML_CHUNK):
    n_b, seq, d = u.shape
    n_chunks = seq // chunk
    kern = functools.partial(_mlstm_kernel, seq=seq, chunk=chunk)
    return pl.pallas_call(
        kern,
        out_shape=jax.ShapeDtypeStruct((n_b, seq, ML_H * HEAD), F32),
        grid=(n_b, ML_H),
        in_specs=[pl.BlockSpec((None, seq, d), lambda b, h: (b, 0, 0)),
                  pl.BlockSpec((None, d, 5 * HEAD), lambda b, h: (h, 0, 0)),
                  pl.BlockSpec((None, 1, HEAD), lambda b, h: (h, 0, 0)),
                  pl.BlockSpec((None, 1, HEAD), lambda b, h: (h, 0, 0))],
        out_specs=pl.BlockSpec((None, seq, HEAD), lambda b, h: (b, 0, h)),
        scratch_shapes=[pltpu.VMEM((seq, HEAD), BF16),
                        pltpu.VMEM((seq, HEAD), BF16),
                        pltpu.VMEM((seq, 2 * HEAD), BF16),
                        pltpu.VMEM((seq, LANES), F32),
                        pltpu.VMEM((2, seq, HEAD), BF16),
                        pltpu.VMEM((2, seq, HEAD), BF16),
                        pltpu.VMEM((2, seq, LANES), F32),
                        pltpu.VMEM((2, seq, LANES), F32),
                        pltpu.VMEM((n_chunks, SUBLANES, chunk), F32),
                        pltpu.VMEM((2, n_chunks, LANES), F32),
                        pltpu.VMEM((n_chunks, HEAD, 2 * HEAD), F32),
                        pltpu.VMEM((n_chunks, HEAD, 2 * HEAD), BF16),
                        pltpu.VMEM((seq, HEAD), F32)],
        compiler_params=pltpu.CompilerParams(dimension_semantics=("parallel", "arbitrary"),
                                             vmem_limit_bytes=VMEM_LIMIT),
        name="mlstm_branch",
    )(u, w, bias, ng)


_SSD_W_Z = 0
_SSD_W_XBC = SSD_GW
_SSD_W_DT = SSD_GW + SSD_GW + 2 * SSD_N
_SSD_W_END = _SSD_W_DT + LANES
_SSD_XBC_G = SSD_GW + 2 * SSD_N
_SSD_PAD = SUBLANES
_SSD_PAIRS = SSD_HG // 2


def _ssd_kernel(u_ref, w_ref, cw_ref, cb_ref, dtb_ref, alog_ref, dsk_ref, ng_ref, o_ref,
                pad_s, xs_s, b_s, c_s, a_s, at_s, dtt_s, cbm_s, ea_s, xw_s, dec_s, cur_s, st_s,
                *, seq, chunk, rblk):
    n_chunks = seq // chunk
    n_rblk = seq // rblk

    zero_rows = jnp.zeros((_SSD_PAD, _SSD_XBC_G), F32)
    pad_s[0:_SSD_PAD, :] = zero_rows
    pad_s[_SSD_PAD + seq:2 * _SSD_PAD + seq, :] = zero_rows
    for i in range(n_rblk):
        r0 = i * rblk
        pad_s[_SSD_PAD + r0:_SSD_PAD + r0 + rblk, :] = _dot(
            u_ref[r0:r0 + rblk, :], w_ref[:, _SSD_W_XBC:_SSD_W_DT])
    for i in range(n_rblk):
        r0 = i * rblk
        acc = cb_ref[...]
        slab = pad_s[r0:r0 + rblk + 2 * _SSD_PAD, :]
        for j in range(SSD_CONV):
            tap = slab if j == SSD_CONV // 2 else pltpu.roll(
                slab, (SSD_CONV // 2 - j) % (rblk + 2 * _SSD_PAD), axis=0)
            acc = acc + tap[_SSD_PAD:_SSD_PAD + rblk, :] * cw_ref[j:j + 1, :]
        xbc = _silu(acc)
        xs_s[r0:r0 + rblk, :] = xbc[:, 0:SSD_GW]
        b_s[r0:r0 + rblk, :] = xbc[:, SSD_GW:SSD_GW + SSD_N].astype(BF16)
        c_s[r0:r0 + rblk, :] = xbc[:, SSD_GW + SSD_N:SSD_GW + 2 * SSD_N].astype(BF16)

    n_hd = 2 * SSD_HG
    dt_tok = _softplus(_dot(u_ref[...], w_ref[:, _SSD_W_DT:_SSD_W_END]) + dtb_ref[...])
    dt = dt_tok.T[0:n_hd, :]
    log_a = (dt_tok * (-jnp.exp(alog_ref[...]))).T[0:n_hd, :]
    fwd_row = lax.broadcasted_iota(jnp.int32, (n_hd, seq), 0) < SSD_HG
    pre = _lane_scan(log_a, chunk, False, "add")
    suf = _lane_scan(log_a, chunk, True, "add")
    acum = jnp.where(fwd_row, pre, suf)
    a_end = pre + suf - log_a
    for c in range(n_chunks):
        lanes_c = slice(c * chunk, (c + 1) * chunk)
        at_s[c] = acum[:, lanes_c]
        dtt_s[c] = dt[:, lanes_c]
        r0 = c * chunk
        cbm_s[r0:r0 + chunk, :] = _dot_nt(c_s[r0:r0 + chunk, :], b_s[r0:r0 + chunk, :])
    a_s[...] = jnp.concatenate(
        [acum, jnp.exp(acum), dt * jnp.exp(a_end - acum), jnp.exp(a_end),
         jnp.zeros((LANES - 4 * n_hd, seq), F32)], axis=0).T
    lane_lo = lax.broadcasted_iota(jnp.int32, (chunk, LANES), 1) < SSD_HEAD_DIM

    def split3(x):
        hi = x.astype(BF16)
        r1 = x - hi.astype(F32)
        mid = r1.astype(BF16)
        return [hi, mid, (r1 - mid.astype(F32)).astype(BF16)]

    def spread(first_lane, pieces):
        one = (lax.broadcasted_iota(jnp.int32, (LANES, SSD_GW), 1) // SSD_HEAD_DIM
               == lax.broadcasted_iota(jnp.int32, (LANES, SSD_GW), 0) - first_lane).astype(BF16)
        return jnp.concatenate([one] * pieces, axis=0)

    for d in range(2):
        reverse = d == 1
        mask = _tri_mask(chunk, reverse)
        spread_ea = spread(n_hd + d * SSD_HG, 1)
        spread_wt = spread(2 * n_hd + d * SSD_HG, 2)
        dec_s[...] = _dot(jnp.concatenate(split3(a_s[pl.ds(0, n_chunks, stride=chunk), :]), axis=1),
                          spread(3 * n_hd + d * SSD_HG, 3))
        for i in range(n_rblk):
            rows = slice(i * rblk, (i + 1) * rblk)
            hi, mid, _ = split3(a_s[rows, :])
            ea_s[rows, :] = _dot(hi, spread_ea).astype(BF16)
            xw_s[rows, :] = (xs_s[rows, :] * _dot(jnp.concatenate([hi, mid], axis=1), spread_wt)).astype(BF16)

        def chunk_rows(c):
            return pl.ds(pl.multiple_of(c * chunk, chunk), chunk)

        for c in range(n_chunks):
            rows = slice(c * chunk, (c + 1) * chunk)
            pad_s[rows, 0:SSD_GW] = _dot_tn(b_s[rows, :], xw_s[rows, :])

        cur_s[...] = jnp.zeros_like(cur_s)

        for c in (range(n_chunks - 1, -1, -1) if reverse else range(n_chunks)):
            cur = cur_s[...]
            st_s[c] = cur.astype(BF16)
            cur_s[...] = cur * dec_s[c:c + 1, :] + pad_s[c * chunk:(c + 1) * chunk, 0:SSD_GW]

        def out_body(c, carry, mask=mask, d=d):
            rows = chunk_rows(c)
            blk = a_s[rows, :]
            blk_t = at_s[c]
            dtt = dtt_s[c]
            cbm = cbm_s[rows, :]
            y_inter = _dot(c_s[rows, :], st_s[c])
            for p in range(_SSD_PAIRS):
                cols = slice(p * LANES, (p + 1) * LANES)
                xs_pair = xs_s[rows, cols]
                m_tiles, x_tiles = [], []
                for hh in range(2):
                    col = d * SSD_HG + 2 * p + hh
                    seg = jnp.exp(jnp.where(mask, blk[:, col:col + 1] - blk_t[col:col + 1, :], NEG))
                    m_tiles.append((cbm * seg * dtt[col:col + 1, :]).astype(BF16))
                    keep = lane_lo if hh == 0 else jnp.logical_not(lane_lo)
                    x_tiles.append(jnp.where(keep, xs_pair, 0.0).astype(BF16))
                y = (y_inter[:, cols] * ea_s[rows, cols].astype(F32)
                     + _dot(jnp.concatenate(m_tiles, axis=1), jnp.concatenate(x_tiles, axis=0)))
                if d == 0:
                    o_ref[rows, cols] = y
                else:
                    o_ref[rows, cols] = o_ref[rows, cols] + y
            return carry

        lax.fori_loop(0, n_chunks, out_body, 0, unroll=SSD_UNROLL)

    for i in range(n_rblk):
        r0 = i * rblk
        rows = slice(r0, r0 + rblk)
        z = _dot(u_ref[rows, :], w_ref[:, _SSD_W_Z:_SSD_W_XBC])
        y = (o_ref[rows, :] + xs_s[rows, :] * dsk_ref[...]) * _silu(z)
        o_ref[rows, :] = y * lax.rsqrt(jnp.mean(y * y, axis=-1, keepdims=True) + EPS) * ng_ref[...]


def _ssd_call(u, w, cw, cb, dtb, alog, dsk, ng, *, chunk=SSD_CHUNK):
    n_b, seq, d = u.shape
    rblk = min(seq, 512)
    kern = functools.partial(_ssd_kernel, seq=seq, chunk=chunk, rblk=rblk)
    vec = lambda n: pl.BlockSpec((None, 1, n), lambda b, g: (g, 0, 0))
    return pl.pallas_call(
        kern,
        out_shape=jax.ShapeDtypeStruct((n_b, seq, SSD_G * SSD_GW), F32),
        grid=(n_b, SSD_G),
        in_specs=[pl.BlockSpec((None, seq, d), lambda b, g: (b, 0, 0)),
                  pl.BlockSpec((None, d, _SSD_W_END), lambda b, g: (g, 0, 0)),
                  pl.BlockSpec((None, SSD_CONV, _SSD_XBC_G), lambda b, g: (g, 0, 0)),
                  vec(_SSD_XBC_G), vec(LANES), vec(LANES), vec(SSD_GW), vec(SSD_GW)],
        out_specs=pl.BlockSpec((None, seq, SSD_GW), lambda b, g: (b, 0, g)),
        scratch_shapes=[pltpu.VMEM((seq + 2 * _SSD_PAD, _SSD_XBC_G), F32),
                        pltpu.VMEM((seq, SSD_GW), F32),
                        pltpu.VMEM((seq, SSD_N), BF16),
                        pltpu.VMEM((seq, SSD_N), BF16),
                        pltpu.VMEM((seq, LANES), F32),
                        pltpu.VMEM((seq // chunk, 2 * SSD_HG, chunk), F32),
                        pltpu.VMEM((seq // chunk, 2 * SSD_HG, chunk), F32),
                        pltpu.VMEM((seq, chunk), F32),
                        pltpu.VMEM((seq, SSD_GW), BF16),
                        pltpu.VMEM((seq, SSD_GW), BF16),
                        pltpu.VMEM((seq // chunk, SSD_GW), F32),
                        pltpu.VMEM((SSD_N, SSD_GW), F32),
                        pltpu.VMEM((seq // chunk, SSD_N, SSD_GW), BF16)],
        compiler_params=pltpu.CompilerParams(dimension_semantics=("parallel", "arbitrary"),
                                             vmem_limit_bytes=VMEM_LIMIT),
        name="ssd_branch",
    )(u, w, cw, cb, dtb, alog, dsk, ng)


def _merge_kernel(u_ref, h_ref, y0_ref, y1_ref, y2_ref, wg_ref, wb_ref, wo_ref, o_ref):
    u = u_ref[...]
    merged = None
    for r, y_ref in enumerate((y0_ref, y1_ref, y2_ref)):
        term = _sigmoid(_dot(u, wg_ref[r])) * _dot(y_ref[...].astype(BF16), wb_ref[r])
        merged = term if merged is None else merged + term
    o_ref[...] = h_ref[...] + _dot(merged.astype(BF16), wo_ref[...])


def _merge_call(u2d, h2d, y0, y1, y2, wg, wb, wo, tm=512):
    n_tok, d = h2d.shape
    tm = min(tm, n_tok)
    tok = lambda: pl.BlockSpec((tm, d), lambda i: (i, 0))
    held = lambda shape: pl.BlockSpec(shape, lambda i: tuple(0 for _ in shape), pipeline_mode=pl.Buffered(1))
    return pl.pallas_call(
        _merge_kernel,
        out_shape=jax.ShapeDtypeStruct((n_tok, d), F32),
        grid=(n_tok // tm,),
        in_specs=[tok(), tok(), tok(), tok(), tok(),
                  held((3, d, d)), held((3, d, d)), held((d, d))],
        out_specs=tok(),
        compiler_params=pltpu.CompilerParams(dimension_semantics=("parallel",),
                                             vmem_limit_bytes=VMEM_LIMIT),
        name="merge",
    )(u2d, h2d, y0, y1, y2, wg, wb, wo)


_FFN_CW = 256
_FFN_NJ = D_FF // _FFN_CW
_FFN_HALO = SUBLANES

def _ffn_kernel(h_ref, hp_ref, hn_ref, p_ref, g_ref, wa_ref, wv_ref, cwa_ref, cwv_ref,
                cba_ref, cbv_ref, wd_ref, wple_ref, wpg_ref, fg_ref, o_ref, u_s,
                *, tm, n_tiles, final_norm):
    i = pl.program_id(1)
    j = pl.program_id(2)

    def norm(x):
        return (x * lax.rsqrt(jnp.mean(x * x, axis=-1, keepdims=True) + EPS) * g_ref[...]).astype(BF16)

    @pl.when(j == 0)
    def _():
        up = jnp.where(i > 0, 1.0, 0.0)
        dn = jnp.where(i < n_tiles - 1, 1.0, 0.0)
        u_s[0:_FFN_HALO, :] = norm(hp_ref[...] * up)
        u_s[_FFN_HALO:_FFN_HALO + tm, :] = norm(h_ref[...])
        u_s[_FFN_HALO + tm:2 * _FFN_HALO + tm, :] = norm(hn_ref[...] * dn)
        o_ref[...] = h_ref[...]

    def conv(w_ref, cw_ref, cb_ref):
        up = _dot(u_s[...], w_ref[...])
        n_rows = tm + 2 * _FFN_HALO
        acc = cb_ref[...]
        for t in range(FFN_CONV):
            tap = up if t == FFN_CONV // 2 else pltpu.roll(up, (FFN_CONV // 2 - t) % n_rows, axis=0)
            acc = acc + tap[_FFN_HALO:_FFN_HALO + tm, :] * cw_ref[t:t + 1, :]
        return acc

    act = (_silu(conv(wa_ref, cwa_ref, cba_ref)) * conv(wv_ref, cwv_ref, cbv_ref)).astype(BF16)
    o_ref[...] += _dot(act, wd_ref[...])

    @pl.when(j == _FFN_NJ - 1)
    def _():
        h2 = o_ref[...]
        ple = _dot(p_ref[...].astype(BF16), wple_ref[...])
        h3 = h2 + ple * _sigmoid(_dot(h2.astype(BF16), wpg_ref[...]))
        if final_norm:
            h3 = h3 * lax.rsqrt(jnp.mean(h3 * h3, axis=-1, keepdims=True) + EPS) * fg_ref[...]
        o_ref[...] = h3


def _ffn_call(h, p, g, w_up, cw, cb, w_down, w_ple, w_pg, fg, *, final_norm, tm=1024):
    n_b, seq, d = h.shape
    tm = min(tm, seq)
    n_tiles = seq // tm
    hb = tm // _FFN_HALO
    n_hb = seq // _FFN_HALO
    kern = functools.partial(_ffn_kernel, tm=tm, n_tiles=n_tiles, final_norm=final_norm)
    cst = lambda shape: pl.BlockSpec(shape, lambda b, i, j: tuple(0 for _ in shape))
    return pl.pallas_call(
        kern,
        out_shape=jax.ShapeDtypeStruct((n_b, seq, d), F32),
        grid=(n_b, n_tiles, _FFN_NJ),
        in_specs=[pl.BlockSpec((None, tm, d), lambda b, i, j: (b, i, 0)),
                  pl.BlockSpec((None, _FFN_HALO, d), lambda b, i, j: (b, jnp.maximum(i * hb - 1, 0), 0)),
                  pl.BlockSpec((None, _FFN_HALO, d), lambda b, i, j: (b, jnp.minimum((i + 1) * hb, n_hb - 1), 0)),
                  pl.BlockSpec((None, tm, PLE_DIM), lambda b, i, j: (b, i, 0)),
                  cst((1, d)),
                  pl.BlockSpec((d, _FFN_CW), lambda b, i, j: (0, j)),
                  pl.BlockSpec((d, _FFN_CW), lambda b, i, j: (0, _FFN_NJ + j)),
                  pl.BlockSpec((FFN_CONV, _FFN_CW), lambda b, i, j: (0, j)),
                  pl.BlockSpec((FFN_CONV, _FFN_CW), lambda b, i, j: (0, _FFN_NJ + j)),
                  pl.BlockSpec((1, _FFN_CW), lambda b, i, j: (0, j)),
                  pl.BlockSpec((1, _FFN_CW), lambda b, i, j: (0, _FFN_NJ + j)),
                  pl.BlockSpec((_FFN_CW, d), lambda b, i, j: (j, 0)),
                  cst((PLE_DIM, d)), cst((d, d)), cst((1, d))],
        out_specs=pl.BlockSpec((None, tm, d), lambda b, i, j: (b, i, 0)),
        scratch_shapes=[pltpu.VMEM((tm + 2 * _FFN_HALO, d), BF16)],
        compiler_params=pltpu.CompilerParams(
            dimension_semantics=("parallel", "parallel", "arbitrary"),
            vmem_limit_bytes=VMEM_LIMIT),
        name="convffn_ple",
    )(h, h, h, p, g.reshape(1, d), w_up, w_up, cw, cw, cb.reshape(1, -1), cb.reshape(1, -1),
      w_down, w_ple, w_pg, fg.reshape(1, d))


def _cols(w, start, width):
    return lax.slice_in_dim(w, start, start + width, axis=1)


def _pad_cols(w, width):
    return jnp.pad(w, ((0, 0), (0, width - w.shape[1])))


def _ssd_params(w_in, conv_w, conv_b, dt_bias, a_log, d_skip, norm_g):
    ws, cws, cbs, dtbs, alogs = [], [], [], [], []
    for g in range(SSD_G):
        dt_cols = jnp.concatenate(
            [_cols(w_in, _OFF_DT + dd * SSD_H + g * SSD_HG, SSD_HG) for dd in range(2)], axis=1)
        ws.append(jnp.concatenate([
            _cols(w_in, _OFF_Z + g * SSD_GW, SSD_GW),
            _cols(w_in, _OFF_XBC + g * SSD_GW, SSD_GW),
            _cols(w_in, _OFF_XBC + D_MODEL + g * SSD_N, SSD_N),
            _cols(w_in, _OFF_XBC + D_MODEL + SSD_G * SSD_N + g * SSD_N, SSD_N),
            _pad_cols(dt_cols, LANES)], axis=1))
        pick = lambda a: jnp.concatenate([
            _cols(a, g * SSD_GW, SSD_GW),
            _cols(a, D_MODEL + g * SSD_N, SSD_N),
            _cols(a, D_MODEL + SSD_G * SSD_N + g * SSD_N, SSD_N)], axis=1)
        cws.append(pick(conv_w))
        cbs.append(pick(conv_b.reshape(1, -1)))
        head_row = lambda a: _pad_cols(
            jnp.concatenate([a[dd, g * SSD_HG:(g + 1) * SSD_HG] for dd in range(2)]).reshape(1, -1), LANES)
        dtbs.append(head_row(dt_bias))
        alogs.append(head_row(a_log))
    dsk = jnp.repeat(d_skip, SSD_HEAD_DIM).reshape(SSD_G, 1, SSD_GW)
    return (jnp.stack(ws).astype(BF16), jnp.stack(cws), jnp.stack(cbs), jnp.stack(dtbs),
            jnp.stack(alogs), dsk, norm_g.reshape(SSD_G, 1, SSD_GW))


def _hgrn2_params(w_in, lb, norm_g):
    ws = [jnp.concatenate([
        _cols(w_in, _OFF_GQ + h * HEAD, HEAD),
        _cols(w_in, _OFF_GF + h * HEAD, HEAD),
        _cols(w_in, _OFF_GF + D_MODEL + h * HEAD, HEAD),
        _cols(w_in, _OFF_GI + h * HEAD, HEAD),
        _cols(w_in, _OFF_GG + h * HEAD, HEAD)], axis=1) for h in range(HG_H)]
    lbs = lb.reshape(2, HG_H, HEAD).swapaxes(0, 1)
    return jnp.stack(ws).astype(BF16), lbs, norm_g.reshape(HG_H, 1, HEAD)


def _mlstm_params(w_in, i_bias, f_bias, norm_g):
    ws, biases = [], []
    for h in range(ML_H):
        gate_cols = jnp.concatenate([
            _cols(w_in, _OFF_MI + h, 1), _cols(w_in, _OFF_MI + ML_H + h, 1),
            _cols(w_in, _OFF_MF + h, 1), _cols(w_in, _OFF_MF + ML_H + h, 1)], axis=1)
        ws.append(jnp.concatenate([
            _cols(w_in, _OFF_MQ + h * HEAD, HEAD),
            _cols(w_in, _OFF_MK + h * HEAD, HEAD),
            _cols(w_in, _OFF_MV + h * HEAD, HEAD),
            _cols(w_in, _OFF_MO + h * HEAD, HEAD),
            _pad_cols(gate_cols, LANES)], axis=1))
        biases.append(_pad_cols(
            jnp.stack([i_bias[0, h], i_bias[1, h], f_bias[0, h], f_bias[1, h]]).reshape(1, 4), LANES))
    return jnp.stack(ws).astype(BF16), jnp.stack(biases), norm_g.reshape(ML_H, 1, HEAD)


def kernel(x, p, norm_mix_g, w_in, ssd_conv_w, ssd_conv_b, ssd_dt_bias, ssd_a_log, ssd_d, ssd_norm_g, hg_lb_raw, hg_norm_g, ml_i_bias, ml_f_bias, ml_norm_g, w_br_ssd, w_br_hg, w_br_ml, w_out, norm_ffn_g, w_up, ffn_conv_w, ffn_conv_b, w_down, w_ple, w_ple_gate, final_norm_g):
    n_b, seq, d = x.shape
    depth = w_in.shape[0]
    lb_soft = jax.nn.softmax(hg_lb_raw.astype(F32), axis=0)
    hg_lb = jnp.cumsum(lb_soft, axis=0) - lb_soft[0:1]
    h = x
    for l in range(depth):
        u2d = _rmsnorm_call(h.reshape(n_b * seq, d), norm_mix_g[l], BF16)
        u = u2d.reshape(n_b, seq, d)
        y_ssd = _ssd_call(u, *_ssd_params(w_in[l], ssd_conv_w[l], ssd_conv_b[l], ssd_dt_bias[l],
                                          ssd_a_log[l], ssd_d[l], ssd_norm_g[l]))
        y_hg = _hgrn2_call(u, *_hgrn2_params(w_in[l], hg_lb[l], hg_norm_g[l]))
        y_ml = _mlstm_call(u, *_mlstm_params(w_in[l], ml_i_bias[l], ml_f_bias[l], ml_norm_g[l]))
        wg = jnp.stack([_cols(w_in[l], _OFF_GATES + r * d, d) for r in range(3)]).astype(BF16)
        wb = jnp.stack([w_br_ssd[l], w_br_hg[l], w_br_ml[l]]).astype(BF16)
        tok = lambda a: a.reshape(n_b * seq, d)
        h = _merge_call(u2d, tok(h), tok(y_ssd), tok(y_hg), tok(y_ml), wg, wb,
                        w_out[l].astype(BF16)).reshape(n_b, seq, d)
        h = _ffn_call(h, p[l], norm_ffn_g[l], w_up[l].astype(BF16), ffn_conv_w[l], ffn_conv_b[l],
                      w_down[l].astype(BF16), w_ple[l].astype(BF16), w_ple_gate[l].astype(BF16),
                      final_norm_g, final_norm=(l == depth - 1))
    return h
```

```python
import functools

import jax
import jax.numpy as jnp
from jax import lax
from jax.experimental import pallas as pl
from jax.experimental.pallas import tpu as pltpu

F32 = jnp.float32
BF16 = jnp.bfloat16

D_MODEL = 1024
PLE_DIM = 256
EPS = 1e-6
NEG = -1e30
TINY = 1e-30

SSD_HEAD_DIM = 64
SSD_H = 16
SSD_G = 2
SSD_HG = 8
SSD_N = 128
SSD_CONV = 5
SSD_GW = SSD_HG * SSD_HEAD_DIM
SSD_XBC = D_MODEL + 2 * SSD_G * SSD_N
HG_H = 8
ML_H = 8
HEAD = 128
D_FF = 2816
FFN_CONV = 3

LANES = 128
SUBLANES = 8
VMEM_LIMIT = 56 * 1024 * 1024

SSD_CHUNK = 128
HG_CHUNK = 64
ML_CHUNK = 128
SSD_UNROLL = 8
HG_UNROLL = 32
HG_GROUP = 16
ML_GROUP = 8
HG_SAFE_DECAY = 60.0

_OFF_Z = 0
_OFF_XBC = 1024
_OFF_DT = _OFF_XBC + SSD_XBC
_OFF_GQ = _OFF_DT + 2 * SSD_H
_OFF_GF = _OFF_GQ + 1024
_OFF_GI = _OFF_GF + 2048
_OFF_GG = _OFF_GI + 1024
_OFF_MQ = _OFF_GG + 1024
_OFF_MK = _OFF_MQ + 1024
_OFF_MV = _OFF_MK + 1024
_OFF_MI = _OFF_MV + 1024
_OFF_MF = _OFF_MI + 2 * ML_H
_OFF_MO = _OFF_MF + 2 * ML_H
_OFF_GATES = _OFF_MO + 1024


def _dot(a, b):
    return jnp.dot(a, b, preferred_element_type=F32)


def _dot_nt(a, b):
    return lax.dot_general(a, b, (((1,), (1,)), ((), ())), preferred_element_type=F32)


def _dot_tn(a, b):
    return lax.dot_general(a, b, (((0,), (0,)), ((), ())), preferred_element_type=F32)


def _sigmoid(x):
    return 1.0 / (1.0 + jnp.exp(-x))


def _silu(x):
    return x * _sigmoid(x)


def _softplus(x):
    return jnp.maximum(x, 0.0) + jnp.log1p(jnp.exp(-jnp.abs(x)))


def _chunk_scan(x, chunk, reverse, op):
    n_rows = x.shape[0]
    row = lax.broadcasted_iota(jnp.int32, x.shape, 0) & (chunk - 1)
    sh = 1
    while sh < chunk:
        if reverse:
            shifted = pltpu.roll(x, n_rows - sh, axis=0)
            ok = row < chunk - sh
        else:
            shifted = pltpu.roll(x, sh, axis=0)
            ok = row >= sh
        if op == "add":
            x = x + jnp.where(ok, shifted, 0.0)
        else:
            x = jnp.maximum(x, jnp.where(ok, shifted, NEG))
        sh *= 2
    return x


def _lane_scan(x, seg, reverse, op, start=1):
    n_lanes = x.shape[1]
    pos = lax.broadcasted_iota(jnp.int32, x.shape, 1) & (seg - 1)
    sh = start
    while sh < seg:
        if reverse:
            shifted = pltpu.roll(x, n_lanes - sh, axis=1)
            ok = pos < seg - sh
        else:
            shifted = pltpu.roll(x, sh, axis=1)
            ok = pos >= sh
        if op == "add":
            x = x + jnp.where(ok, shifted, 0.0)
        else:
            x = jnp.maximum(x, jnp.where(ok, shifted, NEG))
        sh *= 2
    return x


def _tri_mask(n, reverse):
    t = lax.broadcasted_iota(jnp.int32, (n, n), 0)
    s = lax.broadcasted_iota(jnp.int32, (n, n), 1)
    return (s >= t) if reverse else (s <= t)


def _rmsnorm_kernel(h_ref, g_ref, o_ref):
    x = h_ref[...]
    ms = jnp.mean(x * x, axis=-1, keepdims=True)
    o_ref[...] = (x * lax.rsqrt(ms + EPS) * g_ref[...]).astype(o_ref.dtype)


def _rmsnorm_call(h2d, g, out_dtype, tm=1024):
    n_tok, d = h2d.shape
    tm = min(tm, n_tok)
    return pl.pallas_call(
        _rmsnorm_kernel,
        out_shape=jax.ShapeDtypeStruct((n_tok, d), out_dtype),
        grid=(n_tok // tm,),
        in_specs=[pl.BlockSpec((tm, d), lambda i: (i, 0)),
                  pl.BlockSpec((1, d), lambda i: (0, 0))],
        out_specs=pl.BlockSpec((tm, d), lambda i: (i, 0)),
        compiler_params=pltpu.CompilerParams(dimension_semantics=("parallel",),
                                             vmem_limit_bytes=VMEM_LIMIT),
        name="rmsnorm",
    )(h2d, g.reshape(1, d))


def _hg_att_safe(qc, kc, lfc, reverse):
    n = qc.shape[0]
    t = lax.broadcasted_iota(jnp.int32, (n, n), 0)
    s = lax.broadcasted_iota(jnp.int32, (n, n), 1)
    att = jnp.where(t == s, jnp.sum(qc * kc, axis=-1, keepdims=True), 0.0)
    half = 1
    while half < n:
        pre = _chunk_scan(lfc, half, False, "add")
        suf = _chunk_scan(lfc, half, True, "add")
        t_hi = (t & half) != 0
        s_hi = (s & half) != 0
        same_block = (t ^ s) < 2 * half
        if reverse:
            qe = qc * jnp.exp(suf)
            ke = kc * jnp.exp(pre - lfc)
            pair = jnp.logical_and(same_block, jnp.logical_and(jnp.logical_not(t_hi), s_hi))
        else:
            qe = qc * jnp.exp(pre)
            ke = kc * jnp.exp(suf - lfc)
            pair = jnp.logical_and(same_block, jnp.logical_and(t_hi, jnp.logical_not(s_hi)))
        att = att + jnp.where(pair, _dot_nt(qe.astype(BF16), ke.astype(BF16)), 0.0)
        half *= 2
    return att


def _hgrn2_kernel(u_ref, w_ref, lb_ref, ng_ref, o_ref,
                  q_s, v_s, k_s, lf_s, b_s, qe_s, ke_s, qi_s, kd_s, dec_s, upd_s, st_s, acc_s,
                  *, seq, chunk):
    n_chunks = seq // chunk
    proj = _dot(u_ref[...], w_ref[...])
    q_s[...] = _silu(proj[:, 0:HEAD])
    v_s[...] = proj[:, 3 * HEAD:4 * HEAD].astype(BF16)
    o_ref[...] = proj[:, 4 * HEAD:5 * HEAD]

    for d in range(2):
        f_raw = proj[:, (1 + d) * HEAD:(2 + d) * HEAD]
        lb = lb_ref[d:d + 1, :]
        sig_f = _sigmoid(f_raw)
        f = lb + (1.0 - lb) * sig_f
        k_s[d] = (1.0 - lb) * (1.0 - sig_f)
        log_f = jnp.log(jnp.maximum(f, TINY))
        lf_s[d] = log_f
        b = _chunk_scan(log_f, chunk, d == 1, "add")
        b_s[d] = b

    def ref_span(d):
        first, mid, last = (b_s[d, pl.ds(r, n_chunks, stride=chunk), :] for r in (0, chunk // 2, chunk - 1))
        return jnp.maximum(jnp.abs(first - mid), jnp.abs(mid - last))

    mild_decay = jnp.max(jnp.maximum(ref_span(0), ref_span(1))) <= HG_SAFE_DECAY

    def chunk_rows(c):
        r0 = pl.multiple_of(c * chunk, chunk)
        return r0, pl.ds(r0, chunk)

    for d in range(2):
        reverse = d == 1
        end_row = 0 if reverse else chunk - 1

        def scale_body(c, carry, d=d, end_row=end_row):
            r0, rows = chunk_rows(c)
            qc = q_s[rows, :]
            kc = k_s[d, rows, :]
            bc = b_s[d, rows, :]
            ref = b_s[d, pl.ds(r0 + chunk // 2, 1), :]
            bend = b_s[d, pl.ds(r0 + end_row, 1), :]
            qe_s[d, rows, :] = (qc * jnp.exp(bc - ref)).astype(BF16)
            ke_s[d, rows, :] = (kc * jnp.exp(ref - bc)).astype(BF16)
            qi_s[d, rows, :] = (qc * jnp.exp(bc)).astype(BF16)
            kd_s[rows, :] = (kc * jnp.exp(bend - bc)).astype(BF16)
            dec_s[c] = jnp.broadcast_to(jnp.exp(bend), (HEAD, HEAD)).T
            return carry

        lax.fori_loop(0, n_chunks, scale_body, 0, unroll=HG_UNROLL)

        for c in range(n_chunks):
            rows = slice(c * chunk, (c + 1) * chunk)
            upd_s[c] = _dot_tn(kd_s[rows, :], v_s[rows, :])

        st = jnp.zeros((HEAD, HEAD), F32)
        for c in (range(n_chunks - 1, -1, -1) if reverse else range(n_chunks)):
            st_s[d, c] = st.astype(BF16)
            st = st * dec_s[c] + upd_s[c]

    def emit(d, c, rows, att):
        o = _dot(jnp.concatenate([qi_s[d, rows, :], att.astype(BF16)], axis=1),
                 jnp.concatenate([st_s[d, c], v_s[rows, :]], axis=0))
        if d == 0:
            acc_s[rows, :] = o
        else:
            acc_s[rows, :] = acc_s[rows, :] + o

    @pl.when(mild_decay)
    def _():
        for d in range(2):
            mask = _tri_mask(chunk, d == 1)
            for g0 in range(0, n_chunks, HG_GROUP):
                group = [(c, slice(c * chunk, (c + 1) * chunk)) for c in range(g0, min(g0 + HG_GROUP, n_chunks))]
                atts = [jnp.where(mask, _dot_nt(qe_s[d, rows, :], ke_s[d, rows, :]), 0.0) for _, rows in group]
                for (c, rows), att in zip(group, atts):
                    emit(d, c, rows, att)

    @pl.when(jnp.logical_not(mild_decay))
    def _():
        for d in range(2):
            def safe_body(c, carry, d=d):
                _, rows = chunk_rows(c)
                emit(d, c, rows, _hg_att_safe(q_s[rows, :], k_s[d, rows, :], lf_s[d, rows, :], d == 1))
                return carry

            lax.fori_loop(0, n_chunks, safe_body, 0)

    o = acc_s[...]
    o = o * lax.rsqrt(jnp.mean(o * o, axis=-1, keepdims=True) + EPS) * ng_ref[...]
    o_ref[...] = o * _silu(o_ref[...])


def _hgrn2_call(u, w, lb, ng, *, chunk=HG_CHUNK):
    n_b, seq, d = u.shape
    kern = functools.partial(_hgrn2_kernel, seq=seq, chunk=chunk)
    return pl.pallas_call(
        kern,
        out_shape=jax.ShapeDtypeStruct((n_b, seq, HG_H * HEAD), F32),
        grid=(n_b, HG_H),
        in_specs=[pl.BlockSpec((None, seq, d), lambda b, h: (b, 0, 0)),
                  pl.BlockSpec((None, d, 5 * HEAD), lambda b, h: (h, 0, 0)),
                  pl.BlockSpec((None, 2, HEAD), lambda b, h: (h, 0, 0)),
                  pl.BlockSpec((None, 1, HEAD), lambda b, h: (h, 0, 0))],
        out_specs=pl.BlockSpec((None, seq, HEAD), lambda b, h: (b, 0, h)),
        scratch_shapes=[pltpu.VMEM((seq, HEAD), F32),
                        pltpu.VMEM((seq, HEAD), BF16),
                        pltpu.VMEM((2, seq, HEAD), F32),
                        pltpu.VMEM((2, seq, HEAD), F32),
                        pltpu.VMEM((2, seq, HEAD), F32),
                        pltpu.VMEM((2, seq, HEAD), BF16),
                        pltpu.VMEM((2, seq, HEAD), BF16),
                        pltpu.VMEM((2, seq, HEAD), BF16),
                        pltpu.VMEM((seq, HEAD), BF16),
                        pltpu.VMEM((seq // chunk, HEAD, HEAD), F32),
                        pltpu.VMEM((seq // chunk, HEAD, HEAD), F32),
                        pltpu.VMEM((2, seq // chunk, HEAD, HEAD), BF16),
                        pltpu.VMEM((seq, HEAD), F32)],
        compiler_params=pltpu.CompilerParams(dimension_semantics=("parallel", "arbitrary"),
                                             vmem_limit_bytes=VMEM_LIMIT),
        name="hgrn2_branch",
    )(u, w, lb, ng)


def _log_sigmoid(x):
    return jnp.minimum(x, 0.0) - jnp.log1p(jnp.exp(-jnp.abs(x)))


def _mlstm_kernel(u_ref, w_ref, bias_ref, ng_ref, o_ref,
                  qb_s, kb_s, va_s, tm_s, qw_s, kw_s, mx_s, en_s, gt_s, dec_s, upd_s, ct_s,
                  acc_s, *, seq, chunk):
    n_chunks = seq // chunk
    proj = _dot(u_ref[...], w_ref[...])
    q = proj[:, 0:HEAD]
    k = proj[:, HEAD:2 * HEAD] * (HEAD ** -0.5)
    qb_s[...] = q.astype(BF16)
    kb_s[...] = k.astype(BF16)
    va_s[:, 0:HEAD] = proj[:, 2 * HEAD:3 * HEAD].astype(BF16)
    va_s[:, HEAD:2 * HEAD] = jnp.ones((seq, HEAD), BF16)
    o_ref[...] = proj[:, 3 * HEAD:4 * HEAD]

    x = (proj[:, 4 * HEAD:5 * HEAD] + bias_ref[...]).T[0:SUBLANES, :]
    fwd_row = lax.broadcasted_iota(jnp.int32, (SUBLANES, seq), 0) == 0
    pos = lax.broadcasted_iota(jnp.int32, (SUBLANES, seq), 1)

    def scans(y, op, seg, start=1):
        return _lane_scan(y, seg, False, op, start), _lane_scan(y, seg, True, op, start)

    it = x
    log_f = pltpu.roll(_log_sigmoid(x), SUBLANES - 2, axis=0)
    pre, suf = scans(log_f, "add", chunk)
    bt = jnp.where(fwd_row, pre, suf)
    bt_end = pre + suf - log_f
    g = it - bt
    g_pre, g_suf = scans(g, "max", chunk)
    cm = jnp.where(fwd_row, g_pre, g_suf)
    cm_end = jnp.maximum(g_pre, g_suf)

    p_pre, p_suf = scans(bt_end, "add", seq, chunk)
    p_sum = jnp.where(fwd_row, p_pre, p_suf)
    x_pre, x_suf = scans(cm_end - (p_sum - bt_end), "max", seq, chunk)
    m_end = p_sum + jnp.where(fwd_row, x_pre, x_suf)
    m_prev = jnp.where(fwd_row,
                       jnp.where(pos < chunk, NEG, pltpu.roll(m_end, chunk, axis=1)),
                       jnp.where(pos >= seq - chunk, NEG, pltpu.roll(m_end, seq - chunk, axis=1)))
    mx = jnp.maximum(m_prev, cm)
    w_inter = jnp.exp(m_prev - mx)
    wk = jnp.exp(bt_end - bt + it - m_end)
    en = jnp.exp(-(bt + mx))
    decay = jnp.exp(bt_end + m_prev - m_end)

    for c in range(n_chunks):
        gt_s[c] = g[:, c * chunk:(c + 1) * chunk]
    row = lax.broadcasted_iota(jnp.int32, (SUBLANES, seq), 0)
    packed = jnp.where(row < 2, w_inter,
                       jnp.where(row < 4, pltpu.roll(wk, 2, axis=0),
                                 jnp.where(row < 6, pltpu.roll(mx, 4, axis=0), pltpu.roll(en, 6, axis=0))))
    tm_s[...] = jnp.concatenate(
        [packed, decay, jnp.zeros((LANES - 2 * SUBLANES, seq), F32)], axis=0).T
    for d in range(2):
        def lanes(col, ref_rows=slice(None)):
            block = tm_s[ref_rows, :]
            return jnp.broadcast_to(block[:, col:col + 1], (block.shape[0], LANES))
        qw_s[d] = (q * lanes(d)).astype(BF16)
        kw_s[d] = (k * lanes(2 + d)).astype(BF16)
        mx_s[d] = lanes(4 + d)
        en_s[d] = lanes(6 + d)
        dec_s[d] = lanes(8 + d, pl.ds(0, n_chunks, stride=chunk))

    for d in range(2):
        reverse = d == 1
        mask = _tri_mask(chunk, reverse)

        def chunk_rows(c):
            return pl.ds(pl.multiple_of(c * chunk, chunk), chunk)

        for c in range(n_chunks):
            rows = slice(c * chunk, (c + 1) * chunk)
            upd_s[c] = _dot_tn(kw_s[d, rows, :], va_s[rows, :])

        ct = jnp.zeros((HEAD, 2 * HEAD), F32)
        for c in (range(n_chunks - 1, -1, -1) if reverse else range(n_chunks)):
            ct_s[c] = ct.astype(BF16)
            dec = dec_s[d, c:c + 1, :]
            ct = ct * jnp.concatenate([dec, dec], axis=1) + upd_s[c]

        def scores(c, rows):
            p = jnp.exp(jnp.where(mask, gt_s[c][d:d + 1, :] - mx_s[d, rows, :], NEG))
            return (_dot_nt(qb_s[rows, :], kb_s[rows, :]) * p).astype(BF16)

        def emit(c, rows, sc):
            tot = _dot(jnp.concatenate([qw_s[d, rows, :], sc], axis=1),
                       jnp.concatenate([ct_s[c], va_s[rows, :]], axis=0))
            hval = tot[:, 0:HEAD] / jnp.maximum(jnp.abs(tot[:, HEAD:2 * HEAD]), en_s[d, rows, :])
            if d == 0:
                acc_s[rows, :] = hval
            else:
                acc_s[rows, :] = acc_s[rows, :] + hval

        for g0 in range(0, n_chunks, ML_GROUP):
            group = [(c, slice(c * chunk, (c + 1) * chunk)) for c in range(g0, min(g0 + ML_GROUP, n_chunks))]
            staged = [scores(c, rows) for c, rows in group]
            for (c, rows), sc in zip(group, staged):
                emit(c, rows, sc)

    hsum = acc_s[...]
    hc = hsum - jnp.mean(hsum, axis=-1, keepdims=True)
    hn = hc * lax.rsqrt(jnp.mean(hc * hc, axis=-1, keepdims=True) + EPS) * ng_ref[...]
    o_ref[...] = hn * _sigmoid(o_ref[...])


def _mlstm_call(u, w, bias, ng, *, chunk=ML_CHUNK):
    n_b, seq, d = u.shape
    n_chunks = seq // chunk
    kern = functools.partial(_mlstm_kernel, seq=seq, chunk=chunk)
    return pl.pallas_call(
        kern,
        out_shape=jax.ShapeDtypeStruct((n_b, seq, ML_H * HEAD), F32),
        grid=(n_b, ML_H),
        in_specs=[pl.BlockSpec((None, seq, d), lambda b, h: (b, 0, 0)),
                  pl.BlockSpec((None, d, 5 * HEAD), lambda b, h: (h, 0, 0)),
                  pl.BlockSpec((None, 1, HEAD), lambda b, h: (h, 0, 0)),
                  pl.BlockSpec((None, 1, HEAD), lambda b, h: (h, 0, 0))],
        out_specs=pl.BlockSpec((None, seq, HEAD), lambda b, h: (b, 0, h)),
        scratch_shapes=[pltpu.VMEM((seq, HEAD), BF16),
                        pltpu.VMEM((seq, HEAD), BF16),
                        pltpu.VMEM((seq, 2 * HEAD), BF16),
                        pltpu.VMEM((seq, LANES), F32),
                        pltpu.VMEM((2, seq, HEAD), BF16),
                        pltpu.VMEM((2, seq, HEAD), BF16),
                        pltpu.VMEM((2, seq, LANES), F32),
                        pltpu.VMEM((2, seq, LANES), F32),
                        pltpu.VMEM((n_chunks, SUBLANES, chunk), F32),
                        pltpu.VMEM((2, n_chunks, LANES), F32),
                        pltpu.VMEM((n_chunks, HEAD, 2 * HEAD), F32),
                        pltpu.VMEM((n_chunks, HEAD, 2 * HEAD), BF16),
                        pltpu.VMEM((seq, HEAD), F32)],
        compiler_params=pltpu.CompilerParams(dimension_semantics=("parallel", "arbitrary"),
                                             vmem_limit_bytes=VMEM_LIMIT),
        name="mlstm_branch",
    )(u, w, bias, ng)


_SSD_W_Z = 0
_SSD_W_XBC = SSD_GW
_SSD_W_DT = SSD_GW + SSD_GW + 2 * SSD_N
_SSD_W_END = _SSD_W_DT + LANES
_SSD_XBC_G = SSD_GW + 2 * SSD_N
_SSD_PAD = SUBLANES
_SSD_PAIRS = SSD_HG // 2


def _ssd_kernel(u_ref, w_ref, cw_ref, cb_ref, dtb_ref, alog_ref, dsk_ref, ng_ref, o_ref,
                pad_s, xs_s, b_s, c_s, a_s, at_s, dtt_s, cbm_s, ea_s, xw_s, dec_s, cur_s, st_s,
                *, seq, chunk, rblk):
    n_chunks = seq // chunk
    n_rblk = seq // rblk

    zero_rows = jnp.zeros((_SSD_PAD, _SSD_XBC_G), F32)
    pad_s[0:_SSD_PAD, :] = zero_rows
    pad_s[_SSD_PAD + seq:2 * _SSD_PAD + seq, :] = zero_rows
    for i in range(n_rblk):
        r0 = i * rblk
        pad_s[_SSD_PAD + r0:_SSD_PAD + r0 + rblk, :] = _dot(
            u_ref[r0:r0 + rblk, :], w_ref[:, _SSD_W_XBC:_SSD_W_DT])
    for i in range(n_rblk):
        r0 = i * rblk
        acc = cb_ref[...]
        slab = pad_s[r0:r0 + rblk + 2 * _SSD_PAD, :]
        for j in range(SSD_CONV):
            tap = slab if j == SSD_CONV // 2 else pltpu.roll(
                slab, (SSD_CONV // 2 - j) % (rblk + 2 * _SSD_PAD), axis=0)
            acc = acc + tap[_SSD_PAD:_SSD_PAD + rblk, :] * cw_ref[j:j + 1, :]
        xbc = _silu(acc)
        xs_s[r0:r0 + rblk, :] = xbc[:, 0:SSD_GW]
        b_s[r0:r0 + rblk, :] = xbc[:, SSD_GW:SSD_GW + SSD_N].astype(BF16)
        c_s[r0:r0 + rblk, :] = xbc[:, SSD_GW + SSD_N:SSD_GW + 2 * SSD_N].astype(BF16)

    n_hd = 2 * SSD_HG
    dt_tok = _softplus(_dot(u_ref[...], w_ref[:, _SSD_W_DT:_SSD_W_END]) + dtb_ref[...])
    dt = dt_tok.T[0:n_hd, :]
    log_a = (dt_tok * (-jnp.exp(alog_ref[...]))).T[0:n_hd, :]
    fwd_row = lax.broadcasted_iota(jnp.int32, (n_hd, seq), 0) < SSD_HG
    pre = _lane_scan(log_a, chunk, False, "add")
    suf = _lane_scan(log_a, chunk, True, "add")
    acum = jnp.where(fwd_row, pre, suf)
    a_end = pre + suf - log_a
    for c in range(n_chunks):
        lanes_c = slice(c * chunk, (c + 1) * chunk)
        at_s[c] = acum[:, lanes_c]
        dtt_s[c] = dt[:, lanes_c]
        r0 = c * chunk
        cbm_s[r0:r0 + chunk, :] = _dot_nt(c_s[r0:r0 + chunk, :], b_s[r0:r0 + chunk, :])
    a_s[...] = jnp.concatenate(
        [acum, jnp.exp(acum), dt * jnp.exp(a_end - acum), jnp.exp(a_end),
         jnp.zeros((LANES - 4 * n_hd, seq), F32)], axis=0).T
    lane_lo = lax.broadcasted_iota(jnp.int32, (chunk, LANES), 1) < SSD_HEAD_DIM

    def split3(x):
        hi = x.astype(BF16)
        r1 = x - hi.astype(F32)
        mid = r1.astype(BF16)
        return [hi, mid, (r1 - mid.astype(F32)).astype(BF16)]

    def spread(first_lane, pieces):
        one = (lax.broadcasted_iota(jnp.int32, (LANES, SSD_GW), 1) // SSD_HEAD_DIM
               == lax.broadcasted_iota(jnp.int32, (LANES, SSD_GW), 0) - first_lane).astype(BF16)
        return jnp.concatenate([one] * pieces, axis=0)

    for d in range(2):
        reverse = d == 1
        mask = _tri_mask(chunk, reverse)
        spread_ea = spread(n_hd + d * SSD_HG, 1)
        spread_wt = spread(2 * n_hd + d * SSD_HG, 2)
        dec_s[...] = _dot(jnp.concatenate(split3(a_s[pl.ds(0, n_chunks, stride=chunk), :]), axis=1),
                          spread(3 * n_hd + d * SSD_HG, 3))
        for i in range(n_rblk):
            rows = slice(i * rblk, (i + 1) * rblk)
            hi, mid, _ = split3(a_s[rows, :])
            ea_s[rows, :] = _dot(hi, spread_ea).astype(BF16)
            xw_s[rows, :] = (xs_s[rows, :] * _dot(jnp.concatenate([hi, mid], axis=1), spread_wt)).astype(BF16)

        def chunk_rows(c):
            return pl.ds(pl.multiple_of(c * chunk, chunk), chunk)

        for c in range(n_chunks):
            rows = slice(c * chunk, (c + 1) * chunk)
            pad_s[rows, 0:SSD_GW] = _dot_tn(b_s[rows, :], xw_s[rows, :])

        cur_s[...] = jnp.zeros_like(cur_s)

        for c in (range(n_chunks - 1, -1, -1) if reverse else range(n_chunks)):
            cur = cur_s[...]
            st_s[c] = cur.astype(BF16)
            cur_s[...] = cur * dec_s[c:c + 1, :] + pad_s[c * chunk:(c + 1) * chunk, 0:SSD_GW]

        def out_body(c, carry, mask=mask, d=d):
            rows = chunk_rows(c)
            blk = a_s[rows, :]
            blk_t = at_s[c]
            dtt = dtt_s[c]
            cbm = cbm_s[rows, :]
            y_inter = _dot(c_s[rows, :], st_s[c])
            for p in range(_SSD_PAIRS):
                cols = slice(p * LANES, (p + 1) * LANES)
                xs_pair = xs_s[rows, cols]
                m_tiles, x_tiles = [], []
                for hh in range(2):
                    col = d * SSD_HG + 2 * p + hh
                    seg = jnp.exp(jnp.where(mask, blk[:, col:col + 1] - blk_t[col:col + 1, :], NEG))
                    m_tiles.append((cbm * seg * dtt[col:col + 1, :]).astype(BF16))
                    keep = lane_lo if hh == 0 else jnp.logical_not(lane_lo)
                    x_tiles.append(jnp.where(keep, xs_pair, 0.0).astype(BF16))
                y = (y_inter[:, cols] * ea_s[rows, cols].astype(F32)
                     + _dot(jnp.concatenate(m_tiles, axis=1), jnp.concatenate(x_tiles, axis=0)))
                if d == 0:
                    o_ref[rows, cols] = y
                else:
                    o_ref[rows, cols] = o_ref[rows, cols] + y
            return carry

        lax.fori_loop(0, n_chunks, out_body, 0, unroll=SSD_UNROLL)

    for i in range(n_rblk):
        r0 = i * rblk
        rows = slice(r0, r0 + rblk)
        z = _dot(u_ref[rows, :], w_ref[:, _SSD_W_Z:_SSD_W_XBC])
        y = (o_ref[rows, :] + xs_s[rows, :] * dsk_ref[...]) * _silu(z)
        o_ref[rows, :] = y * lax.rsqrt(jnp.mean(y * y, axis=-1, keepdims=True) + EPS) * ng_ref[...]


def _ssd_call(u, w, cw, cb, dtb, alog, dsk, ng, *, chunk=SSD_CHUNK):
    n_b, seq, d = u.shape
    rblk = min(seq, 512)
    kern = functools.partial(_ssd_kernel, seq=seq, chunk=chunk, rblk=rblk)
    vec = lambda n: pl.BlockSpec((None, 1, n), lambda b, g: (g, 0, 0))
    return pl.pallas_call(
        kern,
        out_shape=jax.ShapeDtypeStruct((n_b, seq, SSD_G * SSD_GW), F32),
        grid=(n_b, SSD_G),
        in_specs=[pl.BlockSpec((None, seq, d), lambda b, g: (b, 0, 0)),
                  pl.BlockSpec((None, d, _SSD_W_END), lambda b, g: (g, 0, 0)),
                  pl.BlockSpec((None, SSD_CONV, _SSD_XBC_G), lambda b, g: (g, 0, 0)),
                  vec(_SSD_XBC_G), vec(LANES), vec(LANES), vec(SSD_GW), vec(SSD_GW)],
        out_specs=pl.BlockSpec((None, seq, SSD_GW), lambda b, g: (b, 0, g)),
        scratch_shapes=[pltpu.VMEM((seq + 2 * _SSD_PAD, _SSD_XBC_G), F32),
                        pltpu.VMEM((seq, SSD_GW), F32),
                        pltpu.VMEM((seq, SSD_N), BF16),
                        pltpu.VMEM((seq, SSD_N), BF16),
                        pltpu.VMEM((seq, LANES), F32),
                        pltpu.VMEM((seq // chunk, 2 * SSD_HG, chunk), F32),
                        pltpu.VMEM((seq // chunk, 2 * SSD_HG, chunk), F32),
                        pltpu.VMEM((seq, chunk), F32),
                        pltpu.VMEM((seq, SSD_GW), BF16),
                        pltpu.VMEM((seq, SSD_GW), BF16),
                        pltpu.VMEM((seq // chunk, SSD_GW), F32),
                        pltpu.VMEM((SSD_N, SSD_GW), F32),
                        pltpu.VMEM((seq // chunk, SSD_N, SSD_GW), BF16)],
        compiler_params=pltpu.CompilerParams(dimension_semantics=("parallel", "arbitrary"),
                                             vmem_limit_bytes=VMEM_LIMIT),
        name="ssd_branch",
    )(u, w, cw, cb, dtb, alog, dsk, ng)


def _merge_kernel(u_ref, h_ref, y0_ref, y1_ref, y2_ref, wg_ref, wb_ref, wo_ref, o_ref):
    u = u_ref[...]
    merged = None
    for r, y_ref in enumerate((y0_ref, y1_ref, y2_ref)):
        term = _sigmoid(_dot(u, wg_ref[r])) * _dot(y_ref[...].astype(BF16), wb_ref[r])
        merged = term if merged is None else merged + term
    o_ref[...] = h_ref[...] + _dot(merged.astype(BF16), wo_ref[...])


def _merge_call(u2d, h2d, y0, y1, y2, wg, wb, wo, tm=512):
    n_tok, d = h2d.shape
    tm = min(tm, n_tok)
    tok = lambda: pl.BlockSpec((tm, d), lambda i: (i, 0))
    held = lambda shape: pl.BlockSpec(shape, lambda i: tuple(0 for _ in shape), pipeline_mode=pl.Buffered(1))
    return pl.pallas_call(
        _merge_kernel,
        out_shape=jax.ShapeDtypeStruct((n_tok, d), F32),
        grid=(n_tok // tm,),
        in_specs=[tok(), tok(), tok(), tok(), tok(),
                  held((3, d, d)), held((3, d, d)), held((d, d))],
        out_specs=tok(),
        compiler_params=pltpu.CompilerParams(dimension_semantics=("parallel",),
                                             vmem_limit_bytes=VMEM_LIMIT),
        name="merge",
    )(u2d, h2d, y0, y1, y2, wg, wb, wo)


_FFN_CW = 256
_FFN_NJ = D_FF // _FFN_CW
_FFN_HALO = SUBLANES

def _ffn_kernel(h_ref, hp_ref, hn_ref, p_ref, g_ref, wa_ref, wv_ref, cwa_ref, cwv_ref,
                cba_ref, cbv_ref, wd_ref, wple_ref, wpg_ref, fg_ref, o_ref, un_ref, u_s,
                *, tm, n_tiles, final_norm):
    i = pl.program_id(1)
    j = pl.program_id(2)

    def norm(x):
        return (x * lax.rsqrt(jnp.mean(x * x, axis=-1, keepdims=True) + EPS) * g_ref[...]).astype(BF16)

    @pl.when(j == 0)
    def _():
        up = jnp.where(i > 0, 1.0, 0.0)
        dn = jnp.where(i < n_tiles - 1, 1.0, 0.0)
        u_s[0:_FFN_HALO, :] = norm(hp_ref[...] * up)
        u_s[_FFN_HALO:_FFN_HALO + tm, :] = norm(h_ref[...])
        u_s[_FFN_HALO + tm:2 * _FFN_HALO + tm, :] = norm(hn_ref[...] * dn)
        o_ref[...] = h_ref[...]

    def conv(w_ref, cw_ref, cb_ref):
        up = _dot(u_s[...], w_ref[...])
        n_rows = tm + 2 * _FFN_HALO
        acc = cb_ref[...]
        for t in range(FFN_CONV):
            tap = up if t == FFN_CONV // 2 else pltpu.roll(up, (FFN_CONV // 2 - t) % n_rows, axis=0)
            acc = acc + tap[_FFN_HALO:_FFN_HALO + tm, :] * cw_ref[t:t + 1, :]
        return acc

    act = (_silu(conv(wa_ref, cwa_ref, cba_ref)) * conv(wv_ref, cwv_ref, cbv_ref)).astype(BF16)
    o_ref[...] += _dot(act, wd_ref[...])

    @pl.when(j == _FFN_NJ - 1)
    def _():
        h2 = o_ref[...]
        ple = _dot(p_ref[...].astype(BF16), wple_ref[...])
        h3 = h2 + ple * _sigmoid(_dot(h2.astype(BF16), wpg_ref[...]))
        normed = h3 * lax.rsqrt(jnp.mean(h3 * h3, axis=-1, keepdims=True) + EPS) * fg_ref[...]
        o_ref[...] = normed if final_norm else h3
        un_ref[...] = normed.astype(un_ref.dtype)


def _ffn_call(h, p, g, w_up, cw, cb, w_down, w_ple, w_pg, fg, *, final_norm, tm=1024):
    n_b, seq, d = h.shape
    tm = min(tm, seq)
    n_tiles = seq // tm
    hb = tm // _FFN_HALO
    n_hb = seq // _FFN_HALO
    kern = functools.partial(_ffn_kernel, tm=tm, n_tiles=n_tiles, final_norm=final_norm)
    cst = lambda shape: pl.BlockSpec(shape, lambda b, i, j: tuple(0 for _ in shape))
    return pl.pallas_call(
        kern,
        out_shape=(jax.ShapeDtypeStruct((n_b, seq, d), F32), jax.ShapeDtypeStruct((n_b, seq, d), BF16)),
        grid=(n_b, n_tiles, _FFN_NJ),
        in_specs=[pl.BlockSpec((None, tm, d), lambda b, i, j: (b, i, 0)),
                  pl.BlockSpec((None, _FFN_HALO, d), lambda b, i, j: (b, jnp.maximum(i * hb - 1, 0), 0)),
                  pl.BlockSpec((None, _FFN_HALO, d), lambda b, i, j: (b, jnp.minimum((i + 1) * hb, n_hb - 1), 0)),
                  pl.BlockSpec((None, tm, PLE_DIM), lambda b, i, j: (b, i, 0)),
                  cst((1, d)),
                  pl.BlockSpec((d, _FFN_CW), lambda b, i, j: (0, j)),
                  pl.BlockSpec((d, _FFN_CW), lambda b, i, j: (0, _FFN_NJ + j)),
                  pl.BlockSpec((FFN_CONV, _FFN_CW), lambda b, i, j: (0, j)),
                  pl.BlockSpec((FFN_CONV, _FFN_CW), lambda b, i, j: (0, _FFN_NJ + j)),
                  pl.BlockSpec((1, _FFN_CW), lambda b, i, j: (0, j)),
                  pl.BlockSpec((1, _FFN_CW), lambda b, i, j: (0, _FFN_NJ + j)),
                  pl.BlockSpec((_FFN_CW, d), lambda b, i, j: (j, 0)),
                  cst((PLE_DIM, d)), cst((d, d)), cst((1, d))],
        out_specs=(pl.BlockSpec((None, tm, d), lambda b, i, j: (b, i, 0)),
                   pl.BlockSpec((None, tm, d), lambda b, i, j: (b, i, 0))),
        scratch_shapes=[pltpu.VMEM((tm + 2 * _FFN_HALO, d), BF16)],
        compiler_params=pltpu.CompilerParams(
            dimension_semantics=("parallel", "parallel", "arbitrary"),
            vmem_limit_bytes=VMEM_LIMIT),
        name="convffn_ple",
    )(h, h, h, p, g.reshape(1, d), w_up, w_up, cw, cw, cb.reshape(1, -1), cb.reshape(1, -1),
      w_down, w_ple, w_pg, fg.reshape(1, d))


def _cols(w, start, width):
    return lax.slice_in_dim(w, start, start + width, axis=1)


def _pad_cols(w, width):
    return jnp.pad(w, ((0, 0), (0, width - w.shape[1])))


def _ssd_params(w_in, conv_w, conv_b, dt_bias, a_log, d_skip, norm_g):
    ws, cws, cbs, dtbs, alogs = [], [], [], [], []
    for g in range(SSD_G):
        dt_cols = jnp.concatenate(
            [_cols(w_in, _OFF_DT + dd * SSD_H + g * SSD_HG, SSD_HG) for dd in range(2)], axis=1)
        ws.append(jnp.concatenate([
            _cols(w_in, _OFF_Z + g * SSD_GW, SSD_GW),
            _cols(w_in, _OFF_XBC + g * SSD_GW, SSD_GW),
            _cols(w_in, _OFF_XBC + D_MODEL + g * SSD_N, SSD_N),
            _cols(w_in, _OFF_XBC + D_MODEL + SSD_G * SSD_N + g * SSD_N, SSD_N),
            _pad_cols(dt_cols, LANES)], axis=1))
        pick = lambda a: jnp.concatenate([
            _cols(a, g * SSD_GW, SSD_GW),
            _cols(a, D_MODEL + g * SSD_N, SSD_N),
            _cols(a, D_MODEL + SSD_G * SSD_N + g * SSD_N, SSD_N)], axis=1)
        cws.append(pick(conv_w))
        cbs.append(pick(conv_b.reshape(1, -1)))
        head_row = lambda a: _pad_cols(
            jnp.concatenate([a[dd, g * SSD_HG:(g + 1) * SSD_HG] for dd in range(2)]).reshape(1, -1), LANES)
        dtbs.append(head_row(dt_bias))
        alogs.append(head_row(a_log))
    dsk = jnp.repeat(d_skip, SSD_HEAD_DIM).reshape(SSD_G, 1, SSD_GW)
    return (jnp.stack(ws).astype(BF16), jnp.stack(cws), jnp.stack(cbs), jnp.stack(dtbs),
            jnp.stack(alogs), dsk, norm_g.reshape(SSD_G, 1, SSD_GW))


def _hgrn2_params(w_in, lb, norm_g):
    ws = [jnp.concatenate([
        _cols(w_in, _OFF_GQ + h * HEAD, HEAD),
        _cols(w_in, _OFF_GF + h * HEAD, HEAD),
        _cols(w_in, _OFF_GF + D_MODEL + h * HEAD, HEAD),
        _cols(w_in, _OFF_GI + h * HEAD, HEAD),
        _cols(w_in, _OFF_GG + h * HEAD, HEAD)], axis=1) for h in range(HG_H)]
    lbs = lb.reshape(2, HG_H, HEAD).swapaxes(0, 1)
    return jnp.stack(ws).astype(BF16), lbs, norm_g.reshape(HG_H, 1, HEAD)


def _mlstm_params(w_in, i_bias, f_bias, norm_g):
    ws, biases = [], []
    for h in range(ML_H):
        gate_cols = jnp.concatenate([
            _cols(w_in, _OFF_MI + h, 1), _cols(w_in, _OFF_MI + ML_H + h, 1),
            _cols(w_in, _OFF_MF + h, 1), _cols(w_in, _OFF_MF + ML_H + h, 1)], axis=1)
        ws.append(jnp.concatenate([
            _cols(w_in, _OFF_MQ + h * HEAD, HEAD),
            _cols(w_in, _OFF_MK + h * HEAD, HEAD),
            _cols(w_in, _OFF_MV + h * HEAD, HEAD),
            _cols(w_in, _OFF_MO + h * HEAD, HEAD),
            _pad_cols(gate_cols, LANES)], axis=1))
        biases.append(_pad_cols(
            jnp.stack([i_bias[0, h], i_bias[1, h], f_bias[0, h], f_bias[1, h]]).reshape(1, 4), LANES))
    return jnp.stack(ws).astype(BF16), jnp.stack(biases), norm_g.reshape(ML_H, 1, HEAD)


def kernel(x, p, norm_mix_g, w_in, ssd_conv_w, ssd_conv_b, ssd_dt_bias, ssd_a_log, ssd_d, ssd_norm_g, hg_lb_raw, hg_norm_g, ml_i_bias, ml_f_bias, ml_norm_g, w_br_ssd, w_br_hg, w_br_ml, w_out, norm_ffn_g, w_up, ffn_conv_w, ffn_conv_b, w_down, w_ple, w_ple_gate, final_norm_g):
    n_b, seq, d = x.shape
    depth = w_in.shape[0]
    lb_soft = jax.nn.softmax(hg_lb_raw.astype(F32), axis=0)
    hg_lb = jnp.cumsum(lb_soft, axis=0) - lb_soft[0:1]
    h = x
    u = _rmsnorm_call(x.reshape(n_b * seq, d), norm_mix_g[0], BF16).reshape(n_b, seq, d)
    for l in range(depth):
        u2d = u.reshape(n_b * seq, d)
        y_ssd = _ssd_call(u, *_ssd_params(w_in[l], ssd_conv_w[l], ssd_conv_b[l], ssd_dt_bias[l],
                                          ssd_a_log[l], ssd_d[l], ssd_norm_g[l]))
        y_hg = _hgrn2_call(u, *_hgrn2_params(w_in[l], hg_lb[l], hg_norm_g[l]))
        y_ml = _mlstm_call(u, *_mlstm_params(w_in[l], ml_i_bias[l], ml_f_bias[l], ml_norm_g[l]))
        wg = jnp.stack([_cols(w_in[l], _OFF_GATES + r * d, d) for r in range(3)]).astype(BF16)
        wb = jnp.stack([w_br_ssd[l], w_br_hg[l], w_br_ml[l]]).astype(BF16)
        tok = lambda a: a.reshape(n_b * seq, d)
        h = _merge_call(u2d, tok(h), tok(y_ssd), tok(y_hg), tok(y_ml), wg, wb,
                        w_out[l].astype(BF16)).reshape(n_b, seq, d)
        last = l == depth - 1
        h, u = _ffn_call(h, p[l], norm_ffn_g[l], w_up[l].astype(BF16), ffn_conv_w[l], ffn_conv_b[l],
                         w_down[l].astype(BF16), w_ple[l].astype(BF16), w_ple_gate[l].astype(BF16),
                         final_norm_g if last else norm_mix_g[l + 1], final_norm=last)
    return h
```
